```python
import math
import jax, jax.numpy as jnp
from jax import lax
import numpy as np

D_MODEL = 1024
BATCH = 8
SEQ = 2048
DEPTH = 2
DEC_BATCH = 32
DEC_SEQ = 8
PAST_LEN = 8192
PAGE_SIZE = 128

N_MIXERS = 2
N_LAYERS_A = (DEPTH + 1) // 2
N_LAYERS_B = DEPTH // 2
H_A = 8
HD_A = 64
A_IN = 2 * H_A * HD_A + 2 * H_A * HD_A + H_A * 2 * HD_A
H_B = 16
KV_B = 4
HD_B = 64
H_I = 8
D_I = 64
IDX_ROT = D_I // 2
TOPK_MAX = 256
B_IN = H_B * HD_B + 2 * KV_B * HD_B + H_I * D_I + D_I + H_I
ROPE_THETA = 500000.0
ROT_FRAC = 4
N_EXPERTS = 16
N_GROUPS = 4
E_PER_GROUP = N_EXPERTS // N_GROUPS
TOP_K = 2
D_FF = 512
P_DIM = 256
ALPHA = (2 * DEPTH) ** 0.25
BETA = (8 * DEPTH) ** -0.25
LN_EPS = 1e-5
RMS_EPS = 1e-5
Q_BLOCK = 128

kernel_name = 'hybrid_diffattn_dsa_moe_step'

F32 = jnp.float32


def _layer_norm(x, g, b):
    xf = x.astype(F32)
    mu = jnp.mean(xf, axis=-1, keepdims=True)
    var = jnp.mean(jnp.square(xf - mu), axis=-1, keepdims=True)
    return ((xf - mu) * lax.rsqrt(var + LN_EPS) * g.astype(F32) + b.astype(F32)).astype(x.dtype)


def _rope(x, pos, rot_dim):
    half = rot_dim // 2
    inv_freq = ROPE_THETA ** (-jnp.arange(half, dtype=F32) / half)
    ang = pos.astype(F32)[:, None] * inv_freq[None, :]
    cos = jnp.cos(ang)[:, None, :]
    sin = jnp.sin(ang)[:, None, :]
    xr = x[..., :rot_dim].astype(F32)
    x1, x2 = xr[..., :half], xr[..., half:]
    rot = jnp.concatenate([x1 * cos - x2 * sin, x2 * cos + x1 * sin], axis=-1)
    return jnp.concatenate([rot.astype(x.dtype), x[..., rot_dim:]], axis=-1)


def _block(n):
    return Q_BLOCK if n % Q_BLOCK == 0 else n


def _to_blocks(x, qb):
    b, n = x.shape[:2]
    return x.reshape(b, n // qb, qb, *x.shape[2:]).swapaxes(0, 1)


def _from_blocks(y):
    nb, b, qb = y.shape[:3]
    return y.swapaxes(0, 1).reshape(b, nb * qb, *y.shape[3:])


def _gather_pages(pool, page_table):
    g = pool[page_table]
    return g.reshape(g.shape[0], g.shape[1] * g.shape[2], *g.shape[3:])


def _diff_attention(q, k, v, lam, q_pos):
    qb = _block(q.shape[1])
    k_pos = jnp.arange(k.shape[1])

    def one(blk):
        bq, bpos = blk
        mask = k_pos[None, :] <= bpos[:, None]
        s = jnp.einsum('bqhcd,bkhcd->bhcqk', bq, k).astype(F32) * (HD_A ** -0.5)
        p = jax.nn.softmax(jnp.where(mask, s, -jnp.inf), axis=-1)
        a = p[:, :, 0] - lam * p[:, :, 1]
        return jnp.einsum('bhqk,bkhe->bqhe', a.astype(v.dtype), v)

    out = lax.map(one, (_to_blocks(q, qb), q_pos.reshape(-1, qb)))
    return _from_blocks(out)


def _mixer_a(x, pos, past_k, past_v, w_in, w_out, lam_q1, lam_k1, lam_q2, lam_k2, subln, lam_init):
    b, n, _ = x.shape
    qkv = x @ w_in
    q, k, v = jnp.split(qkv, [2 * H_A * HD_A, 4 * H_A * HD_A], axis=-1)
    q = _rope(q.reshape(b, n, 2 * H_A, HD_A), pos, HD_A // ROT_FRAC)
    k = _rope(k.reshape(b, n, 2 * H_A, HD_A), pos, HD_A // ROT_FRAC)
    v = v.reshape(b, n, H_A, 2 * HD_A)
    k_all = k if past_k is None else jnp.concatenate([past_k, k], axis=1)
    v_all = v if past_v is None else jnp.concatenate([past_v, v], axis=1)
    lam = (jnp.exp(jnp.sum(lam_q1.astype(F32) * lam_k1.astype(F32)))
           - jnp.exp(jnp.sum(lam_q2.astype(F32) * lam_k2.astype(F32))) + lam_init)
    o = _diff_attention(q.reshape(b, n, H_A, 2, HD_A), k_all.reshape(b, -1, H_A, 2, HD_A), v_all, lam, pos)
    of = o.astype(F32)
    of = of * lax.rsqrt(jnp.mean(of * of, axis=-1, keepdims=True) + RMS_EPS) * subln.astype(F32) * (1.0 - lam_init)
    y = of.astype(x.dtype).reshape(b, n, H_A * 2 * HD_A) @ w_out
    return y, k, v


def _dsa_attention(q, k, v, qi, ki, wi, q_pos):
    b, nq = q.shape[:2]
    nk = k.shape[1]
    n_sel = min(TOPK_MAX, nk // 4)
    qb = _block(nq)
    grp = H_B // KV_B
    k_pos = jnp.arange(nk)
    idx_scale = (H_I ** -0.5) * (D_I ** -0.5)

    def one(blk):
        bq, bqi, bwi, bpos = blk
        mask = k_pos[None, :] <= bpos[:, None]
        dots = jnp.einsum('bqhd,bkd->bqhk', bqi, ki).astype(F32)
        score = jnp.einsum('bqh,bqhk->bqk', bwi.astype(F32) * idx_scale, jax.nn.relu(dots))
        score = jnp.where(mask[None], score, -jnp.inf)
        _, idx = lax.top_k(score, n_sel)
        valid = k_pos[idx] <= bpos[None, :, None]
        ks = jax.vmap(lambda kk, ii: kk[ii])(k, idx)
        vs = jax.vmap(lambda vv, ii: vv[ii])(v, idx)
        s = jnp.einsum('bqhgd,bqkhd->bqhgk', bq.reshape(b, qb, KV_B, grp, HD_B), ks).astype(F32) * (HD_B ** -0.5)
        p = jax.nn.softmax(jnp.where(valid[:, :, None, None, :], s, -jnp.inf), axis=-1)
        o = jnp.einsum('bqhgk,bqkhd->bqhgd', p.astype(vs.dtype), vs)
        return o.reshape(b, qb, H_B, HD_B)

    out = lax.map(one, (_to_blocks(q, qb), _to_blocks(qi, qb), _to_blocks(wi, qb), q_pos.reshape(-1, qb)))
    return _from_blocks(out)


def _mixer_b(x, pos, past_k, past_v, past_ki, w_in, w_out):
    b, n, _ = x.shape
    sizes = (H_B * HD_B, KV_B * HD_B, KV_B * HD_B, H_I * D_I, D_I, H_I)
    offs = [sum(sizes[:m]) for m in range(1, len(sizes))]
    q, k, v, qi, ki, wi = jnp.split(x @ w_in, offs, axis=-1)
    q = _rope(q.reshape(b, n, H_B, HD_B), pos, HD_B // ROT_FRAC)
    k = _rope(k.reshape(b, n, KV_B, HD_B), pos, HD_B // ROT_FRAC)
    v = v.reshape(b, n, KV_B, HD_B)
    qi = _rope(qi.reshape(b, n, H_I, D_I), pos, IDX_ROT)
    ki = _rope(ki.reshape(b, n, 1, D_I), pos, IDX_ROT)[:, :, 0]
    k_all = k if past_k is None else jnp.concatenate([past_k, k], axis=1)
    v_all = v if past_v is None else jnp.concatenate([past_v, v], axis=1)
    ki_all = ki if past_ki is None else jnp.concatenate([past_ki, ki], axis=1)
    o = _dsa_attention(q, k_all, v_all, qi, ki_all, wi, pos)
    y = o.reshape(b, n, H_B * HD_B) @ w_out
    return y, k, v, ki


def _moe(x, router_w, router_b, w_gate, w_up, w_down):
    b, n, d = x.shape
    t = x.reshape(b * n, d)
    aff = jax.nn.sigmoid((t @ router_w).astype(F32))
    sel = aff + router_b.astype(F32)
    grp_score = lax.top_k(sel.reshape(-1, N_GROUPS, E_PER_GROUP), TOP_K)[0].sum(-1)
    g = jnp.argmax(grp_score, axis=-1)
    in_group = (jnp.arange(N_EXPERTS) // E_PER_GROUP)[None, :] == g[:, None]
    _, idx = lax.top_k(jnp.where(in_group, sel, -jnp.inf), TOP_K)
    gate = jnp.take_along_axis(aff, idx, axis=-1)
    gate = gate / jnp.sum(gate, axis=-1, keepdims=True)
    combine = jnp.sum(jax.nn.one_hot(idx, N_EXPERTS, dtype=F32) * gate[..., None], axis=1).astype(t.dtype)
    out = jnp.zeros_like(t)
    for e in range(N_EXPERTS):
        hdn = jax.nn.silu(t @ w_gate[e]) * (t @ w_up[e])
        out = out + combine[:, e:e + 1] * (hdn @ w_down[e])
    return out.reshape(b, n, d)


def _trunk(x, p, pos, past_a, past_b, a_w_in, a_w_out, a_lam_q1, a_lam_k1, a_lam_q2, a_lam_k2, a_subln,
           b_w_in, b_w_out, ln_g, ln_b, router_w, router_b, moe_w_gate, moe_w_up, moe_w_down,
           ple_proj, ple_gate_w, ple_gate_b):
    new_a, new_b = [], []
    for i in range(DEPTH):
        j = i // N_MIXERS
        if i % N_MIXERS == 0:
            pk, pv = (None, None) if past_a is None else past_a[j]
            lam_init = 0.8 - 0.6 * math.exp(-0.3 * i)
            h, k, v = _mixer_a(x, pos, pk, pv, a_w_in[j], a_w_out[j], a_lam_q1[j], a_lam_k1[j],
                               a_lam_q2[j], a_lam_k2[j], a_subln[j], lam_init)
            new_a.append((k, v))
        else:
            pk, pv, pki = (None, None, None) if past_b is None else past_b[j]
            h, k, v, ki = _mixer_b(x, pos, pk, pv, pki, b_w_in[j], b_w_out[j])
            new_b.append((k, v, ki))
        x = _layer_norm(ALPHA * x + h, ln_g[i, 0], ln_b[i, 0])
        x = _layer_norm(ALPHA * x + _moe(x, router_w, router_b, moe_w_gate[i], moe_w_up[i], moe_w_down[i]),
                        ln_g[i, 1], ln_b[i, 1])
        x = x + jax.nn.sigmoid(x @ ple_gate_w[i] + ple_gate_b[i]) * (p[i] @ ple_proj[i])
    return x, new_a, new_b


def setup_inputs(seed: int = 0) -> dict:
    key = jax.random.key(seed)
    ks = jax.random.split(key, 32)
    nrm = lambda kk, shape, scale: jax.random.normal(kk, shape, F32) * scale
    n_pages = PAST_LEN // PAGE_SIZE
    n_used = DEC_BATCH * n_pages
    n_pool = n_used + max(1, n_used // 4)
    page_table = jax.random.permutation(ks[0], n_pool)[:n_used].reshape(DEC_BATCH, n_pages).astype(jnp.int32)
    return {
        'x_prompt': nrm(ks[1], (BATCH, SEQ, D_MODEL), 1.0),
        'x_sample': nrm(ks[2], (DEC_BATCH, DEC_SEQ, D_MODEL), 1.0),
        'cache_a_k': nrm(ks[3], (N_LAYERS_A, n_pool, PAGE_SIZE, 2 * H_A, HD_A), 1.0),
        'cache_a_v': nrm(ks[4], (N_LAYERS_A, n_pool, PAGE_SIZE, H_A, 2 * HD_A), 1.0),
        'cache_b_k': nrm(ks[5], (N_LAYERS_B, n_pool, PAGE_SIZE, KV_B, HD_B), 1.0),
        'cache_b_v': nrm(ks[6], (N_LAYERS_B, n_pool, PAGE_SIZE, KV_B, HD_B), 1.0),
        'cache_b_kidx': nrm(ks[7], (N_LAYERS_B, n_pool, PAGE_SIZE, D_I), 1.0),
        'page_table': page_table,
        'p_prompt': nrm(ks[8], (DEPTH, BATCH, SEQ, P_DIM), 1.0),
        'p_sample': nrm(ks[9], (DEPTH, DEC_BATCH, DEC_SEQ, P_DIM), 1.0),
        'a_w_in': nrm(ks[10], (N_LAYERS_A, D_MODEL, A_IN), D_MODEL ** -0.5),
        'a_w_out': nrm(ks[11], (N_LAYERS_A, H_A * 2 * HD_A, D_MODEL), BETA * (H_A * 2 * HD_A) ** -0.5),
        'a_lam_q1': nrm(ks[12], (N_LAYERS_A, HD_A), 0.1),
        'a_lam_k1': nrm(ks[13], (N_LAYERS_A, HD_A), 0.1),
        'a_lam_q2': nrm(ks[14], (N_LAYERS_A, HD_A), 0.1),
        'a_lam_k2': nrm(ks[15], (N_LAYERS_A, HD_A), 0.1),
        'a_subln': 1.0 + nrm(ks[16], (N_LAYERS_A, 2 * HD_A), 0.02),
        'b_w_in': nrm(ks[17], (N_LAYERS_B, D_MODEL, B_IN), D_MODEL ** -0.5),
        'b_w_out': nrm(ks[18], (N_LAYERS_B, H_B * HD_B, D_MODEL), BETA * (H_B * HD_B) ** -0.5),
        'ln_g': 1.0 + nrm(ks[19], (DEPTH, 2, D_MODEL), 0.02),
        'ln_b': nrm(ks[20], (DEPTH, 2, D_MODEL), 0.02),
        'router_w': nrm(ks[21], (D_MODEL, N_EXPERTS), D_MODEL ** -0.5),
        'router_b': nrm(ks[22], (N_EXPERTS,), 0.01),
        'moe_w_gate': nrm(ks[23], (DEPTH, N_EXPERTS, D_MODEL, D_FF), D_MODEL ** -0.5),
        'moe_w_up': nrm(ks[24], (DEPTH, N_EXPERTS, D_MODEL, D_FF), D_MODEL ** -0.5),
        'moe_w_down': nrm(ks[25], (DEPTH, N_EXPERTS, D_FF, D_MODEL), BETA * D_FF ** -0.5),
        'ple_proj': nrm(ks[26], (DEPTH, P_DIM, D_MODEL), P_DIM ** -0.5),
        'ple_gate_w': nrm(ks[27], (DEPTH, D_MODEL, D_MODEL), D_MODEL ** -0.5),
        'ple_gate_b': nrm(ks[28], (DEPTH, D_MODEL), 0.02),
    }


def reference(x_prompt, x_sample, cache_a_k, cache_a_v, cache_b_k, cache_b_v, cache_b_kidx, page_table,
              p_prompt, p_sample, a_w_in, a_w_out, a_lam_q1, a_lam_k1, a_lam_q2, a_lam_k2, a_subln,
              b_w_in, b_w_out, ln_g, ln_b, router_w, router_b, moe_w_gate, moe_w_up, moe_w_down,
              ple_proj, ple_gate_w, ple_gate_b):
    past_len = page_table.shape[1] * cache_a_k.shape[2]
    pos_prompt = jnp.arange(x_prompt.shape[1])
    pos_sample = past_len + jnp.arange(x_sample.shape[1])
    weights = (a_w_in, a_w_out, a_lam_q1, a_lam_k1, a_lam_q2, a_lam_k2, a_subln, b_w_in, b_w_out, ln_g, ln_b,
               router_w, router_b, moe_w_gate, moe_w_up, moe_w_down, ple_proj, ple_gate_w, ple_gate_b)
    y_prompt, na_p, nb_p = _trunk(x_prompt, p_prompt, pos_prompt, None, None, *weights)
    past_a = [(_gather_pages(cache_a_k[j], page_table), _gather_pages(cache_a_v[j], page_table))
              for j in range(N_LAYERS_A)]
    past_b = [(_gather_pages(cache_b_k[j], page_table), _gather_pages(cache_b_v[j], page_table),
               _gather_pages(cache_b_kidx[j], page_table)) for j in range(N_LAYERS_B)]
    y_sample, na_s, nb_s = _trunk(x_sample, p_sample, pos_sample, past_a, past_b, *weights)
    a_k_p = jnp.stack([e[0] for e in na_p])
    a_v_p = jnp.stack([e[1] for e in na_p])
    b_k_p = jnp.stack([e[0] for e in nb_p])
    b_v_p = jnp.stack([e[1] for e in nb_p])
    b_ki_p = jnp.stack([e[2] for e in nb_p])
    a_k_s = jnp.stack([e[0] for e in na_s])
    a_v_s = jnp.stack([e[1] for e in na_s])
    b_k_s = jnp.stack([e[0] for e in nb_s])
    b_v_s = jnp.stack([e[1] for e in nb_s])
    b_ki_s = jnp.stack([e[2] for e in nb_s])
    return (y_prompt, y_sample, a_k_p, a_v_p, b_k_p, b_v_p, b_ki_p, a_k_s, a_v_s, b_k_s, b_v_s, b_ki_s)
```

```python
import functools
import math

import jax
import jax.numpy as jnp
from jax import lax
from jax.experimental import pallas as pl
from jax.experimental.pallas import tpu as pltpu

F32 = jnp.float32
BF16 = jnp.bfloat16
I32 = jnp.int32

D_MODEL = 1024
DEPTH = 2
H_A = 8
HD_A = 64
H_B = 16
KV_B = 4
HD_B = 64
H_I = 8
D_I = 64
TOPK_MAX = 256
ROPE_THETA = 500000.0
N_EXPERTS = 16
N_GROUPS = 4
E_PER_GROUP = 4
D_FF = 512
ALPHA = (2 * DEPTH) ** 0.25
LN_EPS = 1e-5
RMS_EPS = 1e-5
IDX_SCALE = (H_I ** -0.5) * (D_I ** -0.5)

LANES = 128
HEAD = 64
VMEM_LIMIT = 56 * 1024 * 1024
INT_MIN = -2 ** 31
NEG_BIG = -1e30


def _params(sem):
    return pltpu.CompilerParams(dimension_semantics=sem, vmem_limit_bytes=VMEM_LIMIT)


def _dot(a, b):
    return jnp.dot(a, b, preferred_element_type=F32)


def _dot_nt(a, b):
    return lax.dot_general(a, b, (((1,), (1,)), ((), ())), preferred_element_type=F32)


def _div_pow2(x, n):
    assert n & (n - 1) == 0, n
    return x >> (n.bit_length() - 1)


def _layer_norm(y, g, b):
    mu = jnp.mean(y, axis=-1, keepdims=True)
    var = jnp.mean(jnp.square(y - mu), axis=-1, keepdims=True)
    return (y - mu) * lax.rsqrt(var + LN_EPS) * g + b


def _sigmoid(x):
    return 1.0 / (1.0 + jnp.exp(-x))


def _rope_tables_kernel(inv8l, inv16l, inv8c, inv16c, tok8, tok16, tr8, tr16, *, seq, off, tp):
    base = pl.program_id(0) * tp
    row = base + lax.broadcasted_iota(I32, (tp, LANES), 0)
    lane = lax.broadcasted_iota(I32, (tp, LANES), 1)
    pos = (off + (row & (seq - 1))).astype(F32)
    d = lane & (HEAD - 1)
    for half, inv, out in ((8, inv8l, tok8), (16, inv16l, tok16)):
        ang = pos * inv[...]
        c = jnp.cos(ang)
        s = jnp.sin(ang)
        out[0] = jnp.where(d < 2 * half, c, 1.0)
        out[1] = jnp.where(d < half, -s, 0.0)
        out[2] = jnp.where((d >= half) & (d < 2 * half), s, 0.0)
    col = base + lax.broadcasted_iota(I32, (1, tp), 1)
    posr = (off + (col & (seq - 1))).astype(F32)
    for inv, out in ((inv8c, tr8), (inv16c, tr16)):
        ang = posr * inv[...]
        out[0] = jnp.cos(ang)
        out[1] = jnp.sin(ang)


def _rope_tables(n_pos, seq, off):
    tp = min(n_pos, 256)
    lane = jnp.arange(LANES) % HEAD
    invs = []
    for half in (8, 16):
        inv = ROPE_THETA ** (-jnp.arange(half, dtype=F32) / half)
        invs.append((jnp.where(lane < 2 * half, inv[lane % half], 0.0).reshape(1, LANES), inv.reshape(half, 1)))
    full = lambda shape: pl.BlockSpec(shape, lambda i: (0,) * len(shape))
    return pl.pallas_call(
        functools.partial(_rope_tables_kernel, seq=seq, off=off, tp=tp),
        grid=(n_pos // tp,),
        in_specs=[full((1, LANES)), full((1, LANES)), full((8, 1)), full((16, 1))],
        out_specs=[
            pl.BlockSpec((3, tp, LANES), lambda i: (0, i, 0)),
            pl.BlockSpec((3, tp, LANES), lambda i: (0, i, 0)),
            pl.BlockSpec((2, 8, tp), lambda i: (0, 0, i)),
            pl.BlockSpec((2, 16, tp), lambda i: (0, 0, i)),
        ],
        out_shape=[
            jax.ShapeDtypeStruct((3, n_pos, LANES), F32),
            jax.ShapeDtypeStruct((3, n_pos, LANES), F32),
            jax.ShapeDtypeStruct((2, 8, n_pos), F32),
            jax.ShapeDtypeStruct((2, 16, n_pos), F32),
        ],
        compiler_params=_params(("arbitrary",)),
        name="rope_tables",
    )(invs[0][0], invs[1][0], invs[0][1], invs[1][1])


def _rope_tok(y, tab, half):
    c, a, b = tab[0], tab[1], tab[2]
    outs = []
    for k in range(y.shape[1] // LANES):
        yc = y[:, k * LANES:(k + 1) * LANES]
        outs.append(yc * c + pltpu.roll(yc, LANES - half, 1) * a + pltpu.roll(yc, half, 1) * b)
    return outs[0] if len(outs) == 1 else jnp.concatenate(outs, axis=1)


def _rope_tr(yt, tr, half):
    cos, sin = tr[0], tr[1]
    parts = []
    for h in range(yt.shape[0] // HEAD):
        b = h * HEAD
        x1 = yt[b:b + half]
        x2 = yt[b + half:b + 2 * half]
        parts += [x1 * cos - x2 * sin, x2 * cos + x1 * sin, yt[b + 2 * half:b + HEAD]]
    return jnp.concatenate(parts, axis=0)


def _inproj_kernel(*refs, segs):
    x_ref, tok8, tok16, tr8, tr16 = refs[:5]
    w_refs = refs[5:5 + len(segs)]
    out_refs = list(refs[5 + len(segs):])
    xb = x_ref[...].astype(BF16)
    for seg, w_ref in zip(segs, w_refs):
        if seg["tr"]:
            y = _dot_nt(w_ref[...], xb)
            if seg["rot"]:
                y = _rope_tr(y, tr8 if seg["rot"] == 8 else tr16, seg["rot"])
        else:
            y = _dot(xb, w_ref[...])
            if seg["rot"]:
                y = _rope_tok(y, tok8 if seg["rot"] == 8 else tok16, seg["rot"])
        if seg["scale"] != 1.0:
            y = y * seg["scale"]
        for dt, split in seg["outs"]:
            o_ref = out_refs.pop(0)
            if split:
                for g in range(y.shape[1] // HEAD):
                    o_ref[g] = y[:, g * HEAD:(g + 1) * HEAD].astype(dt)
            else:
                o_ref[...] = y.astype(dt)


def _inproj(x, tabs, segs, *, seq, tm):
    t, k = x.shape
    nb = t // seq
    tps = max(seq // tm, 1)
    n_tab = tabs[0].shape[1] // tm
    in_specs = [
        pl.BlockSpec((tm, k), lambda i: (i, 0)),
        pl.BlockSpec((3, tm, LANES), lambda i: (0, i % n_tab, 0)),
        pl.BlockSpec((3, tm, LANES), lambda i: (0, i % n_tab, 0)),
        pl.BlockSpec((2, 8, tm), lambda i: (0, 0, i % n_tab)),
        pl.BlockSpec((2, 16, tm), lambda i: (0, 0, i % n_tab)),
    ]
    out_specs, out_shape = [], []
    for seg in segs:
        w = seg["w"]
        in_specs.append(pl.BlockSpec(w.shape, lambda i: (0, 0)))
        n = w.shape[0] if seg["tr"] else w.shape[1]
        for dt, split in seg["outs"]:
            if seg["tr"]:
                out_specs.append(pl.BlockSpec((None, n, tm), lambda i: (i // tps, 0, i % tps)))
                out_shape.append(jax.ShapeDtypeStruct((nb, n, seq), dt))
            elif split:
                out_specs.append(pl.BlockSpec((n // HEAD, tm, HEAD), lambda i: (0, i, 0)))
                out_shape.append(jax.ShapeDtypeStruct((n // HEAD, t, HEAD), dt))
            else:
                out_specs.append(pl.BlockSpec((tm, n), lambda i: (i, 0)))
                out_shape.append(jax.ShapeDtypeStruct((t, n), dt))
    kern_segs = tuple({k2: v for k2, v in seg.items() if k2 != "w"} for seg in segs)
    return pl.pallas_call(
        functools.partial(_inproj_kernel, segs=kern_segs),
        grid=(t // tm,),
        in_specs=in_specs,
        out_specs=out_specs,
        out_shape=out_shape,
        compiler_params=_params(("arbitrary",)),
        name="inproj",
    )(x, *tabs, *[seg["w"] for seg in segs])


def _lambda(lq1, lk1, lq2, lk2, lam_init):
    return (jnp.exp(jnp.sum(lq1[...] * lk1[...], axis=1, keepdims=True))
            - jnp.exp(jnp.sum(lq2[...] * lk2[...], axis=1, keepdims=True)) + lam_init)


def _sub_norm(o, sub, lam_init):
    return o * lax.rsqrt(jnp.mean(o * o, axis=-1, keepdims=True) + RMS_EPS) * sub * (1.0 - lam_init)


def _diff_attn_kernel(q_ref, kt_ref, v_ref, lq1, lk1, lq2, lk2, sub_ref, o_ref, *, tq, lam_init):
    qi = pl.program_id(2)
    q = q_ref[...]

    def step(j, carry, masked):
        off = pl.multiple_of(j * tq, tq)
        v = v_ref[pl.ds(off, tq), :]
        new = []
        for c in range(2):
            m, l, acc = carry[c]
            kt = kt_ref[c * HEAD:(c + 1) * HEAD, pl.ds(off, tq)]
            s = _dot(q[:, c * HEAD:(c + 1) * HEAD], kt)
            if masked:
                row = lax.broadcasted_iota(I32, (tq, tq), 0)
                col = lax.broadcasted_iota(I32, (tq, tq), 1)
                s = jnp.where(col <= row, s, -jnp.inf)
            m_new = jnp.maximum(m, jnp.max(s, axis=1, keepdims=True))
            alpha = jnp.exp(m - m_new)
            p = jnp.exp(s - m_new)
            l = alpha * l + jnp.sum(p, axis=1, keepdims=True)
            acc = alpha * acc + _dot(p.astype(BF16), v)
            new.append((m_new, l, acc))
        return tuple(new)

    init = tuple((jnp.full((tq, 1), -jnp.inf, F32), jnp.zeros((tq, 1), F32), jnp.zeros((tq, 2 * HEAD), F32))
                 for _ in range(2))
    carry = lax.fori_loop(0, qi, functools.partial(step, masked=False), init)
    (_, l0, a0), (_, l1, a1) = step(qi, carry, True)
    lam = _lambda(lq1, lk1, lq2, lk2, lam_init)
    o = a0 / l0 - lam * (a1 / l1)
    o_ref[...] = _sub_norm(o, sub_ref[...], lam_init).astype(o_ref.dtype)


def _diff_attn_prompt(q, kt, v, lams, sub, *, nb, seq, lam_init):
    tq = min(seq, 256)
    nq = seq // tq
    small = lambda shape: pl.BlockSpec(shape, lambda b, h, i: (0, 0))
    return pl.pallas_call(
        functools.partial(_diff_attn_kernel, tq=tq, lam_init=lam_init),
        grid=(nb, H_A, nq),
        in_specs=[
            pl.BlockSpec((tq, 2 * HEAD), lambda b, h, i: (b * nq + i, h)),
            pl.BlockSpec((None, 2 * HEAD, seq), lambda b, h, i: (b, h, 0)),
            pl.BlockSpec((seq, 2 * HEAD), lambda b, h, i: (b, h)),
            small((1, HEAD)), small((1, HEAD)), small((1, HEAD)), small((1, HEAD)), small((1, 2 * HEAD)),
        ],
        out_specs=pl.BlockSpec((tq, 2 * HEAD), lambda b, h, i: (b * nq + i, h)),
        out_shape=jax.ShapeDtypeStruct(q.shape, BF16),
        compiler_params=_params(("arbitrary", "arbitrary", "arbitrary")),
        name="diff_attn_prompt",
    )(q, kt, v, *lams, sub)


PAGES_PER_STEP = 4


def _block_diag_q(q, n_heads, width):
    s = q.shape[0]
    rows = jnp.concatenate([q] * n_heads, axis=0)
    r = _div_pow2(lax.broadcasted_iota(I32, rows.shape, 0), s)
    c = _div_pow2(lax.broadcasted_iota(I32, rows.shape, 1), HEAD)
    return jnp.where(r == c, rows, 0.0)


def _diff_attn_dec_kernel(pt_ref, q_ref, *refs, s_new, n_steps, lam_init):
    g = PAGES_PER_STEP
    kt_refs, v_refs = refs[:g], refs[g:2 * g]
    ktn_ref, vn_ref, lq1, lk1, lq2, lk2, sub_ref, o_ref, qbd_ref, m_ref, l_ref, acc_ref = refs[2 * g:]
    step = pl.program_id(1)
    nr = 2 * H_A * s_new

    @pl.when(step == 0)
    def _():
        qbd_ref[...] = _block_diag_q(q_ref[...], 2 * H_A, D_MODEL).astype(BF16)
        m_ref[...] = jnp.full(m_ref.shape, -jnp.inf, F32)
        l_ref[...] = jnp.zeros(l_ref.shape, F32)
        acc_ref[...] = jnp.zeros(acc_ref.shape, F32)

    def consume(kt, v_of_head, mask):
        s = _dot(qbd_ref[...], kt.astype(BF16))
        if mask is not None:
            s = jnp.where(mask, s, -jnp.inf)
        m = m_ref[...]
        m_new = jnp.maximum(m, jnp.max(s, axis=1, keepdims=True))
        alpha = jnp.exp(m - m_new)
        p = jnp.exp(s - m_new)
        l_ref[...] = alpha * l_ref[...] + jnp.sum(p, axis=1, keepdims=True)
        m_ref[...] = m_new
        pb = p.astype(BF16)
        for h in range(H_A):
            r = slice(2 * s_new * h, 2 * s_new * (h + 1))
            acc_ref[r, :] = alpha[r] * acc_ref[r, :] + _dot(pb[r], v_of_head(h).astype(BF16))

    @pl.when(step < n_steps - 1)
    def _():
        for kt_ref, v_ref in zip(kt_refs, v_refs):
            consume(kt_ref[...], lambda h, v_ref=v_ref: v_ref[pl.ds(h, LANES, stride=H_A), :], None)

    @pl.when(step == n_steps - 1)
    def _():
        key = lax.broadcasted_iota(I32, (nr, LANES), 1)
        tok = lax.broadcasted_iota(I32, (nr, LANES), 0) & (s_new - 1)
        consume(ktn_ref[...], lambda h: vn_ref[pl.ds(h, LANES, stride=H_A), :], key <= tok)
        lam = _lambda(lq1, lk1, lq2, lk2, lam_init)
        o = acc_ref[...] / l_ref[...]
        outs = []
        for h in range(H_A):
            b = 2 * s_new * h
            outs.append(_sub_norm(o[b:b + s_new] - lam * o[b + s_new:b + 2 * s_new], sub_ref[...], lam_init))
        o_ref[...] = jnp.concatenate(outs, axis=1)


def _diff_attn_decode(q, kt_pool, v_pool, page_table, kt_new, v_new, lams, sub, *, s_new, lam_init):
    nb, n_pages = page_table.shape
    g = PAGES_PER_STEP
    n_steps = n_pages // g + 1
    nr = 2 * H_A * s_new

    def page_map(k):
        return lambda b, i, pt: (pt[b, jnp.minimum(i, n_steps - 2) * g + k], 0, 0)

    page = lambda k: pl.BlockSpec((None, D_MODEL, LANES), page_map(k))
    new = pl.BlockSpec((None, D_MODEL, LANES), lambda b, i, pt: (b, 0, 0))
    small = lambda shape: pl.BlockSpec(shape, lambda b, i, pt: (0, 0))
    grid_spec = pltpu.PrefetchScalarGridSpec(
        num_scalar_prefetch=1,
        grid=(nb, n_steps),
        in_specs=[pl.BlockSpec((s_new, D_MODEL), lambda b, i, pt: (b, 0))]
        + [page(k) for k in range(g)] + [page(k) for k in range(g)] + [new, new]
        + [small((1, HEAD))] * 4 + [small((1, 2 * HEAD))],
        out_specs=pl.BlockSpec((s_new, D_MODEL), lambda b, i, pt: (b, 0)),
        scratch_shapes=[
            pltpu.VMEM((nr, D_MODEL), BF16),
            pltpu.VMEM((nr, 1), F32),
            pltpu.VMEM((nr, 1), F32),
            pltpu.VMEM((nr, 2 * HEAD), F32),
        ],
    )
    return pl.pallas_call(
        functools.partial(_diff_attn_dec_kernel, s_new=s_new, n_steps=n_steps, lam_init=lam_init),
        grid_spec=grid_spec,
        out_shape=jax.ShapeDtypeStruct(q.shape, F32),
        compiler_params=_params(("arbitrary", "arbitrary")),
        name="diff_attn_decode",
    )(page_table, q, *([kt_pool] * g), *([v_pool] * g), kt_new, v_new, *lams, sub)


def _sort_key(score):
    bits = lax.bitcast_convert_type(score, I32)
    return bits ^ ((bits >> 31) & 0x7FFFFFFF)


def _select_mask(key, valid, col, n_valid, n_sel, idx_bits):
    rows = key.shape[0]

    def count(pred):
        return jnp.sum(pred.astype(I32), axis=1, keepdims=True)

    def body(i, t_u):
        cand = t_u | lax.shift_left(jnp.int32(1), 31 - i)
        return jnp.where(count(key >= (cand ^ INT_MIN)) >= n_sel, cand, t_u)

    t = lax.fori_loop(0, 32, body, jnp.zeros((rows, 1), I32)) ^ INT_MIN
    n_ge = count(key >= t)
    need = n_sel - count(key > t)
    tie_rows = (n_valid > n_sel) & (n_ge > n_sel)
    return (key >= t) & valid, tie_rows, t, need


def _tie_break(key, valid, col, t, need, idx_bits):
    eq = key == t

    def body(i, c):
        cand = c | lax.shift_left(jnp.int32(1), idx_bits - 1 - i)
        before = jnp.sum((eq & (col < cand)).astype(I32), axis=1, keepdims=True)
        return jnp.where(before <= need - 1, cand, c)

    c = lax.fori_loop(0, idx_bits, body, jnp.zeros_like(t))
    return ((key > t) | (eq & (col <= c))) & valid


def _write_bias(bias_ref, key, valid, col, n_valid, n_sel, idx_bits):
    sel, tie_rows, t, need = _select_mask(key, valid, col, n_valid, n_sel, idx_bits)
    bias_ref[...] = jnp.where(sel, 0.0, NEG_BIG).astype(bias_ref.dtype)

    @pl.when(jnp.max(tie_rows.astype(I32)) > 0)
    def _():
        tb = _tie_break(key, valid, col, t, need, idx_bits)
        fixed = (tie_rows & tb) | (jnp.logical_not(tie_rows) & sel)
        bias_ref[...] = jnp.where(fixed, 0.0, NEG_BIG).astype(bias_ref.dtype)


def _dsa_select_kernel(qi_ref, w_ref, kit_ref, bias_ref, *, tq, n_sel):
    seq = kit_ref.shape[1]
    qi = qi_ref[...]
    w = w_ref[...]
    kit = kit_ref[...]
    score = jnp.zeros((tq, seq), F32)
    for h in range(H_I):
        d = _dot(qi[:, h * HEAD:(h + 1) * HEAD], kit)
        score = score + w[:, h:h + 1] * jnp.maximum(d, 0.0)
    row = pl.program_id(1) * tq + lax.broadcasted_iota(I32, (tq, seq), 0)
    col = lax.broadcasted_iota(I32, (tq, seq), 1)
    valid = col <= row
    key = jnp.where(valid, _sort_key(score), INT_MIN)
    _write_bias(bias_ref, key, valid, col, row[:, :1] + 1, n_sel, max(seq.bit_length(), 1))


def _dsa_select_prompt(qi, w, kit, *, nb, seq, n_sel):
    tq = min(seq, 128)
    nq = seq // tq
    return pl.pallas_call(
        functools.partial(_dsa_select_kernel, tq=tq, n_sel=n_sel),
        grid=(nb, nq),
        in_specs=[
            pl.BlockSpec((tq, H_I * HEAD), lambda b, i: (b * nq + i, 0)),
            pl.BlockSpec((tq, LANES), lambda b, i: (b * nq + i, 0)),
            pl.BlockSpec((None, HEAD, seq), lambda b, i: (b, 0, 0)),
        ],
        out_specs=pl.BlockSpec((None, tq, seq), lambda b, i: (b, i, 0)),
        out_shape=jax.ShapeDtypeStruct((nb, seq, seq), BF16),
        compiler_params=_params(("arbitrary", "arbitrary")),
        name="dsa_select_prompt",
    )(qi, w, kit)


def _dsa_select_dec_kernel(pt_ref, qi_ref, w_ref, *refs, s_new, n_steps, n_sel, n_past):
    g = PAGES_PER_STEP
    kit_refs = refs[:g]
    kitn_ref, bias_ref, qs_ref, ws_ref, score_ref = refs[g:]
    step = pl.program_id(1)

    @pl.when(step == 0)
    def _():
        qi = qi_ref[...]
        w = w_ref[...]
        qs_ref[...] = jnp.concatenate([qi[:, h * HEAD:(h + 1) * HEAD] for h in range(H_I)], axis=0).astype(BF16)
        ws_ref[...] = jnp.concatenate([w[:, h:h + 1] for h in range(H_I)], axis=0)

    def page_score(kit):
        d = jnp.maximum(_dot(qs_ref[...], kit.astype(BF16)), 0.0) * ws_ref[...]
        sc = d[0:s_new]
        for h in range(1, H_I):
            sc = sc + d[h * s_new:(h + 1) * s_new]
        return sc

    @pl.when(step < n_steps - 1)
    def _():
        for k, kit_ref in enumerate(kit_refs):
            off = pl.multiple_of((step * g + k) * LANES, LANES)
            score_ref[:, pl.ds(off, LANES)] = page_score(kit_ref[...])

    @pl.when(step == n_steps - 1)
    def _():
        score_ref[:, n_past:n_past + LANES] = page_score(kitn_ref[...])
        width = n_past + LANES
        col = lax.broadcasted_iota(I32, (s_new, width), 1)
        tok = lax.broadcasted_iota(I32, (s_new, width), 0)
        valid = col <= n_past + tok
        key = jnp.where(valid, _sort_key(score_ref[...]), INT_MIN)
        _write_bias(bias_ref, key, valid, col, n_past + tok[:, :1] + 1, n_sel, width.bit_length())


def _dsa_select_decode(qi, w, kit_pool, page_table, kit_new, *, s_new, n_sel):
    nb, n_pages = page_table.shape
    g = PAGES_PER_STEP
    n_steps = n_pages // g + 1
    n_past = n_pages * LANES
    width = n_past + LANES

    def page_map(k):
        return lambda b, i, pt: (pt[b, jnp.minimum(i, n_steps - 2) * g + k], 0, 0)

    grid_spec = pltpu.PrefetchScalarGridSpec(
        num_scalar_prefetch=1,
        grid=(nb, n_steps),
        in_specs=[pl.BlockSpec((s_new, H_I * HEAD), lambda b, i, pt: (b, 0)),
                  pl.BlockSpec((s_new, LANES), lambda b, i, pt: (b, 0))]
        + [pl.BlockSpec((None, HEAD, LANES), page_map(k)) for k in range(g)]
        + [pl.BlockSpec((None, HEAD, LANES), lambda b, i, pt: (b, 0, 0))],
        out_specs=pl.BlockSpec((None, s_new, width), lambda b, i, pt: (b, 0, 0)),
        scratch_shapes=[
            pltpu.VMEM((H_I * s_new, HEAD), BF16),
            pltpu.VMEM((H_I * s_new, 1), F32),
            pltpu.VMEM((s_new, width), F32),
        ],
    )
    return pl.pallas_call(
        functools.partial(_dsa_select_dec_kernel, s_new=s_new, n_steps=n_steps, n_sel=n_sel, n_past=n_past),
        grid_spec=grid_spec,
        out_shape=jax.ShapeDtypeStruct((nb, s_new, width), F32),
        compiler_params=_params(("arbitrary", "arbitrary")),
        name="dsa_select_decode",
    )(page_table, qi, w, *([kit_pool] * g), kit_new)


def _dsa_attn_kernel(q_ref, kt_ref, v_ref, bias_ref, o_ref, *, tq, tk):
    grp = H_B // KV_B
    qi = pl.program_id(2)
    qb = q_ref[...]
    q4 = jnp.concatenate([qb[:, h * HEAD:(h + 1) * HEAD] for h in range(grp)], axis=0)
    n_kv = ((qi + 1) * tq + tk - 1) // tk

    def step(j, carry):
        m, l, acc = carry
        off = pl.multiple_of(j * tk, tk)
        s = _dot(q4, kt_ref[:, pl.ds(off, tk)])
        s = (s.reshape(grp, tq, tk) + bias_ref[:, pl.ds(off, tk)].astype(F32)[None]).reshape(grp * tq, tk)
        m_new = jnp.maximum(m, jnp.max(s, axis=1, keepdims=True))
        alpha = jnp.exp(m - m_new)
        p = jnp.exp(s - m_new)
        l = alpha * l + jnp.sum(p, axis=1, keepdims=True)
        acc = alpha * acc + _dot(p.astype(BF16), v_ref[pl.ds(off, tk), :])
        return m_new, l, acc

    init = (jnp.full((grp * tq, 1), -jnp.inf, F32), jnp.zeros((grp * tq, 1), F32), jnp.zeros((grp * tq, HEAD), F32))
    _, l, acc = lax.fori_loop(0, n_kv, step, init)
    o = acc / l
    o_ref[...] = jnp.concatenate([o[h * tq:(h + 1) * tq] for h in range(grp)], axis=1).astype(o_ref.dtype)


def _dsa_attn_prompt(q, kt, v, bias, *, nb, seq):
    tq = min(seq, 128)
    tk = min(seq, 256)
    nq = seq // tq
    grp = H_B // KV_B
    return pl.pallas_call(
        functools.partial(_dsa_attn_kernel, tq=tq, tk=tk),
        grid=(nb, KV_B, nq),
        in_specs=[
            pl.BlockSpec((tq, grp * HEAD), lambda b, g, i: (b * nq + i, g)),
            pl.BlockSpec((None, HEAD, seq), lambda b, g, i: (b, g, 0)),
            pl.BlockSpec((None, seq, HEAD), lambda b, g, i: (g, b, 0)),
            pl.BlockSpec((None, tq, seq), lambda b, g, i: (b, i, 0)),
        ],
        out_specs=pl.BlockSpec((tq, grp * HEAD), lambda b, g, i: (b * nq + i, g)),
        out_shape=jax.ShapeDtypeStruct(q.shape, BF16),
        compiler_params=_params(("arbitrary", "arbitrary", "arbitrary")),
        name="dsa_attn_prompt",
    )(q, kt, v, bias)


def _dsa_attn_dec_kernel(pt_ref, q_ref, bias_ref, *refs, s_new, n_steps, n_past):
    g = PAGES_PER_STEP
    kt_refs, vt_refs = refs[:g], refs[g:2 * g]
    ktn_ref, vtn_ref, o_ref, qbd_ref, m_ref, l_ref, acc_ref = refs[2 * g:]
    step = pl.program_id(1)
    nr = H_B * s_new
    grp = H_B // KV_B

    @pl.when(step == 0)
    def _():
        q = q_ref[...]
        rows = jnp.concatenate([q[:, h * HEAD:(h + 1) * HEAD] for h in range(H_B)], axis=0)
        wide = jnp.concatenate([rows] * KV_B, axis=1)
        r = _div_pow2(lax.broadcasted_iota(I32, wide.shape, 0), grp * s_new)
        c = _div_pow2(lax.broadcasted_iota(I32, wide.shape, 1), HEAD)
        qbd_ref[...] = jnp.where(r == c, wide, 0.0).astype(BF16)
        m_ref[...] = jnp.full(m_ref.shape, -jnp.inf, F32)
        l_ref[...] = jnp.zeros(l_ref.shape, F32)
        acc_ref[...] = jnp.zeros(acc_ref.shape, F32)

    def consume(kt, vt, bias):
        s = _dot(qbd_ref[...], kt.astype(BF16))
        s = (s.reshape(H_B, s_new, LANES) + bias[None]).reshape(nr, LANES)
        m = m_ref[...]
        m_new = jnp.maximum(m, jnp.max(s, axis=1, keepdims=True))
        alpha = jnp.exp(m - m_new)
        p = jnp.exp(s - m_new)
        l_ref[...] = alpha * l_ref[...] + jnp.sum(p, axis=1, keepdims=True)
        m_ref[...] = m_new
        acc_ref[...] = alpha * acc_ref[...] + _dot_nt(p.astype(BF16), vt.astype(BF16))

    @pl.when(step < n_steps - 1)
    def _():
        for k in range(g):
            off = pl.multiple_of((step * g + k) * LANES, LANES)
            consume(kt_refs[k][...], vt_refs[k][...], bias_ref[:, pl.ds(off, LANES)])

    @pl.when(step == n_steps - 1)
    def _():
        consume(ktn_ref[...], vtn_ref[...], bias_ref[:, n_past:n_past + LANES])
        o = acc_ref[...] / l_ref[...]
        outs = []
        for h in range(H_B):
            kv = h // grp
            outs.append(o[h * s_new:(h + 1) * s_new, kv * HEAD:(kv + 1) * HEAD])
        o_ref[...] = jnp.concatenate(outs, axis=1)


def _dsa_attn_decode(q, bias, kt_pool, vt_pool, page_table, kt_new, vt_new, *, s_new):
    nb, n_pages = page_table.shape
    g = PAGES_PER_STEP
    n_steps = n_pages // g + 1
    n_past = n_pages * LANES
    nr = H_B * s_new
    kvw = KV_B * HEAD

    def page_map(k):
        return lambda b, i, pt: (pt[b, jnp.minimum(i, n_steps - 2) * g + k], 0, 0)

    page = lambda k: pl.BlockSpec((None, kvw, LANES), page_map(k))
    new = pl.BlockSpec((None, kvw, LANES), lambda b, i, pt: (b, 0, 0))
    grid_spec = pltpu.PrefetchScalarGridSpec(
        num_scalar_prefetch=1,
        grid=(nb, n_steps),
        in_specs=[pl.BlockSpec((s_new, D_MODEL), lambda b, i, pt: (b, 0)),
                  pl.BlockSpec((None, s_new, n_past + LANES), lambda b, i, pt: (b, 0, 0))]
        + [page(k) for k in range(g)] + [page(k) for k in range(g)] + [new, new],
        out_specs=pl.BlockSpec((s_new, D_MODEL), lambda b, i, pt: (b, 0)),
        scratch_shapes=[
            pltpu.VMEM((nr, kvw), BF16),
            pltpu.VMEM((nr, 1), F32),
            pltpu.VMEM((nr, 1), F32),
            pltpu.VMEM((nr, kvw), F32),
        ],
    )
    return pl.pallas_call(
        functools.partial(_dsa_attn_dec_kernel, s_new=s_new, n_steps=n_steps, n_past=n_past),
        grid_spec=grid_spec,
        out_shape=jax.ShapeDtypeStruct(q.shape, F32),
        compiler_params=_params(("arbitrary", "arbitrary")),
        name="dsa_attn_decode",
    )(page_table, q, bias, *([kt_pool] * g), *([vt_pool] * g), kt_new, vt_new)


def _outproj_ln_kernel(o_ref, w_ref, x_ref, g_ref, b_ref, out_ref):
    h = _dot(o_ref[...].astype(BF16), w_ref[...])
    out_ref[...] = _layer_norm(ALPHA * x_ref[...] + h, g_ref[...], b_ref[...])


def _outproj_ln(o, w, x, g, b, *, tm):
    t, d = x.shape
    row = lambda n: pl.BlockSpec((tm, n), lambda i: (i, 0))
    full = lambda shape: pl.BlockSpec(shape, lambda i: (0, 0))
    return pl.pallas_call(
        _outproj_ln_kernel,
        grid=(t // tm,),
        in_specs=[row(o.shape[1]), full(w.shape), row(d), full((1, d)), full((1, d))],
        out_specs=row(d),
        out_shape=jax.ShapeDtypeStruct((t, d), F32),
        compiler_params=_params(("arbitrary",)),
        name="outproj_ln",
    )(o, w, x, g, b)


def _route(sel, aff):
    gs = []
    for g in range(N_GROUPS):
        a, b, c, d = sel[4 * g:4 * g + 4]
        hi1, lo1, hi2, lo2 = jnp.maximum(a, b), jnp.minimum(a, b), jnp.maximum(c, d), jnp.minimum(c, d)
        gs.append(jnp.maximum(hi1, hi2) + jnp.maximum(jnp.minimum(hi1, hi2), jnp.maximum(lo1, lo2)))
    best, gi = gs[0], jnp.zeros(gs[0].shape, I32)
    for g in range(1, N_GROUPS):
        better = gs[g] > best
        best = jnp.where(better, gs[g], best)
        gi = jnp.where(better, g, gi)

    def pick(rows):
        out = []
        for j in range(E_PER_GROUP):
            v = rows[j]
            for g in range(1, N_GROUPS):
                v = jnp.where(gi == g, rows[4 * g + j], v)
            out.append(v)
        return out

    sv, av = pick(sel), pick(aff)

    def argmax_first(vals):
        bv, bi = vals[0], jnp.zeros(vals[0].shape, I32)
        for j in range(1, E_PER_GROUP):
            better = vals[j] > bv
            bv = jnp.where(better, vals[j], bv)
            bi = jnp.where(better, j, bi)
        return bi

    i1 = argmax_first(sv)
    i2 = argmax_first([jnp.where(i1 == j, -jnp.inf, sv[j]) for j in range(E_PER_GROUP)])
    g1, g2 = av[0], av[0]
    for j in range(1, E_PER_GROUP):
        g1 = jnp.where(i1 == j, av[j], g1)
        g2 = jnp.where(i2 == j, av[j], g2)
    tot = g1 + g2
    g1, g2 = g1 / tot, g2 / tot
    comb = []
    for e in range(N_EXPERTS):
        g, j = divmod(e, E_PER_GROUP)
        in_g = gi == g
        comb.append(jnp.where(in_g & (i1 == j), g1, jnp.where(in_g & (i2 == j), g2, 0.0)))
    return comb


def _router_rows(x, rwt_hi, rwt_lo, rb):
    xh = x.astype(BF16)
    xl = (x - xh.astype(F32)).astype(BF16)
    logits = _dot_nt(rwt_hi, xh) + (_dot_nt(rwt_hi, xl) + _dot_nt(rwt_lo, xh))
    aff = _sigmoid(logits)
    sel = aff + rb
    return [sel[e:e + 1] for e in range(N_EXPERTS)], [aff[e:e + 1] for e in range(N_EXPERTS)]


def _moe_dense_kernel(x_ref, rwh_ref, rwl_ref, rb_ref, wg_ref, wu_ref, wd_ref, g_ref, b_ref, out_ref,
                      xb_ref, comb_ref, acc_ref, *, tm):
    e = pl.program_id(1)

    @pl.when(e == 0)
    def _():
        x = x_ref[...]
        xb_ref[...] = x.astype(BF16)
        sel, aff = _router_rows(x, rwh_ref[...], rwl_ref[...], rb_ref[...])
        comb = jnp.concatenate(_route(sel, aff) + [jnp.zeros((LANES - N_EXPERTS, tm), F32)], axis=0)
        comb_ref[...] = comb.T
        acc_ref[...] = jnp.zeros(acc_ref.shape, F32)

    xb = xb_ref[...]
    hg = _dot(xb, wg_ref[...])
    hu = _dot(xb, wu_ref[...])
    hdn = (hg * _sigmoid(hg)) * hu
    y = _dot(hdn.astype(BF16), wd_ref[...])
    lane = lax.broadcasted_iota(I32, (tm, LANES), 1)
    col = jnp.sum(jnp.where(lane == e, comb_ref[...], 0.0), axis=1, keepdims=True)
    acc_ref[...] += col * y

    @pl.when(e == N_EXPERTS - 1)
    def _():
        out_ref[...] = _layer_norm(ALPHA * x_ref[...] + acc_ref[...], g_ref[...], b_ref[...])


def _moe_ln(x, rwt_hi, rwt_lo, rb, wg, wu, wd, g, b, *, tm):
    t, d = x.shape
    f = wg.shape[2]
    full = lambda shape: pl.BlockSpec(shape, lambda i, e: (0,) * len(shape))
    return pl.pallas_call(
        functools.partial(_moe_dense_kernel, tm=tm),
        grid=(t // tm, N_EXPERTS),
        in_specs=[
            pl.BlockSpec((tm, d), lambda i, e: (i, 0)),
            full((N_EXPERTS, d)), full((N_EXPERTS, d)), full((N_EXPERTS, 1)),
            pl.BlockSpec((None, d, f), lambda i, e: (e, 0, 0)),
            pl.BlockSpec((None, d, f), lambda i, e: (e, 0, 0)),
            pl.BlockSpec((None, f, d), lambda i, e: (e, 0, 0)),
            full((1, d)), full((1, d)),
        ],
        out_specs=pl.BlockSpec((tm, d), lambda i, e: (i, 0)),
        out_shape=jax.ShapeDtypeStruct((t, d), F32),
        scratch_shapes=[pltpu.VMEM((tm, d), BF16), pltpu.VMEM((tm, LANES), F32), pltpu.VMEM((tm, d), F32)],
        compiler_params=_params(("arbitrary", "arbitrary")),
        name="moe_ln",
    )(x, rwt_hi, rwt_lo, rb, wg, wu, wd, g, b)


def _ple_kernel(x_ref, p_ref, wg_ref, bg_ref, wp_ref, out_ref):
    x = x_ref[...]
    gate = _sigmoid(_dot(x.astype(BF16), wg_ref[...]) + bg_ref[...])
    out_ref[...] = x + gate * _dot(p_ref[...].astype(BF16), wp_ref[...])


def _ple(x, p, wg, bg, wp, *, tm):
    t, d = x.shape
    pd = p.shape[1]
    full = lambda shape: pl.BlockSpec(shape, lambda i: (0, 0))
    return pl.pallas_call(
        _ple_kernel,
        grid=(t // tm,),
        in_specs=[pl.BlockSpec((tm, d), lambda i: (i, 0)), pl.BlockSpec((tm, pd), lambda i: (i, 0)),
                  full((d, d)), full((1, d)), full((pd, d))],
        out_specs=pl.BlockSpec((tm, d), lambda i: (i, 0)),
        out_shape=jax.ShapeDtypeStruct((t, d), F32),
        compiler_params=_params(("arbitrary",)),
        name="ple",
    )(x, p, wg, bg, wp)


def _pad_page_t(x, nb, s_new):
    xt = x.reshape(nb, s_new, -1).transpose(0, 2, 1)
    return jnp.pad(xt, ((0, 0), (0, 0), (0, LANES - s_new)))


def _trunk(x3, p4, pos_off, past, page_table, w):
    nb, seq, d = x3.shape
    t = nb * seq
    prompt = past is None
    x = x3.reshape(t, d)
    p = p4.reshape(DEPTH, t, -1)
    tm = min(256, t)
    if prompt:
        tabs = _rope_tables(seq, seq, pos_off)
    else:
        tabs = _rope_tables(t, seq, pos_off)
    tm_moe = 512 if t % 512 == 0 else tm
    outs = {}

    lam_init = 0.8 - 0.6 * math.exp(-0.3 * 0)
    w_in = w["a_w_in"]
    nq = 2 * H_A * HD_A
    wq, wk, wv = w_in[:, :nq], w_in[:, nq:2 * nq], w_in[:, 2 * nq:]
    lams = [w[n] for n in ("a_lam_q1", "a_lam_k1", "a_lam_q2", "a_lam_k2")]
    if prompt:
        segs = [
            dict(w=wq, tr=False, rot=8, scale=HD_A ** -0.5, outs=[(BF16, False)]),
            dict(w=wk.T, tr=True, rot=8, scale=1.0, outs=[(F32, False), (BF16, False)]),
            dict(w=wv, tr=False, rot=0, scale=1.0, outs=[(F32, False), (BF16, False)]),
        ]
        q, kt, kt_b, v, v_b = _inproj(x, tabs, segs, seq=seq, tm=tm)
        o = _diff_attn_prompt(q, kt_b, v_b, lams, w["a_subln"], nb=nb, seq=seq, lam_init=lam_init)
        outs["a_k"] = kt.reshape(nb, 2 * H_A, HD_A, seq).transpose(0, 3, 1, 2)
    else:
        segs = [
            dict(w=wq, tr=False, rot=8, scale=HD_A ** -0.5, outs=[(F32, False)]),
            dict(w=wk, tr=False, rot=8, scale=1.0, outs=[(F32, False)]),
            dict(w=wv, tr=False, rot=0, scale=1.0, outs=[(F32, False)]),
        ]
        q, k, v = _inproj(x, tabs, segs, seq=seq, tm=tm)
        v_new = jnp.pad(v.reshape(nb, seq, H_A, 2 * HD_A), ((0, 0), (0, LANES - seq), (0, 0), (0, 0)))
        o = _diff_attn_decode(q, past["a_kt"], past["a_v"], page_table, _pad_page_t(k, nb, seq),
                              v_new.reshape(nb, LANES * H_A, 2 * HD_A), lams, w["a_subln"],
                              s_new=seq, lam_init=lam_init)
        outs["a_k"] = k.reshape(nb, seq, 2 * H_A, HD_A)
    outs["a_v"] = v.reshape(nb, seq, H_A, 2 * HD_A)
    x = _outproj_ln(o, w["a_w_out"], x, w["ln_g"][0, 0], w["ln_b"][0, 0], tm=tm)
    x = _moe_ln(x, w["rwt_hi"], w["rwt_lo"], w["rb"], w["moe_w_gate"][0], w["moe_w_up"][0], w["moe_w_down"][0],
                w["ln_g"][0, 1], w["ln_b"][0, 1], tm=tm_moe)
    x = _ple(x, p[0], w["ple_gate_w"][0], w["ple_gate_b"][0], w["ple_proj"][0], tm=tm)

    w_in = w["b_w_in"]
    sizes = (H_B * HD_B, KV_B * HD_B, KV_B * HD_B, H_I * D_I, D_I, H_I)
    offs = [sum(sizes[:m]) for m in range(len(sizes) + 1)]
    wq, wk, wv, wqi, wki, wwi = [w_in[:, offs[m]:offs[m + 1]] for m in range(len(sizes))]
    wwi = jnp.pad(wwi, ((0, 0), (0, LANES - H_I)))
    if prompt:
        n_sel = min(TOPK_MAX, seq // 4)
        segs = [
            dict(w=wq, tr=False, rot=8, scale=HD_B ** -0.5, outs=[(BF16, False)]),
            dict(w=wk.T, tr=True, rot=8, scale=1.0, outs=[(F32, False), (BF16, False)]),
            dict(w=wv.T, tr=True, rot=0, scale=1.0, outs=[(F32, False)]),
            dict(w=wv, tr=False, rot=0, scale=1.0, outs=[(BF16, True)]),
            dict(w=wqi, tr=False, rot=16, scale=1.0, outs=[(BF16, False)]),
            dict(w=wki.T, tr=True, rot=16, scale=1.0, outs=[(F32, False), (BF16, False)]),
            dict(w=wwi, tr=False, rot=0, scale=IDX_SCALE, outs=[(F32, False)]),
        ]
        q, kt, kt_b, vt, v_b, qi, kit, kit_b, wi = _inproj(x, tabs, segs, seq=seq, tm=tm)
        bias = _dsa_select_prompt(qi, wi, kit_b, nb=nb, seq=seq, n_sel=n_sel)
        o = _dsa_attn_prompt(q, kt_b, v_b, bias, nb=nb, seq=seq)
        tr4 = lambda a, h: a.reshape(nb, h, HEAD, seq).transpose(0, 3, 1, 2)
        outs["b_k"] = tr4(kt, KV_B)
        outs["b_v"] = tr4(vt, KV_B)
        outs["b_ki"] = kit.transpose(0, 2, 1)
    else:
        n_past = page_table.shape[1] * LANES
        n_sel = min(TOPK_MAX, (n_past + seq) // 4)
        segs = [
            dict(w=wq, tr=False, rot=8, scale=HD_B ** -0.5, outs=[(F32, False)]),
            dict(w=wk, tr=False, rot=8, scale=1.0, outs=[(F32, False)]),
            dict(w=wv, tr=False, rot=0, scale=1.0, outs=[(F32, False)]),
            dict(w=wqi, tr=False, rot=16, scale=1.0, outs=[(F32, False)]),
            dict(w=jnp.pad(wki, ((0, 0), (0, LANES - D_I))), tr=False, rot=16, scale=1.0, outs=[(F32, False)]),
            dict(w=wwi, tr=False, rot=0, scale=IDX_SCALE, outs=[(F32, False)]),
        ]
        q, k, v, qi, ki, wi = _inproj(x, tabs, segs, seq=seq, tm=tm)
        ki = ki[:, :D_I]
        bias = _dsa_select_decode(qi, wi, past["b_kit"], page_table, _pad_page_t(ki, nb, seq), s_new=seq, n_sel=n_sel)
        o = _dsa_attn_decode(q, bias, past["b_kt"], past["b_vt"], page_table, _pad_page_t(k, nb, seq),
                             _pad_page_t(v, nb, seq), s_new=seq)
        outs["b_k"] = k.reshape(nb, seq, KV_B, HD_B)
        outs["b_v"] = v.reshape(nb, seq, KV_B, HD_B)
        outs["b_ki"] = ki.reshape(nb, seq, D_I)
    x = _outproj_ln(o, w["b_w_out"], x, w["ln_g"][1, 0], w["ln_b"][1, 0], tm=tm)
    x = _moe_ln(x, w["rwt_hi"], w["rwt_lo"], w["rb"], w["moe_w_gate"][1], w["moe_w_up"][1], w["moe_w_down"][1],
                w["ln_g"][1, 1], w["ln_b"][1, 1], tm=tm_moe)
    x = _ple(x, p[1], w["ple_gate_w"][1], w["ple_gate_b"][1], w["ple_proj"][1], tm=tm)
    return x.reshape(nb, seq, d), outs


def _prep_weights(a_w_in, a_w_out, a_lam_q1, a_lam_k1, a_lam_q2, a_lam_k2, a_subln, b_w_in, b_w_out, ln_g, ln_b,
                  router_w, router_b, moe_w_gate, moe_w_up, moe_w_down, ple_proj, ple_gate_w, ple_gate_b):
    rwt = router_w.T
    rwt_hi = rwt.astype(BF16)
    return {
        "a_w_in": a_w_in[0].astype(BF16), "a_w_out": a_w_out[0].astype(BF16),
        "a_lam_q1": a_lam_q1, "a_lam_k1": a_lam_k1, "a_lam_q2": a_lam_q2, "a_lam_k2": a_lam_k2,
        "a_subln": a_subln,
        "b_w_in": b_w_in[0].astype(BF16), "b_w_out": b_w_out[0].astype(BF16),
        "ln_g": ln_g[:, :, None, :], "ln_b": ln_b[:, :, None, :],
        "rwt_hi": rwt_hi, "rwt_lo": (rwt - rwt_hi.astype(F32)).astype(BF16), "rb": router_b.reshape(N_EXPERTS, 1),
        "moe_w_gate": moe_w_gate.astype(BF16), "moe_w_up": moe_w_up.astype(BF16),
        "moe_w_down": moe_w_down.astype(BF16),
        "ple_proj": ple_proj.astype(BF16), "ple_gate_w": ple_gate_w.astype(BF16),
        "ple_gate_b": ple_gate_b[:, None, :],
    }


def kernel(x_prompt, x_sample, cache_a_k, cache_a_v, cache_b_k, cache_b_v, cache_b_kidx, page_table, p_prompt,
           p_sample, a_w_in, a_w_out, a_lam_q1, a_lam_k1, a_lam_q2, a_lam_k2, a_subln, b_w_in, b_w_out, ln_g, ln_b,
           router_w, router_b, moe_w_gate, moe_w_up, moe_w_down, ple_proj, ple_gate_w, ple_gate_b):
    w = _prep_weights(a_w_in, a_w_out, a_lam_q1, a_lam_k1, a_lam_q2, a_lam_k2, a_subln, b_w_in, b_w_out, ln_g, ln_b,
                      router_w, router_b, moe_w_gate, moe_w_up, moe_w_down, ple_proj, ple_gate_w, ple_gate_b)
    n_pool, page = cache_a_k.shape[1], cache_a_k.shape[2]
    past_len = page_table.shape[1] * page
    past = {
        "a_kt": cache_a_k[0].transpose(0, 2, 3, 1).reshape(n_pool, 2 * H_A * HD_A, page),
        "a_v": cache_a_v[0].reshape(n_pool, page * H_A, 2 * HD_A),
        "b_kt": cache_b_k[0].transpose(0, 2, 3, 1).reshape(n_pool, KV_B * HD_B, page),
        "b_vt": cache_b_v[0].transpose(0, 2, 3, 1).reshape(n_pool, KV_B * HD_B, page),
        "b_kit": cache_b_kidx[0].transpose(0, 2, 1),
    }
    y_p, op = _trunk(x_prompt, p_prompt, 0, None, None, w)
    y_s, os_ = _trunk(x_sample, p_sample, past_len, past, page_table, w)
    lead = lambda a: a[None]
    return (y_p, y_s,
            lead(op["a_k"]), lead(op["a_v"]), lead(op["b_k"]), lead(op["b_v"]), lead(op["b_ki"]),
            lead(os_["a_k"]), lead(os_["a_v"]), lead(os_["b_k"]), lead(os_["b_v"]), lead(os_["b_ki"]))
```

```python
import functools
import math

import jax
import jax.numpy as jnp
from jax import lax
from jax.experimental import pallas as pl
from jax.experimental.pallas import tpu as pltpu

F32 = jnp.float32
BF16 = jnp.bfloat16
I32 = jnp.int32

D_MODEL = 1024
DEPTH = 2
H_A = 8
HD_A = 64
H_B = 16
KV_B = 4
HD_B = 64
H_I = 8
D_I = 64
TOPK_MAX = 256
ROPE_THETA = 500000.0
N_EXPERTS = 16
N_GROUPS = 4
E_PER_GROUP = 4
D_FF = 512
ALPHA = (2 * DEPTH) ** 0.25
LN_EPS = 1e-5
RMS_EPS = 1e-5
IDX_SCALE = (H_I ** -0.5) * (D_I ** -0.5)

LANES = 128
HEAD = 64
VMEM_LIMIT = 56 * 1024 * 1024
INT_MIN = -2 ** 31
NEG_BIG = -1e30


def _params(sem):
    return pltpu.CompilerParams(dimension_semantics=sem, vmem_limit_bytes=VMEM_LIMIT)


def _dot(a, b):
    return jnp.dot(a, b, preferred_element_type=F32)


def _dot_nt(a, b):
    return lax.dot_general(a, b, (((1,), (1,)), ((), ())), preferred_element_type=F32)


def _div_pow2(x, n):
    assert n & (n - 1) == 0, n
    return x >> (n.bit_length() - 1)


def _layer_norm(y, g, b):
    mu = jnp.mean(y, axis=-1, keepdims=True)
    var = jnp.mean(jnp.square(y - mu), axis=-1, keepdims=True)
    return (y - mu) * lax.rsqrt(var + LN_EPS) * g + b


def _sigmoid(x):
    return 1.0 / (1.0 + jnp.exp(-x))


def _rope_tables_kernel(inv8l, inv16l, inv8c, inv16c, tok8, tok16, tr8, tr16, *, seq, off, tp):
    base = pl.program_id(0) * tp
    row = base + lax.broadcasted_iota(I32, (tp, LANES), 0)
    lane = lax.broadcasted_iota(I32, (tp, LANES), 1)
    pos = (off + (row & (seq - 1))).astype(F32)
    d = lane & (HEAD - 1)
    for half, inv, out in ((8, inv8l, tok8), (16, inv16l, tok16)):
        ang = pos * inv[...]
        c = jnp.cos(ang)
        s = jnp.sin(ang)
        out[0] = jnp.where(d < 2 * half, c, 1.0)
        out[1] = jnp.where(d < half, -s, 0.0)
        out[2] = jnp.where((d >= half) & (d < 2 * half), s, 0.0)
    col = base + lax.broadcasted_iota(I32, (1, tp), 1)
    posr = (off + (col & (seq - 1))).astype(F32)
    for inv, out in ((inv8c, tr8), (inv16c, tr16)):
        ang = posr * inv[...]
        out[0] = jnp.cos(ang)
        out[1] = jnp.sin(ang)


def _rope_tables(n_pos, seq, off):
    tp = min(n_pos, 256)
    lane = jnp.arange(LANES) % HEAD
    invs = []
    for half in (8, 16):
        inv = ROPE_THETA ** (-jnp.arange(half, dtype=F32) / half)
        invs.append((jnp.where(lane < 2 * half, inv[lane % half], 0.0).reshape(1, LANES), inv.reshape(half, 1)))
    full = lambda shape: pl.BlockSpec(shape, lambda i: (0,) * len(shape))
    return pl.pallas_call(
        functools.partial(_rope_tables_kernel, seq=seq, off=off, tp=tp),
        grid=(n_pos // tp,),
        in_specs=[full((1, LANES)), full((1, LANES)), full((8, 1)), full((16, 1))],
        out_specs=[
            pl.BlockSpec((3, tp, LANES), lambda i: (0, i, 0)),
            pl.BlockSpec((3, tp, LANES), lambda i: (0, i, 0)),
            pl.BlockSpec((2, 8, tp), lambda i: (0, 0, i)),
            pl.BlockSpec((2, 16, tp), lambda i: (0, 0, i)),
        ],
        out_shape=[
            jax.ShapeDtypeStruct((3, n_pos, LANES), F32),
            jax.ShapeDtypeStruct((3, n_pos, LANES), F32),
            jax.ShapeDtypeStruct((2, 8, n_pos), F32),
            jax.ShapeDtypeStruct((2, 16, n_pos), F32),
        ],
        compiler_params=_params(("arbitrary",)),
        name="rope_tables",
    )(invs[0][0], invs[1][0], invs[0][1], invs[1][1])


def _rope_tok(y, tab, half):
    c, a, b = tab[0], tab[1], tab[2]
    outs = []
    for k in range(y.shape[1] // LANES):
        yc = y[:, k * LANES:(k + 1) * LANES]
        outs.append(yc * c + pltpu.roll(yc, LANES - half, 1) * a + pltpu.roll(yc, half, 1) * b)
    return outs[0] if len(outs) == 1 else jnp.concatenate(outs, axis=1)


def _rope_tr(yt, tr, half):
    cos, sin = tr[0], tr[1]
    parts = []
    for h in range(yt.shape[0] // HEAD):
        b = h * HEAD
        x1 = yt[b:b + half]
        x2 = yt[b + half:b + 2 * half]
        parts += [x1 * cos - x2 * sin, x2 * cos + x1 * sin, yt[b + 2 * half:b + HEAD]]
    return jnp.concatenate(parts, axis=0)


def _inproj_kernel(*refs, segs):
    x_ref, tok8, tok16, tr8, tr16 = refs[:5]
    w_refs = refs[5:5 + len(segs)]
    out_refs = list(refs[5 + len(segs):])
    xb = x_ref[...].astype(BF16)
    for seg, w_ref in zip(segs, w_refs):
        if seg["tr"]:
            y = _dot_nt(w_ref[...], xb)
            if seg["rot"]:
                y = _rope_tr(y, tr8 if seg["rot"] == 8 else tr16, seg["rot"])
        else:
            y = _dot(xb, w_ref[...])
            if seg["rot"]:
                y = _rope_tok(y, tok8 if seg["rot"] == 8 else tok16, seg["rot"])
        if seg["scale"] != 1.0:
            y = y * seg["scale"]
        for dt, split in seg["outs"]:
            o_ref = out_refs.pop(0)
            if split:
                for g in range(y.shape[1] // HEAD):
                    o_ref[g] = y[:, g * HEAD:(g + 1) * HEAD].astype(dt)
            else:
                o_ref[...] = y.astype(dt)


def _inproj(x, tabs, segs, *, seq, tm):
    t, k = x.shape
    nb = t // seq
    tps = max(seq // tm, 1)
    n_tab = tabs[0].shape[1] // tm
    in_specs = [
        pl.BlockSpec((tm, k), lambda i: (i, 0)),
        pl.BlockSpec((3, tm, LANES), lambda i: (0, i % n_tab, 0)),
        pl.BlockSpec((3, tm, LANES), lambda i: (0, i % n_tab, 0)),
        pl.BlockSpec((2, 8, tm), lambda i: (0, 0, i % n_tab)),
        pl.BlockSpec((2, 16, tm), lambda i: (0, 0, i % n_tab)),
    ]
    out_specs, out_shape = [], []
    for seg in segs:
        w = seg["w"]
        in_specs.append(pl.BlockSpec(w.shape, lambda i: (0, 0)))
        n = w.shape[0] if seg["tr"] else w.shape[1]
        for dt, split in seg["outs"]:
            if seg["tr"]:
                out_specs.append(pl.BlockSpec((None, n, tm), lambda i: (i // tps, 0, i % tps)))
                out_shape.append(jax.ShapeDtypeStruct((nb, n, seq), dt))
            elif split:
                out_specs.append(pl.BlockSpec((n // HEAD, tm, HEAD), lambda i: (0, i, 0)))
                out_shape.append(jax.ShapeDtypeStruct((n // HEAD, t, HEAD), dt))
            else:
                out_specs.append(pl.BlockSpec((tm, n), lambda i: (i, 0)))
                out_shape.append(jax.ShapeDtypeStruct((t, n), dt))
    kern_segs = tuple({k2: v for k2, v in seg.items() if k2 != "w"} for seg in segs)
    return pl.pallas_call(
        functools.partial(_inproj_kernel, segs=kern_segs),
        grid=(t // tm,),
        in_specs=in_specs,
        out_specs=out_specs,
        out_shape=out_shape,
        compiler_params=_params(("arbitrary",)),
        name="inproj",
    )(x, *tabs, *[seg["w"] for seg in segs])


def _lambda(lq1, lk1, lq2, lk2, lam_init):
    return (jnp.exp(jnp.sum(lq1[...] * lk1[...], axis=1, keepdims=True))
            - jnp.exp(jnp.sum(lq2[...] * lk2[...], axis=1, keepdims=True)) + lam_init)


def _sub_norm(o, sub, lam_init):
    return o * lax.rsqrt(jnp.mean(o * o, axis=-1, keepdims=True) + RMS_EPS) * sub * (1.0 - lam_init)


def _diff_attn_kernel(q_ref, kt_ref, v_ref, lq1, lk1, lq2, lk2, sub_ref, o_ref, *, tq, lam_init):
    qi = pl.program_id(2)
    q = q_ref[...]

    def step(j, carry, masked):
        off = pl.multiple_of(j * tq, tq)
        v = v_ref[pl.ds(off, tq), :]
        new = []
        for c in range(2):
            m, l, acc = carry[c]
            kt = kt_ref[c * HEAD:(c + 1) * HEAD, pl.ds(off, tq)]
            s = _dot(q[:, c * HEAD:(c + 1) * HEAD], kt)
            if masked:
                row = lax.broadcasted_iota(I32, (tq, tq), 0)
                col = lax.broadcasted_iota(I32, (tq, tq), 1)
                s = jnp.where(col <= row, s, -jnp.inf)
            m_new = jnp.maximum(m, jnp.max(s, axis=1, keepdims=True))
            alpha = jnp.exp(m - m_new)
            p = jnp.exp(s - m_new)
            l = alpha * l + jnp.sum(p, axis=1, keepdims=True)
            acc = alpha * acc + _dot(p.astype(BF16), v)
            new.append((m_new, l, acc))
        return tuple(new)

    init = tuple((jnp.full((tq, 1), -jnp.inf, F32), jnp.zeros((tq, 1), F32), jnp.zeros((tq, 2 * HEAD), F32))
                 for _ in range(2))
    carry = lax.fori_loop(0, qi, functools.partial(step, masked=False), init)
    (_, l0, a0), (_, l1, a1) = step(qi, carry, True)
    lam = _lambda(lq1, lk1, lq2, lk2, lam_init)
    o = a0 / l0 - lam * (a1 / l1)
    o_ref[...] = _sub_norm(o, sub_ref[...], lam_init).astype(o_ref.dtype)


def _diff_attn_prompt(q, kt, v, lams, sub, *, nb, seq, lam_init):
    tq = min(seq, 512)
    nq = seq // tq
    small = lambda shape: pl.BlockSpec(shape, lambda b, h, i: (0, 0))
    return pl.pallas_call(
        functools.partial(_diff_attn_kernel, tq=tq, lam_init=lam_init),
        grid=(nb, H_A, nq),
        in_specs=[
            pl.BlockSpec((tq, 2 * HEAD), lambda b, h, i: (b * nq + i, h)),
            pl.BlockSpec((None, 2 * HEAD, seq), lambda b, h, i: (b, h, 0)),
            pl.BlockSpec((seq, 2 * HEAD), lambda b, h, i: (b, h)),
            small((1, HEAD)), small((1, HEAD)), small((1, HEAD)), small((1, HEAD)), small((1, 2 * HEAD)),
        ],
        out_specs=pl.BlockSpec((tq, 2 * HEAD), lambda b, h, i: (b * nq + i, h)),
        out_shape=jax.ShapeDtypeStruct(q.shape, BF16),
        compiler_params=_params(("arbitrary", "arbitrary", "arbitrary")),
        name="diff_attn_prompt",
    )(q, kt, v, *lams, sub)


PAGES_PER_STEP = 8


def _block_diag_q(q, n_heads, width):
    s = q.shape[0]
    rows = jnp.concatenate([q] * n_heads, axis=0)
    r = _div_pow2(lax.broadcasted_iota(I32, rows.shape, 0), s)
    c = _div_pow2(lax.broadcasted_iota(I32, rows.shape, 1), HEAD)
    return jnp.where(r == c, rows, 0.0)


def _diff_attn_dec_kernel(pt_ref, q_ref, *refs, g, s_new, n_steps, lam_init):
    kt_refs, v_refs = refs[:g], refs[g:2 * g]
    ktn_ref, vn_ref, lq1, lk1, lq2, lk2, sub_ref, o_ref, qbd_ref, m_ref, l_ref, acc_ref = refs[2 * g:]
    step = pl.program_id(1)
    nr = 2 * H_A * s_new

    @pl.when(step == 0)
    def _():
        qbd_ref[...] = _block_diag_q(q_ref[...], 2 * H_A, D_MODEL).astype(BF16)
        m_ref[...] = jnp.full(m_ref.shape, -jnp.inf, F32)
        l_ref[...] = jnp.zeros(l_ref.shape, F32)
        acc_ref[...] = jnp.zeros(acc_ref.shape, F32)

    def consume(kts, page_v_refs, mask):
        kt = kts[0] if len(kts) == 1 else jnp.concatenate(kts, axis=1)
        s = _dot(qbd_ref[...], kt.astype(BF16))
        if mask is not None:
            s = jnp.where(mask, s, -jnp.inf)
        m = m_ref[...]
        m_new = jnp.maximum(m, jnp.max(s, axis=1, keepdims=True))
        alpha = jnp.exp(m - m_new)
        p = jnp.exp(s - m_new)
        l_ref[...] = alpha * l_ref[...] + jnp.sum(p, axis=1, keepdims=True)
        m_ref[...] = m_new
        pb = p.astype(BF16)
        for h in range(H_A):
            r = slice(2 * s_new * h, 2 * s_new * (h + 1))
            vs = [v_ref[pl.ds(h, LANES, stride=H_A), :] for v_ref in page_v_refs]
            v = vs[0] if len(vs) == 1 else jnp.concatenate(vs, axis=0)
            acc_ref[r, :] = alpha[r] * acc_ref[r, :] + _dot(pb[r], v.astype(BF16))

    @pl.when(step < n_steps - 1)
    def _():
        consume([kt_ref[...] for kt_ref in kt_refs], v_refs, None)

    @pl.when(step == n_steps - 1)
    def _():
        key = lax.broadcasted_iota(I32, (nr, LANES), 1)
        tok = lax.broadcasted_iota(I32, (nr, LANES), 0) & (s_new - 1)
        consume([ktn_ref[...]], [vn_ref], key <= tok)
        lam = _lambda(lq1, lk1, lq2, lk2, lam_init)
        o = acc_ref[...] / l_ref[...]
        outs = []
        for h in range(H_A):
            b = 2 * s_new * h
            outs.append(_sub_norm(o[b:b + s_new] - lam * o[b + s_new:b + 2 * s_new], sub_ref[...], lam_init))
        o_ref[...] = jnp.concatenate(outs, axis=1)


def _diff_attn_decode(q, kt_pool, v_pool, page_table, kt_new, v_new, lams, sub, *, s_new, lam_init):
    nb, n_pages = page_table.shape
    g = min(PAGES_PER_STEP, n_pages)
    n_steps = n_pages // g + 1
    nr = 2 * H_A * s_new

    def page_map(k):
        return lambda b, i, pt: (pt[b, jnp.minimum(i, n_steps - 2) * g + k], 0, 0)

    page = lambda k: pl.BlockSpec((None, D_MODEL, LANES), page_map(k))
    new = pl.BlockSpec((None, D_MODEL, LANES), lambda b, i, pt: (b, 0, 0))
    small = lambda shape: pl.BlockSpec(shape, lambda b, i, pt: (0, 0))
    grid_spec = pltpu.PrefetchScalarGridSpec(
        num_scalar_prefetch=1,
        grid=(nb, n_steps),
        in_specs=[pl.BlockSpec((s_new, D_MODEL), lambda b, i, pt: (b, 0))]
        + [page(k) for k in range(g)] + [page(k) for k in range(g)] + [new, new]
        + [small((1, HEAD))] * 4 + [small((1, 2 * HEAD))],
        out_specs=pl.BlockSpec((s_new, D_MODEL), lambda b, i, pt: (b, 0)),
        scratch_shapes=[
            pltpu.VMEM((nr, D_MODEL), BF16),
            pltpu.VMEM((nr, 1), F32),
            pltpu.VMEM((nr, 1), F32),
            pltpu.VMEM((nr, 2 * HEAD), F32),
        ],
    )
    return pl.pallas_call(
        functools.partial(_diff_attn_dec_kernel, g=g, s_new=s_new, n_steps=n_steps, lam_init=lam_init),
        grid_spec=grid_spec,
        out_shape=jax.ShapeDtypeStruct(q.shape, F32),
        compiler_params=_params(("arbitrary", "arbitrary")),
        name="diff_attn_decode",
    )(page_table, q, *([kt_pool] * g), *([v_pool] * g), kt_new, v_new, *lams, sub)


def _float_of_rank(u):
    key = u ^ INT_MIN
    bits = key ^ ((key >> 31) & 0x7FFFFFFF)
    return lax.bitcast_convert_type(bits, F32)


def _count(pred):
    return jnp.sum(pred.astype(I32), axis=1, keepdims=True)


def _write_bias(store, score, valid, col, n_valid, n_sel, idx_bits):
    score = jnp.where(valid, score, jnp.nan)

    def body(i, t_u):
        cand = t_u | lax.shift_left(jnp.int32(1), 31 - i)
        return jnp.where(_count(score >= _float_of_rank(cand)) >= n_sel, cand, t_u)

    t = _float_of_rank(lax.fori_loop(0, 32, body, jnp.zeros((score.shape[0], 1), I32)))
    keep_all = n_valid <= n_sel
    t = jnp.where(keep_all, -jnp.inf, t)
    tie_rows = jnp.logical_not(keep_all) & (_count(score >= t) > n_sel)
    store(jnp.where(score >= t, 0.0, NEG_BIG))

    @pl.when(jnp.max(tie_rows.astype(I32)) > 0)
    def _():
        need = n_sel - _count(score > t)
        eq = score == t

        def idx_body(i, c):
            cand = c | lax.shift_left(jnp.int32(1), idx_bits - 1 - i)
            return jnp.where(_count(eq & (col < cand)) <= need - 1, cand, c)

        c = jnp.where(tie_rows, lax.fori_loop(0, idx_bits, idx_body, jnp.zeros_like(need)), 2 ** 30)
        store(jnp.where((score > t) | (eq & (col <= c)), 0.0, NEG_BIG))


SELECT_WIDTH_STEP = 512
SCORE_CHUNK = 256


def _dsa_select_kernel(qi_ref, w_ref, kit_ref, bias_ref, *, tq, n_sel, wstep):
    seq = kit_ref.shape[1]
    j = pl.program_id(1)
    variant = lax.shift_right_logical((j + 1) * tq + wstep - 1, wstep.bit_length() - 1) - 1

    def run(width):
        qi = qi_ref[...]
        w = w_ref[...]
        chunks = []
        for c in range(width // SCORE_CHUNK):
            kit = kit_ref[:, c * SCORE_CHUNK:(c + 1) * SCORE_CHUNK]
            sc = w[:, 0:1] * jnp.maximum(_dot(qi[:, 0:HEAD], kit), 0.0)
            for h in range(1, H_I):
                sc = sc + w[:, h:h + 1] * jnp.maximum(_dot(qi[:, h * HEAD:(h + 1) * HEAD], kit), 0.0)
            chunks.append(sc)
        score = chunks[0] if len(chunks) == 1 else jnp.concatenate(chunks, axis=1)
        row = j * tq + lax.broadcasted_iota(I32, (tq, width), 0)
        col = lax.broadcasted_iota(I32, (tq, width), 1)

        def store(bias):
            bias_ref[:, :width] = bias.astype(bias_ref.dtype)

        _write_bias(store, score, col <= row, col, row[:, :1] + 1, n_sel, width.bit_length())
        if width < seq:
            bias_ref[:, width:] = jnp.full((tq, seq - width), NEG_BIG, bias_ref.dtype)

    for k in range(seq // wstep):
        pl.when(variant == k)(functools.partial(run, (k + 1) * wstep))


def _dsa_select_prompt(qi, w, kit, *, nb, seq, n_sel):
    tq = min(seq, SELECT_WIDTH_STEP)
    nq = seq // tq
    return pl.pallas_call(
        functools.partial(_dsa_select_kernel, tq=tq, n_sel=n_sel, wstep=min(seq, SELECT_WIDTH_STEP)),
        grid=(nb, nq),
        in_specs=[
            pl.BlockSpec((tq, H_I * HEAD), lambda b, i: (b * nq + i, 0)),
            pl.BlockSpec((tq, LANES), lambda b, i: (b * nq + i, 0)),
            pl.BlockSpec((None, HEAD, seq), lambda b, i: (b, 0, 0)),
        ],
        out_specs=pl.BlockSpec((None, tq, seq), lambda b, i: (b, i, 0)),
        out_shape=jax.ShapeDtypeStruct((nb, seq, seq), BF16),
        compiler_params=_params(("arbitrary", "arbitrary")),
        name="dsa_select_prompt",
    )(qi, w, kit)


def _dsa_select_dec_kernel(pt_ref, qi_ref, w_ref, *refs, g, s_new, n_steps, n_sel, n_past):
    kit_refs = refs[:g]
    kitn_ref, bias_ref, qs_ref, ws_ref, score_ref = refs[g:]
    step = pl.program_id(1)

    @pl.when(step == 0)
    def _():
        qi = qi_ref[...]
        w = w_ref[...]
        qs_ref[...] = jnp.concatenate([qi[:, h * HEAD:(h + 1) * HEAD] for h in range(H_I)], axis=0).astype(BF16)
        ws_ref[...] = jnp.concatenate([w[:, h:h + 1] for h in range(H_I)], axis=0)

    def page_score(kit):
        d = jnp.maximum(_dot(qs_ref[...], kit.astype(BF16)), 0.0) * ws_ref[...]
        sc = d[0:s_new]
        for h in range(1, H_I):
            sc = sc + d[h * s_new:(h + 1) * s_new]
        return sc

    @pl.when(step < n_steps - 1)
    def _():
        off = pl.multiple_of(step * (g * LANES), g * LANES)
        score_ref[:, pl.ds(off, g * LANES)] = page_score(jnp.concatenate([r[...] for r in kit_refs], axis=1))

    @pl.when(step == n_steps - 1)
    def _():
        score_ref[:, n_past:n_past + LANES] = page_score(kitn_ref[...])
        width = n_past + LANES
        col = lax.broadcasted_iota(I32, (s_new, width), 1)
        tok = lax.broadcasted_iota(I32, (s_new, width), 0)

        def store(bias):
            bias_ref[...] = bias

        _write_bias(store, score_ref[...], col <= n_past + tok, col, n_past + tok[:, :1] + 1, n_sel,
                    width.bit_length())


def _dsa_select_decode(qi, w, kit_pool, page_table, kit_new, *, s_new, n_sel):
    nb, n_pages = page_table.shape
    g = min(PAGES_PER_STEP, n_pages)
    n_steps = n_pages // g + 1
    n_past = n_pages * LANES
    width = n_past + LANES

    def page_map(k):
        return lambda b, i, pt: (pt[b, jnp.minimum(i, n_steps - 2) * g + k], 0, 0)

    grid_spec = pltpu.PrefetchScalarGridSpec(
        num_scalar_prefetch=1,
        grid=(nb, n_steps),
        in_specs=[pl.BlockSpec((s_new, H_I * HEAD), lambda b, i, pt: (b, 0)),
                  pl.BlockSpec((s_new, LANES), lambda b, i, pt: (b, 0))]
        + [pl.BlockSpec((None, HEAD, LANES), page_map(k)) for k in range(g)]
        + [pl.BlockSpec((None, HEAD, LANES), lambda b, i, pt: (b, 0, 0))],
        out_specs=pl.BlockSpec((None, s_new, width), lambda b, i, pt: (b, 0, 0)),
        scratch_shapes=[
            pltpu.VMEM((H_I * s_new, HEAD), BF16),
            pltpu.VMEM((H_I * s_new, 1), F32),
            pltpu.VMEM((s_new, width), F32),
        ],
    )
    return pl.pallas_call(
        functools.partial(_dsa_select_dec_kernel, g=g, s_new=s_new, n_steps=n_steps, n_sel=n_sel, n_past=n_past),
        grid_spec=grid_spec,
        out_shape=jax.ShapeDtypeStruct((nb, s_new, width), F32),
        compiler_params=_params(("arbitrary", "arbitrary")),
        name="dsa_select_decode",
    )(page_table, qi, w, *([kit_pool] * g), kit_new)


def _dsa_attn_kernel(q_ref, kt_ref, v_ref, bias_ref, o_ref, *, tq, tk):
    grp = H_B // KV_B
    qi = pl.program_id(2)
    qb = q_ref[...]
    q4 = jnp.concatenate([qb[:, h * HEAD:(h + 1) * HEAD] for h in range(grp)], axis=0)
    n_kv = lax.shift_right_logical((qi + 1) * tq + tk - 1, tk.bit_length() - 1)

    def step(j, carry):
        m, l, acc = carry
        off = pl.multiple_of(j * tk, tk)
        s = _dot(q4, kt_ref[:, pl.ds(off, tk)])
        s = (s.reshape(grp, tq, tk) + bias_ref[:, pl.ds(off, tk)].astype(F32)[None]).reshape(grp * tq, tk)
        m_new = jnp.maximum(m, jnp.max(s, axis=1, keepdims=True))
        alpha = jnp.exp(m - m_new)
        p = jnp.exp(s - m_new)
        l = alpha * l + jnp.sum(p, axis=1, keepdims=True)
        acc = alpha * acc + _dot(p.astype(BF16), v_ref[pl.ds(off, tk), :])
        return m_new, l, acc

    init = (jnp.full((grp * tq, 1), -jnp.inf, F32), jnp.zeros((grp * tq, 1), F32), jnp.zeros((grp * tq, HEAD), F32))
    _, l, acc = lax.fori_loop(0, n_kv, step, init)
    o = acc / l
    o_ref[...] = jnp.concatenate([o[h * tq:(h + 1) * tq] for h in range(grp)], axis=1).astype(o_ref.dtype)


def _dsa_attn_prompt(q, kt, v, bias, *, nb, seq):
    tq = min(seq, 256)
    tk = min(seq, SELECT_WIDTH_STEP)
    nq = seq // tq
    grp = H_B // KV_B
    return pl.pallas_call(
        functools.partial(_dsa_attn_kernel, tq=tq, tk=tk),
        grid=(nb, KV_B, nq),
        in_specs=[
            pl.BlockSpec((tq, grp * HEAD), lambda b, g, i: (b * nq + i, g)),
            pl.BlockSpec((None, HEAD, seq), lambda b, g, i: (b, g, 0)),
            pl.BlockSpec((None, seq, HEAD), lambda b, g, i: (g, b, 0)),
            pl.BlockSpec((None, tq, seq), lambda b, g, i: (b, i, 0)),
        ],
        out_specs=pl.BlockSpec((tq, grp * HEAD), lambda b, g, i: (b * nq + i, g)),
        out_shape=jax.ShapeDtypeStruct(q.shape, BF16),
        compiler_params=_params(("arbitrary", "arbitrary", "arbitrary")),
        name="dsa_attn_prompt",
    )(q, kt, v, bias)


def _dsa_attn_dec_kernel(pt_ref, q_ref, bias_ref, *refs, g, s_new, n_steps, n_past):
    kt_refs, vt_refs = refs[:g], refs[g:2 * g]
    ktn_ref, vtn_ref, o_ref, qbd_ref, m_ref, l_ref, acc_ref = refs[2 * g:]
    step = pl.program_id(1)
    nr = H_B * s_new
    grp = H_B // KV_B

    @pl.when(step == 0)
    def _():
        q = q_ref[...]
        rows = jnp.concatenate([q[:, h * HEAD:(h + 1) * HEAD] for h in range(H_B)], axis=0)
        wide = jnp.concatenate([rows] * KV_B, axis=1)
        r = _div_pow2(lax.broadcasted_iota(I32, wide.shape, 0), grp * s_new)
        c = _div_pow2(lax.broadcasted_iota(I32, wide.shape, 1), HEAD)
        qbd_ref[...] = jnp.where(r == c, wide, 0.0).astype(BF16)
        m_ref[...] = jnp.full(m_ref.shape, -jnp.inf, F32)
        l_ref[...] = jnp.zeros(l_ref.shape, F32)
        acc_ref[...] = jnp.zeros(acc_ref.shape, F32)

    def consume(kt, vt, bias):
        keys = kt.shape[1]
        s = _dot(qbd_ref[...], kt.astype(BF16))
        s = (s.reshape(H_B, s_new, keys) + bias[None]).reshape(nr, keys)
        m = m_ref[...]
        m_new = jnp.maximum(m, jnp.max(s, axis=1, keepdims=True))
        alpha = jnp.exp(m - m_new)
        p = jnp.exp(s - m_new)
        l_ref[...] = alpha * l_ref[...] + jnp.sum(p, axis=1, keepdims=True)
        m_ref[...] = m_new
        acc_ref[...] = alpha * acc_ref[...] + _dot_nt(p.astype(BF16), vt.astype(BF16))

    @pl.when(step < n_steps - 1)
    def _():
        off = pl.multiple_of(step * (g * LANES), g * LANES)
        consume(jnp.concatenate([r[...] for r in kt_refs], axis=1),
                jnp.concatenate([r[...] for r in vt_refs], axis=1), bias_ref[:, pl.ds(off, g * LANES)])

    @pl.when(step == n_steps - 1)
    def _():
        consume(ktn_ref[...], vtn_ref[...], bias_ref[:, n_past:n_past + LANES])
        o = acc_ref[...] / l_ref[...]
        outs = []
        for h in range(H_B):
            kv = h // grp
            outs.append(o[h * s_new:(h + 1) * s_new, kv * HEAD:(kv + 1) * HEAD])
        o_ref[...] = jnp.concatenate(outs, axis=1)


def _dsa_attn_decode(q, bias, kt_pool, vt_pool, page_table, kt_new, vt_new, *, s_new):
    nb, n_pages = page_table.shape
    g = min(PAGES_PER_STEP, n_pages)
    n_steps = n_pages // g + 1
    n_past = n_pages * LANES
    nr = H_B * s_new
    kvw = KV_B * HEAD

    def page_map(k):
        return lambda b, i, pt: (pt[b, jnp.minimum(i, n_steps - 2) * g + k], 0, 0)

    page = lambda k: pl.BlockSpec((None, kvw, LANES), page_map(k))
    new = pl.BlockSpec((None, kvw, LANES), lambda b, i, pt: (b, 0, 0))
    grid_spec = pltpu.PrefetchScalarGridSpec(
        num_scalar_prefetch=1,
        grid=(nb, n_steps),
        in_specs=[pl.BlockSpec((s_new, D_MODEL), lambda b, i, pt: (b, 0)),
                  pl.BlockSpec((None, s_new, n_past + LANES), lambda b, i, pt: (b, 0, 0))]
        + [page(k) for k in range(g)] + [page(k) for k in range(g)] + [new, new],
        out_specs=pl.BlockSpec((s_new, D_MODEL), lambda b, i, pt: (b, 0)),
        scratch_shapes=[
            pltpu.VMEM((nr, kvw), BF16),
            pltpu.VMEM((nr, 1), F32),
            pltpu.VMEM((nr, 1), F32),
            pltpu.VMEM((nr, kvw), F32),
        ],
    )
    return pl.pallas_call(
        functools.partial(_dsa_attn_dec_kernel, g=g, s_new=s_new, n_steps=n_steps, n_past=n_past),
        grid_spec=grid_spec,
        out_shape=jax.ShapeDtypeStruct(q.shape, F32),
        compiler_params=_params(("arbitrary", "arbitrary")),
        name="dsa_attn_decode",
    )(page_table, q, bias, *([kt_pool] * g), *([vt_pool] * g), kt_new, vt_new)


def _outproj_ln_kernel(o_ref, w_ref, x_ref, g_ref, b_ref, out_ref):
    h = _dot(o_ref[...].astype(BF16), w_ref[...])
    out_ref[...] = _layer_norm(ALPHA * x_ref[...] + h, g_ref[...], b_ref[...])


def _outproj_ln(o, w, x, g, b, *, tm):
    t, d = x.shape
    row = lambda n: pl.BlockSpec((tm, n), lambda i: (i, 0))
    full = lambda shape: pl.BlockSpec(shape, lambda i: (0, 0))
    return pl.pallas_call(
        _outproj_ln_kernel,
        grid=(t // tm,),
        in_specs=[row(o.shape[1]), full(w.shape), row(d), full((1, d)), full((1, d))],
        out_specs=row(d),
        out_shape=jax.ShapeDtypeStruct((t, d), F32),
        compiler_params=_params(("arbitrary",)),
        name="outproj_ln",
    )(o, w, x, g, b)


def _route(sel, aff):
    gs = []
    for g in range(N_GROUPS):
        a, b, c, d = sel[4 * g:4 * g + 4]
        hi1, lo1, hi2, lo2 = jnp.maximum(a, b), jnp.minimum(a, b), jnp.maximum(c, d), jnp.minimum(c, d)
        gs.append(jnp.maximum(hi1, hi2) + jnp.maximum(jnp.minimum(hi1, hi2), jnp.maximum(lo1, lo2)))
    best, gi = gs[0], jnp.zeros(gs[0].shape, I32)
    for g in range(1, N_GROUPS):
        better = gs[g] > best
        best = jnp.where(better, gs[g], best)
        gi = jnp.where(better, g, gi)

    def pick(rows):
        out = []
        for j in range(E_PER_GROUP):
            v = rows[j]
            for g in range(1, N_GROUPS):
                v = jnp.where(gi == g, rows[4 * g + j], v)
            out.append(v)
        return out

    sv, av = pick(sel), pick(aff)

    def argmax_first(vals):
        bv, bi = vals[0], jnp.zeros(vals[0].shape, I32)
        for j in range(1, E_PER_GROUP):
            better = vals[j] > bv
            bv = jnp.where(better, vals[j], bv)
            bi = jnp.where(better, j, bi)
        return bi

    i1 = argmax_first(sv)
    i2 = argmax_first([jnp.where(i1 == j, -jnp.inf, sv[j]) for j in range(E_PER_GROUP)])
    g1, g2 = av[0], av[0]
    for j in range(1, E_PER_GROUP):
        g1 = jnp.where(i1 == j, av[j], g1)
        g2 = jnp.where(i2 == j, av[j], g2)
    tot = g1 + g2
    g1, g2 = g1 / tot, g2 / tot
    comb = []
    for e in range(N_EXPERTS):
        g, j = divmod(e, E_PER_GROUP)
        in_g = gi == g
        comb.append(jnp.where(in_g & (i1 == j), g1, jnp.where(in_g & (i2 == j), g2, 0.0)))
    return comb


def _router_rows(x, rwt_hi, rwt_lo, rb):
    xh = x.astype(BF16)
    xl = (x - xh.astype(F32)).astype(BF16)
    logits = _dot_nt(rwt_hi, xh) + (_dot_nt(rwt_hi, xl) + _dot_nt(rwt_lo, xh))
    aff = _sigmoid(logits)
    sel = aff + rb
    return [sel[e:e + 1] for e in range(N_EXPERTS)], [aff[e:e + 1] for e in range(N_EXPERTS)]


def _moe_dense_kernel(x_ref, rwh_ref, rwl_ref, rb_ref, wg_ref, wu_ref, wd_ref, g_ref, b_ref, out_ref,
                      xb_ref, comb_ref, acc_ref, *, tm):
    e = pl.program_id(1)

    @pl.when(e == 0)
    def _():
        x = x_ref[...]
        xb_ref[...] = x.astype(BF16)
        sel, aff = _router_rows(x, rwh_ref[...], rwl_ref[...], rb_ref[...])
        comb = jnp.concatenate(_route(sel, aff) + [jnp.zeros((LANES - N_EXPERTS, tm), F32)], axis=0)
        comb_ref[...] = comb.T
        acc_ref[...] = jnp.zeros(acc_ref.shape, F32)

    xb = xb_ref[...]
    hg = _dot(xb, wg_ref[...])
    hu = _dot(xb, wu_ref[...])
    hdn = (hg * _sigmoid(hg)) * hu
    y = _dot(hdn.astype(BF16), wd_ref[...])
    lane = lax.broadcasted_iota(I32, (tm, LANES), 1)
    col = jnp.sum(jnp.where(lane == e, comb_ref[...], 0.0), axis=1, keepdims=True)
    acc_ref[...] += col * y

    @pl.when(e == N_EXPERTS - 1)
    def _():
        out_ref[...] = _layer_norm(ALPHA * x_ref[...] + acc_ref[...], g_ref[...], b_ref[...])


def _moe_ln(x, rwt_hi, rwt_lo, rb, wg, wu, wd, g, b, *, tm):
    t, d = x.shape
    f = wg.shape[2]
    full = lambda shape: pl.BlockSpec(shape, lambda i, e: (0,) * len(shape))
    return pl.pallas_call(
        functools.partial(_moe_dense_kernel, tm=tm),
        grid=(t // tm, N_EXPERTS),
        in_specs=[
            pl.BlockSpec((tm, d), lambda i, e: (i, 0)),
            full((N_EXPERTS, d)), full((N_EXPERTS, d)), full((N_EXPERTS, 1)),
            pl.BlockSpec((None, d, f), lambda i, e: (e, 0, 0)),
            pl.BlockSpec((None, d, f), lambda i, e: (e, 0, 0)),
            pl.BlockSpec((None, f, d), lambda i, e: (e, 0, 0)),
            full((1, d)), full((1, d)),
        ],
        out_specs=pl.BlockSpec((tm, d), lambda i, e: (i, 0)),
        out_shape=jax.ShapeDtypeStruct((t, d), F32),
        scratch_shapes=[pltpu.VMEM((tm, d), BF16), pltpu.VMEM((tm, LANES), F32), pltpu.VMEM((tm, d), F32)],
        compiler_params=_params(("arbitrary", "arbitrary")),
        name="moe_ln",
    )(x, rwt_hi, rwt_lo, rb, wg, wu, wd, g, b)


def _ple_kernel(x_ref, p_ref, wg_ref, bg_ref, wp_ref, out_ref):
    x = x_ref[...]
    gate = _sigmoid(_dot(x.astype(BF16), wg_ref[...]) + bg_ref[...])
    out_ref[...] = x + gate * _dot(p_ref[...].astype(BF16), wp_ref[...])


def _ple(x, p, wg, bg, wp, *, tm):
    t, d = x.shape
    pd = p.shape[1]
    full = lambda shape: pl.BlockSpec(shape, lambda i: (0, 0))
    return pl.pallas_call(
        _ple_kernel,
        grid=(t // tm,),
        in_specs=[pl.BlockSpec((tm, d), lambda i: (i, 0)), pl.BlockSpec((tm, pd), lambda i: (i, 0)),
                  full((d, d)), full((1, d)), full((pd, d))],
        out_specs=pl.BlockSpec((tm, d), lambda i: (i, 0)),
        out_shape=jax.ShapeDtypeStruct((t, d), F32),
        compiler_params=_params(("arbitrary",)),
        name="ple",
    )(x, p, wg, bg, wp)


def _pad_page_t(x, nb, s_new):
    xt = x.reshape(nb, s_new, -1).transpose(0, 2, 1)
    return jnp.pad(xt, ((0, 0), (0, 0), (0, LANES - s_new)))


def _trunk(x3, p4, pos_off, past, page_table, w):
    nb, seq, d = x3.shape
    t = nb * seq
    prompt = past is None
    x = x3.reshape(t, d)
    p = p4.reshape(DEPTH, t, -1)
    tm = min(256, t)
    if prompt:
        tabs = _rope_tables(seq, seq, pos_off)
    else:
        tabs = _rope_tables(t, seq, pos_off)
    tm_moe = 512 if t % 512 == 0 else tm
    outs = {}

    lam_init = 0.8 - 0.6 * math.exp(-0.3 * 0)
    w_in = w["a_w_in"]
    nq = 2 * H_A * HD_A
    wq, wk, wv = w_in[:, :nq], w_in[:, nq:2 * nq], w_in[:, 2 * nq:]
    lams = [w[n] for n in ("a_lam_q1", "a_lam_k1", "a_lam_q2", "a_lam_k2")]
    if prompt:
        segs = [
            dict(w=wq, tr=False, rot=8, scale=HD_A ** -0.5, outs=[(BF16, False)]),
            dict(w=wk.T, tr=True, rot=8, scale=1.0, outs=[(F32, False), (BF16, False)]),
            dict(w=wv, tr=False, rot=0, scale=1.0, outs=[(F32, False), (BF16, False)]),
        ]
        q, kt, kt_b, v, v_b = _inproj(x, tabs, segs, seq=seq, tm=tm)
        o = _diff_attn_prompt(q, kt_b, v_b, lams, w["a_subln"], nb=nb, seq=seq, lam_init=lam_init)
        outs["a_k"] = kt.reshape(nb, 2 * H_A, HD_A, seq).transpose(0, 3, 1, 2)
    else:
        segs = [
            dict(w=wq, tr=False, rot=8, scale=HD_A ** -0.5, outs=[(F32, False)]),
            dict(w=wk, tr=False, rot=8, scale=1.0, outs=[(F32, False)]),
            dict(w=wv, tr=False, rot=0, scale=1.0, outs=[(F32, False)]),
        ]
        q, k, v = _inproj(x, tabs, segs, seq=seq, tm=tm)
        v_new = jnp.pad(v.reshape(nb, seq, H_A, 2 * HD_A), ((0, 0), (0, LANES - seq), (0, 0), (0, 0)))
        o = _diff_attn_decode(q, past["a_kt"], past["a_v"], page_table, _pad_page_t(k, nb, seq),
                              v_new.reshape(nb, LANES * H_A, 2 * HD_A), lams, w["a_subln"],
                              s_new=seq, lam_init=lam_init)
        outs["a_k"] = k.reshape(nb, seq, 2 * H_A, HD_A)
    outs["a_v"] = v.reshape(nb, seq, H_A, 2 * HD_A)
    x = _outproj_ln(o, w["a_w_out"], x, w["ln_g"][0, 0], w["ln_b"][0, 0], tm=tm)
    x = _moe_ln(x, w["rwt_hi"], w["rwt_lo"], w["rb"], w["moe_w_gate"][0], w["moe_w_up"][0], w["moe_w_down"][0],
                w["ln_g"][0, 1], w["ln_b"][0, 1], tm=tm_moe)
    x = _ple(x, p[0], w["ple_gate_w"][0], w["ple_gate_b"][0], w["ple_proj"][0], tm=tm)

    w_in = w["b_w_in"]
    sizes = (H_B * HD_B, KV_B * HD_B, KV_B * HD_B, H_I * D_I, D_I, H_I)
    offs = [sum(sizes[:m]) for m in range(len(sizes) + 1)]
    wq, wk, wv, wqi, wki, wwi = [w_in[:, offs[m]:offs[m + 1]] for m in range(len(sizes))]
    wwi = jnp.pad(wwi, ((0, 0), (0, LANES - H_I)))
    if prompt:
        n_sel = min(TOPK_MAX, seq // 4)
        segs = [
            dict(w=wq, tr=False, rot=8, scale=HD_B ** -0.5, outs=[(BF16, False)]),
            dict(w=wk.T, tr=True, rot=8, scale=1.0, outs=[(F32, False), (BF16, False)]),
            dict(w=wv.T, tr=True, rot=0, scale=1.0, outs=[(F32, False)]),
            dict(w=wv, tr=False, rot=0, scale=1.0, outs=[(BF16, True)]),
            dict(w=wqi, tr=False, rot=16, scale=1.0, outs=[(BF16, False)]),
            dict(w=wki.T, tr=True, rot=16, scale=1.0, outs=[(F32, False), (BF16, False)]),
            dict(w=wwi, tr=False, rot=0, scale=IDX_SCALE, outs=[(F32, False)]),
        ]
        q, kt, kt_b, vt, v_b, qi, kit, kit_b, wi = _inproj(x, tabs, segs, seq=seq, tm=tm)
        bias = _dsa_select_prompt(qi, wi, kit_b, nb=nb, seq=seq, n_sel=n_sel)
        o = _dsa_attn_prompt(q, kt_b, v_b, bias, nb=nb, seq=seq)
        tr4 = lambda a, h: a.reshape(nb, h, HEAD, seq).transpose(0, 3, 1, 2)
        outs["b_k"] = tr4(kt, KV_B)
        outs["b_v"] = tr4(vt, KV_B)
        outs["b_ki"] = kit.transpose(0, 2, 1)
    else:
        n_past = page_table.shape[1] * LANES
        n_sel = min(TOPK_MAX, (n_past + seq) // 4)
        segs = [
            dict(w=wq, tr=False, rot=8, scale=HD_B ** -0.5, outs=[(F32, False)]),
            dict(w=wk, tr=False, rot=8, scale=1.0, outs=[(F32, False)]),
            dict(w=wv, tr=False, rot=0, scale=1.0, outs=[(F32, False)]),
            dict(w=wqi, tr=False, rot=16, scale=1.0, outs=[(F32, False)]),
            dict(w=jnp.pad(wki, ((0, 0), (0, LANES - D_I))), tr=False, rot=16, scale=1.0, outs=[(F32, False)]),
            dict(w=wwi, tr=False, rot=0, scale=IDX_SCALE, outs=[(F32, False)]),
        ]
        q, k, v, qi, ki, wi = _inproj(x, tabs, segs, seq=seq, tm=tm)
        ki = ki[:, :D_I]
        bias = _dsa_select_decode(qi, wi, past["b_kit"], page_table, _pad_page_t(ki, nb, seq), s_new=seq, n_sel=n_sel)
        o = _dsa_attn_decode(q, bias, past["b_kt"], past["b_vt"], page_table, _pad_page_t(k, nb, seq),
                             _pad_page_t(v, nb, seq), s_new=seq)
        outs["b_k"] = k.reshape(nb, seq, KV_B, HD_B)
        outs["b_v"] = v.reshape(nb, seq, KV_B, HD_B)
        outs["b_ki"] = ki.reshape(nb, seq, D_I)
    x = _outproj_ln(o, w["b_w_out"], x, w["ln_g"][1, 0], w["ln_b"][1, 0], tm=tm)
    x = _moe_ln(x, w["rwt_hi"], w["rwt_lo"], w["rb"], w["moe_w_gate"][1], w["moe_w_up"][1], w["moe_w_down"][1],
                w["ln_g"][1, 1], w["ln_b"][1, 1], tm=tm_moe)
    x = _ple(x, p[1], w["ple_gate_w"][1], w["ple_gate_b"][1], w["ple_proj"][1], tm=tm)
    return x.reshape(nb, seq, d), outs


def _prep_weights(a_w_in, a_w_out, a_lam_q1, a_lam_k1, a_lam_q2, a_lam_k2, a_subln, b_w_in, b_w_out, ln_g, ln_b,
                  router_w, router_b, moe_w_gate, moe_w_up, moe_w_down, ple_proj, ple_gate_w, ple_gate_b):
    rwt = router_w.T
    rwt_hi = rwt.astype(BF16)
    return {
        "a_w_in": a_w_in[0].astype(BF16), "a_w_out": a_w_out[0].astype(BF16),
        "a_lam_q1": a_lam_q1, "a_lam_k1": a_lam_k1, "a_lam_q2": a_lam_q2, "a_lam_k2": a_lam_k2,
        "a_subln": a_subln,
        "b_w_in": b_w_in[0].astype(BF16), "b_w_out": b_w_out[0].astype(BF16),
        "ln_g": ln_g[:, :, None, :], "ln_b": ln_b[:, :, None, :],
        "rwt_hi": rwt_hi, "rwt_lo": (rwt - rwt_hi.astype(F32)).astype(BF16), "rb": router_b.reshape(N_EXPERTS, 1),
        "moe_w_gate": moe_w_gate.astype(BF16), "moe_w_up": moe_w_up.astype(BF16),
        "moe_w_down": moe_w_down.astype(BF16),
        "ple_proj": ple_proj.astype(BF16), "ple_gate_w": ple_gate_w.astype(BF16),
        "ple_gate_b": ple_gate_b[:, None, :],
    }


def kernel(x_prompt, x_sample, cache_a_k, cache_a_v, cache_b_k, cache_b_v, cache_b_kidx, page_table, p_prompt,
           p_sample, a_w_in, a_w_out, a_lam_q1, a_lam_k1, a_lam_q2, a_lam_k2, a_subln, b_w_in, b_w_out, ln_g, ln_b,
           router_w, router_b, moe_w_gate, moe_w_up, moe_w_down, ple_proj, ple_gate_w, ple_gate_b):
    w = _prep_weights(a_w_in, a_w_out, a_lam_q1, a_lam_k1, a_lam_q2, a_lam_k2, a_subln, b_w_in, b_w_out, ln_g, ln_b,
                      router_w, router_b, moe_w_gate, moe_w_up, moe_w_down, ple_proj, ple_gate_w, ple_gate_b)
    n_pool, page = cache_a_k.shape[1], cache_a_k.shape[2]
    past_len = page_table.shape[1] * page
    past = {
        "a_kt": cache_a_k[0].transpose(0, 2, 3, 1).reshape(n_pool, 2 * H_A * HD_A, page),
        "a_v": cache_a_v[0].reshape(n_pool, page * H_A, 2 * HD_A),
        "b_kt": cache_b_k[0].transpose(0, 2, 3, 1).reshape(n_pool, KV_B * HD_B, page),
        "b_vt": cache_b_v[0].transpose(0, 2, 3, 1).reshape(n_pool, KV_B * HD_B, page),
        "b_kit": cache_b_kidx[0].transpose(0, 2, 1),
    }
    y_p, op = _trunk(x_prompt, p_prompt, 0, None, None, w)
    y_s, os_ = _trunk(x_sample, p_sample, past_len, past, page_table, w)
    lead = lambda a: a[None]
    return (y_p, y_s,
            lead(op["a_k"]), lead(op["a_v"]), lead(op["b_k"]), lead(op["b_v"]), lead(op["b_ki"]),
            lead(os_["a_k"]), lead(os_["a_v"]), lead(os_["b_k"]), lead(os_["b_v"]), lead(os_["b_ki"]))
```

```python
import functools
import math

import jax
import jax.numpy as jnp
from jax import lax
from jax.experimental import pallas as pl
from jax.experimental.pallas import tpu as pltpu

F32 = jnp.float32
BF16 = jnp.bfloat16
I32 = jnp.int32

D_MODEL = 1024
DEPTH = 2
H_A = 8
HD_A = 64
H_B = 16
KV_B = 4
HD_B = 64
H_I = 8
D_I = 64
TOPK_MAX = 256
ROPE_THETA = 500000.0
N_EXPERTS = 16
N_GROUPS = 4
E_PER_GROUP = 4
D_FF = 512
ALPHA = (2 * DEPTH) ** 0.25
LN_EPS = 1e-5
RMS_EPS = 1e-5
IDX_SCALE = (H_I ** -0.5) * (D_I ** -0.5)

LANES = 128
HEAD = 64
VMEM_LIMIT = 56 * 1024 * 1024
INT_MIN = -2 ** 31
NEG_BIG = -1e30


def _params(sem):
    return pltpu.CompilerParams(dimension_semantics=sem, vmem_limit_bytes=VMEM_LIMIT)


def _dot(a, b):
    return jnp.dot(a, b, preferred_element_type=F32)


def _dot_nt(a, b):
    return lax.dot_general(a, b, (((1,), (1,)), ((), ())), preferred_element_type=F32)


def _div_pow2(x, n):
    assert n & (n - 1) == 0, n
    return x >> (n.bit_length() - 1)


def _layer_norm(y, g, b):
    mu = jnp.mean(y, axis=-1, keepdims=True)
    var = jnp.mean(jnp.square(y - mu), axis=-1, keepdims=True)
    return (y - mu) * lax.rsqrt(var + LN_EPS) * g + b


def _sigmoid(x):
    return 1.0 / (1.0 + jnp.exp(-x))


def _rope_tables_kernel(inv8l, inv16l, inv8c, inv16c, tok8, tok16, tr8, tr16, *, seq, off, tp):
    base = pl.program_id(0) * tp
    row = base + lax.broadcasted_iota(I32, (tp, LANES), 0)
    lane = lax.broadcasted_iota(I32, (tp, LANES), 1)
    pos = (off + (row & (seq - 1))).astype(F32)
    d = lane & (HEAD - 1)
    for half, inv, out in ((8, inv8l, tok8), (16, inv16l, tok16)):
        ang = pos * inv[...]
        c = jnp.cos(ang)
        s = jnp.sin(ang)
        out[0] = jnp.where(d < 2 * half, c, 1.0)
        out[1] = jnp.where(d < half, -s, 0.0)
        out[2] = jnp.where((d >= half) & (d < 2 * half), s, 0.0)
    col = base + lax.broadcasted_iota(I32, (1, tp), 1)
    posr = (off + (col & (seq - 1))).astype(F32)
    for inv, out in ((inv8c, tr8), (inv16c, tr16)):
        ang = posr * inv[...]
        out[0] = jnp.cos(ang)
        out[1] = jnp.sin(ang)


def _rope_tables(n_pos, seq, off):
    tp = min(n_pos, 256)
    lane = jnp.arange(LANES) % HEAD
    invs = []
    for half in (8, 16):
        inv = ROPE_THETA ** (-jnp.arange(half, dtype=F32) / half)
        invs.append((jnp.where(lane < 2 * half, inv[lane % half], 0.0).reshape(1, LANES), inv.reshape(half, 1)))
    full = lambda shape: pl.BlockSpec(shape, lambda i: (0,) * len(shape))
    return pl.pallas_call(
        functools.partial(_rope_tables_kernel, seq=seq, off=off, tp=tp),
        grid=(n_pos // tp,),
        in_specs=[full((1, LANES)), full((1, LANES)), full((8, 1)), full((16, 1))],
        out_specs=[
            pl.BlockSpec((3, tp, LANES), lambda i: (0, i, 0)),
            pl.BlockSpec((3, tp, LANES), lambda i: (0, i, 0)),
            pl.BlockSpec((2, 8, tp), lambda i: (0, 0, i)),
            pl.BlockSpec((2, 16, tp), lambda i: (0, 0, i)),
        ],
        out_shape=[
            jax.ShapeDtypeStruct((3, n_pos, LANES), F32),
            jax.ShapeDtypeStruct((3, n_pos, LANES), F32),
            jax.ShapeDtypeStruct((2, 8, n_pos), F32),
            jax.ShapeDtypeStruct((2, 16, n_pos), F32),
        ],
        compiler_params=_params(("arbitrary",)),
        name="rope_tables",
    )(invs[0][0], invs[1][0], invs[0][1], invs[1][1])


def _rope_tok(y, tab, half):
    c, a, b = tab[0], tab[1], tab[2]
    outs = []
    for k in range(y.shape[1] // LANES):
        yc = y[:, k * LANES:(k + 1) * LANES]
        outs.append(yc * c + pltpu.roll(yc, LANES - half, 1) * a + pltpu.roll(yc, half, 1) * b)
    return outs[0] if len(outs) == 1 else jnp.concatenate(outs, axis=1)


def _rope_tr(yt, tr, half):
    cos, sin = tr[0], tr[1]
    parts = []
    for h in range(yt.shape[0] // HEAD):
        b = h * HEAD
        x1 = yt[b:b + half]
        x2 = yt[b + half:b + 2 * half]
        parts += [x1 * cos - x2 * sin, x2 * cos + x1 * sin, yt[b + 2 * half:b + HEAD]]
    return jnp.concatenate(parts, axis=0)


def _inproj_kernel(*refs, segs):
    x_ref, tok8, tok16, tr8, tr16 = refs[:5]
    w_refs = refs[5:5 + len(segs)]
    out_refs = list(refs[5 + len(segs):])
    xb = x_ref[...].astype(BF16)
    for seg, w_ref in zip(segs, w_refs):
        if seg["tr"]:
            y = _dot_nt(w_ref[...], xb)
            if seg["rot"]:
                y = _rope_tr(y, tr8 if seg["rot"] == 8 else tr16, seg["rot"])
        else:
            y = _dot(xb, w_ref[...])
            if seg["rot"]:
                y = _rope_tok(y, tok8 if seg["rot"] == 8 else tok16, seg["rot"])
        if seg["scale"] != 1.0:
            y = y * seg["scale"]
        for dt, split in seg["outs"]:
            o_ref = out_refs.pop(0)
            if split:
                for g in range(y.shape[1] // HEAD):
                    o_ref[g] = y[:, g * HEAD:(g + 1) * HEAD].astype(dt)
            else:
                o_ref[...] = y.astype(dt)


def _inproj(x, tabs, segs, *, seq, tm):
    t, k = x.shape
    nb = t // seq
    tps = max(seq // tm, 1)
    n_tab = tabs[0].shape[1] // tm
    in_specs = [
        pl.BlockSpec((tm, k), lambda i: (i, 0)),
        pl.BlockSpec((3, tm, LANES), lambda i: (0, i % n_tab, 0)),
        pl.BlockSpec((3, tm, LANES), lambda i: (0, i % n_tab, 0)),
        pl.BlockSpec((2, 8, tm), lambda i: (0, 0, i % n_tab)),
        pl.BlockSpec((2, 16, tm), lambda i: (0, 0, i % n_tab)),
    ]
    out_specs, out_shape = [], []
    for seg in segs:
        w = seg["w"]
        in_specs.append(pl.BlockSpec(w.shape, lambda i: (0, 0)))
        n = w.shape[0] if seg["tr"] else w.shape[1]
        for dt, split in seg["outs"]:
            if seg["tr"]:
                out_specs.append(pl.BlockSpec((None, n, tm), lambda i: (i // tps, 0, i % tps)))
                out_shape.append(jax.ShapeDtypeStruct((nb, n, seq), dt))
            elif split:
                out_specs.append(pl.BlockSpec((n // HEAD, tm, HEAD), lambda i: (0, i, 0)))
                out_shape.append(jax.ShapeDtypeStruct((n // HEAD, t, HEAD), dt))
            else:
                out_specs.append(pl.BlockSpec((tm, n), lambda i: (i, 0)))
                out_shape.append(jax.ShapeDtypeStruct((t, n), dt))
    kern_segs = tuple({k2: v for k2, v in seg.items() if k2 != "w"} for seg in segs)
    return pl.pallas_call(
        functools.partial(_inproj_kernel, segs=kern_segs),
        grid=(t // tm,),
        in_specs=in_specs,
        out_specs=out_specs,
        out_shape=out_shape,
        compiler_params=_params(("arbitrary",)),
        name="inproj",
    )(x, *tabs, *[seg["w"] for seg in segs])


def _lambda(lq1, lk1, lq2, lk2, lam_init):
    return (jnp.exp(jnp.sum(lq1[...] * lk1[...], axis=1, keepdims=True))
            - jnp.exp(jnp.sum(lq2[...] * lk2[...], axis=1, keepdims=True)) + lam_init)


def _sub_norm(o, sub, lam_init):
    return o * lax.rsqrt(jnp.mean(o * o, axis=-1, keepdims=True) + RMS_EPS) * sub * (1.0 - lam_init)


def _diff_attn_kernel(q_ref, kt_ref, v_ref, lq1, lk1, lq2, lk2, sub_ref, o_ref, *, tq, lam_init):
    qi = pl.program_id(2)
    q = q_ref[...]

    def step(j, carry, masked):
        off = pl.multiple_of(j * tq, tq)
        v = v_ref[pl.ds(off, tq), :]
        new = []
        for c in range(2):
            m, l, acc = carry[c]
            kt = kt_ref[c * HEAD:(c + 1) * HEAD, pl.ds(off, tq)]
            s = _dot(q[:, c * HEAD:(c + 1) * HEAD], kt)
            if masked:
                row = lax.broadcasted_iota(I32, (tq, tq), 0)
                col = lax.broadcasted_iota(I32, (tq, tq), 1)
                s = jnp.where(col <= row, s, -jnp.inf)
            m_new = jnp.maximum(m, jnp.max(s, axis=1, keepdims=True))
            alpha = jnp.exp(m - m_new)
            p = jnp.exp(s - m_new)
            l = alpha * l + jnp.sum(p, axis=1, keepdims=True)
            acc = alpha * acc + _dot(p.astype(BF16), v)
            new.append((m_new, l, acc))
        return tuple(new)

    init = tuple((jnp.full((tq, 1), -jnp.inf, F32), jnp.zeros((tq, 1), F32), jnp.zeros((tq, 2 * HEAD), F32))
                 for _ in range(2))
    carry = lax.fori_loop(0, qi, functools.partial(step, masked=False), init)
    (_, l0, a0), (_, l1, a1) = step(qi, carry, True)
    lam = _lambda(lq1, lk1, lq2, lk2, lam_init)
    o = a0 / l0 - lam * (a1 / l1)
    o_ref[...] = _sub_norm(o, sub_ref[...], lam_init).astype(o_ref.dtype)


def _diff_attn_prompt(q, kt, v, lams, sub, *, nb, seq, lam_init):
    tq = min(seq, 512)
    nq = seq // tq
    small = lambda shape: pl.BlockSpec(shape, lambda b, h, i: (0, 0))
    return pl.pallas_call(
        functools.partial(_diff_attn_kernel, tq=tq, lam_init=lam_init),
        grid=(nb, H_A, nq),
        in_specs=[
            pl.BlockSpec((tq, 2 * HEAD), lambda b, h, i: (b * nq + i, h)),
            pl.BlockSpec((None, 2 * HEAD, seq), lambda b, h, i: (b, h, 0)),
            pl.BlockSpec((seq, 2 * HEAD), lambda b, h, i: (b, h)),
            small((1, HEAD)), small((1, HEAD)), small((1, HEAD)), small((1, HEAD)), small((1, 2 * HEAD)),
        ],
        out_specs=pl.BlockSpec((tq, 2 * HEAD), lambda b, h, i: (b * nq + i, h)),
        out_shape=jax.ShapeDtypeStruct(q.shape, BF16),
        compiler_params=_params(("arbitrary", "arbitrary", "arbitrary")),
        name="diff_attn_prompt",
    )(q, kt, v, *lams, sub)


PAGES_PER_STEP = 8


def _block_diag_q(q, n_heads, width):
    s = q.shape[0]
    rows = jnp.concatenate([q] * n_heads, axis=0)
    r = _div_pow2(lax.broadcasted_iota(I32, rows.shape, 0), s)
    c = _div_pow2(lax.broadcasted_iota(I32, rows.shape, 1), HEAD)
    return jnp.where(r == c, rows, 0.0)


def _diff_attn_dec_kernel(pt_ref, q_ref, *refs, g, s_new, n_steps, lam_init):
    kt_refs, v_refs = refs[:g], refs[g:2 * g]
    ktn_ref, vn_ref, lq1, lk1, lq2, lk2, sub_ref, o_ref, qbd_ref, m_ref, l_ref, acc_ref = refs[2 * g:]
    step = pl.program_id(1)
    nr = 2 * H_A * s_new

    @pl.when(step == 0)
    def _():
        qbd_ref[...] = _block_diag_q(q_ref[...], 2 * H_A, D_MODEL).astype(BF16)
        m_ref[...] = jnp.full(m_ref.shape, -jnp.inf, F32)
        l_ref[...] = jnp.zeros(l_ref.shape, F32)
        acc_ref[...] = jnp.zeros(acc_ref.shape, F32)

    def consume(kts, page_v_refs, mask):
        kt = kts[0] if len(kts) == 1 else jnp.concatenate(kts, axis=1)
        s = _dot(qbd_ref[...], kt.astype(BF16))
        if mask is not None:
            s = jnp.where(mask, s, -jnp.inf)
        m = m_ref[...]
        m_new = jnp.maximum(m, jnp.max(s, axis=1, keepdims=True))
        alpha = jnp.exp(m - m_new)
        p = jnp.exp(s - m_new)
        l_ref[...] = alpha * l_ref[...] + jnp.sum(p, axis=1, keepdims=True)
        m_ref[...] = m_new
        pb = p.astype(BF16)
        for h in range(H_A):
            r = slice(2 * s_new * h, 2 * s_new * (h + 1))
            vs = [v_ref[pl.ds(h, LANES, stride=H_A), :] for v_ref in page_v_refs]
            v = vs[0] if len(vs) == 1 else jnp.concatenate(vs, axis=0)
            acc_ref[r, :] = alpha[r] * acc_ref[r, :] + _dot(pb[r], v.astype(BF16))

    @pl.when(step < n_steps - 1)
    def _():
        consume([kt_ref[...] for kt_ref in kt_refs], v_refs, None)

    @pl.when(step == n_steps - 1)
    def _():
        key = lax.broadcasted_iota(I32, (nr, LANES), 1)
        tok = lax.broadcasted_iota(I32, (nr, LANES), 0) & (s_new - 1)
        consume([ktn_ref[...]], [vn_ref], key <= tok)
        lam = _lambda(lq1, lk1, lq2, lk2, lam_init)
        o = acc_ref[...] / l_ref[...]
        outs = []
        for h in range(H_A):
            b = 2 * s_new * h
            outs.append(_sub_norm(o[b:b + s_new] - lam * o[b + s_new:b + 2 * s_new], sub_ref[...], lam_init))
        o_ref[...] = jnp.concatenate(outs, axis=1)


def _diff_attn_decode(q, kt_pool, v_pool, page_table, kt_new, v_new, lams, sub, *, s_new, lam_init):
    nb, n_pages = page_table.shape
    g = min(PAGES_PER_STEP, n_pages)
    n_steps = n_pages // g + 1
    nr = 2 * H_A * s_new

    def page_map(k):
        return lambda b, i, pt: (pt[b, jnp.minimum(i, n_steps - 2) * g + k], 0, 0)

    page = lambda k: pl.BlockSpec((None, D_MODEL, LANES), page_map(k))
    new = pl.BlockSpec((None, D_MODEL, LANES), lambda b, i, pt: (b, 0, 0))
    small = lambda shape: pl.BlockSpec(shape, lambda b, i, pt: (0, 0))
    grid_spec = pltpu.PrefetchScalarGridSpec(
        num_scalar_prefetch=1,
        grid=(nb, n_steps),
        in_specs=[pl.BlockSpec((s_new, D_MODEL), lambda b, i, pt: (b, 0))]
        + [page(k) for k in range(g)] + [page(k) for k in range(g)] + [new, new]
        + [small((1, HEAD))] * 4 + [small((1, 2 * HEAD))],
        out_specs=pl.BlockSpec((s_new, D_MODEL), lambda b, i, pt: (b, 0)),
        scratch_shapes=[
            pltpu.VMEM((nr, D_MODEL), BF16),
            pltpu.VMEM((nr, 1), F32),
            pltpu.VMEM((nr, 1), F32),
            pltpu.VMEM((nr, 2 * HEAD), F32),
        ],
    )
    return pl.pallas_call(
        functools.partial(_diff_attn_dec_kernel, g=g, s_new=s_new, n_steps=n_steps, lam_init=lam_init),
        grid_spec=grid_spec,
        out_shape=jax.ShapeDtypeStruct(q.shape, F32),
        compiler_params=_params(("arbitrary", "arbitrary")),
        name="diff_attn_decode",
    )(page_table, q, *([kt_pool] * g), *([v_pool] * g), kt_new, v_new, *lams, sub)


def _float_of_rank(u):
    key = u ^ INT_MIN
    bits = key ^ ((key >> 31) & 0x7FFFFFFF)
    return lax.bitcast_convert_type(bits, F32)


def _count(pred):
    return jnp.sum(pred.astype(I32), axis=1, keepdims=True)


def _write_bias(store, score, valid, col, n_valid, n_sel, idx_bits):
    score = jnp.where(valid, score, jnp.nan)

    def body(i, t_u):
        cand = t_u | lax.shift_left(jnp.int32(1), 31 - i)
        return jnp.where(_count(score >= _float_of_rank(cand)) >= n_sel, cand, t_u)

    t = _float_of_rank(lax.fori_loop(0, 32, body, jnp.zeros((score.shape[0], 1), I32)))
    keep_all = n_valid <= n_sel
    t = jnp.where(keep_all, -jnp.inf, t)
    tie_rows = jnp.logical_not(keep_all) & (_count(score >= t) > n_sel)
    store(jnp.where(score >= t, 0.0, NEG_BIG))

    @pl.when(jnp.max(tie_rows.astype(I32)) > 0)
    def _():
        need = n_sel - _count(score > t)
        eq = score == t

        def idx_body(i, c):
            cand = c | lax.shift_left(jnp.int32(1), idx_bits - 1 - i)
            return jnp.where(_count(eq & (col < cand)) <= need - 1, cand, c)

        c = jnp.where(tie_rows, lax.fori_loop(0, idx_bits, idx_body, jnp.zeros_like(need)), 2 ** 30)
        store(jnp.where((score > t) | (eq & (col <= c)), 0.0, NEG_BIG))


SELECT_WIDTH_STEP = 512
SCORE_CHUNK = 256


def _dsa_select_kernel(qi_ref, w_ref, kit_ref, bias_ref, *, tq, n_sel, wstep):
    seq = kit_ref.shape[1]
    j = pl.program_id(1)
    variant = lax.shift_right_logical((j + 1) * tq + wstep - 1, wstep.bit_length() - 1) - 1

    def run(width):
        qi = qi_ref[...]
        w = w_ref[...]
        chunks = []
        for c in range(width // SCORE_CHUNK):
            kit = kit_ref[:, c * SCORE_CHUNK:(c + 1) * SCORE_CHUNK]
            sc = w[:, 0:1] * jnp.maximum(_dot(qi[:, 0:HEAD], kit), 0.0)
            for h in range(1, H_I):
                sc = sc + w[:, h:h + 1] * jnp.maximum(_dot(qi[:, h * HEAD:(h + 1) * HEAD], kit), 0.0)
            chunks.append(sc)
        score = chunks[0] if len(chunks) == 1 else jnp.concatenate(chunks, axis=1)
        row = j * tq + lax.broadcasted_iota(I32, (tq, width), 0)
        col = lax.broadcasted_iota(I32, (tq, width), 1)

        def store(bias):
            bias_ref[:, :width] = bias.astype(bias_ref.dtype)

        _write_bias(store, score, col <= row, col, row[:, :1] + 1, n_sel, width.bit_length())
        if width < seq:
            bias_ref[:, width:] = jnp.full((tq, seq - width), NEG_BIG, bias_ref.dtype)

    for k in range(seq // wstep):
        pl.when(variant == k)(functools.partial(run, (k + 1) * wstep))


def _dsa_select_prompt(qi, w, kit, *, nb, seq, n_sel):
    tq = min(seq, SELECT_WIDTH_STEP)
    nq = seq // tq
    return pl.pallas_call(
        functools.partial(_dsa_select_kernel, tq=tq, n_sel=n_sel, wstep=min(seq, SELECT_WIDTH_STEP)),
        grid=(nb, nq),
        in_specs=[
            pl.BlockSpec((tq, H_I * HEAD), lambda b, i: (b * nq + i, 0)),
            pl.BlockSpec((tq, LANES), lambda b, i: (b * nq + i, 0)),
            pl.BlockSpec((None, HEAD, seq), lambda b, i: (b, 0, 0)),
        ],
        out_specs=pl.BlockSpec((None, tq, seq), lambda b, i: (b, i, 0)),
        out_shape=jax.ShapeDtypeStruct((nb, seq, seq), BF16),
        compiler_params=_params(("arbitrary", "arbitrary")),
        name="dsa_select_prompt",
    )(qi, w, kit)


def _dsa_select_dec_kernel(pt_ref, qi_ref, w_ref, *refs, g, s_new, n_steps, n_sel, n_past):
    kit_refs = refs[:g]
    kitn_ref, bias_ref, qs_ref, ws_ref, score_ref = refs[g:]
    step = pl.program_id(1)

    @pl.when(step == 0)
    def _():
        qi = qi_ref[...]
        w = w_ref[...]
        qs_ref[...] = jnp.concatenate([qi[:, h * HEAD:(h + 1) * HEAD] for h in range(H_I)], axis=0).astype(BF16)
        ws_ref[...] = jnp.concatenate([w[:, h:h + 1] for h in range(H_I)], axis=0)

    def page_score(kit):
        d = jnp.maximum(_dot(qs_ref[...], kit.astype(BF16)), 0.0) * ws_ref[...]
        sc = d[0:s_new]
        for h in range(1, H_I):
            sc = sc + d[h * s_new:(h + 1) * s_new]
        return sc

    @pl.when(step < n_steps - 1)
    def _():
        off = pl.multiple_of(step * (g * LANES), g * LANES)
        score_ref[:, pl.ds(off, g * LANES)] = page_score(jnp.concatenate([r[...] for r in kit_refs], axis=1))

    @pl.when(step == n_steps - 1)
    def _():
        score_ref[:, n_past:n_past + LANES] = page_score(kitn_ref[...])
        width = n_past + LANES
        col = lax.broadcasted_iota(I32, (s_new, width), 1)
        tok = lax.broadcasted_iota(I32, (s_new, width), 0)

        def store(bias):
            bias_ref[...] = bias

        _write_bias(store, score_ref[...], col <= n_past + tok, col, n_past + tok[:, :1] + 1, n_sel,
                    width.bit_length())


def _dsa_select_decode(qi, w, kit_pool, page_table, kit_new, *, s_new, n_sel):
    nb, n_pages = page_table.shape
    g = min(PAGES_PER_STEP, n_pages)
    n_steps = n_pages // g + 1
    n_past = n_pages * LANES
    width = n_past + LANES

    def page_map(k):
        return lambda b, i, pt: (pt[b, jnp.minimum(i, n_steps - 2) * g + k], 0, 0)

    grid_spec = pltpu.PrefetchScalarGridSpec(
        num_scalar_prefetch=1,
        grid=(nb, n_steps),
        in_specs=[pl.BlockSpec((s_new, H_I * HEAD), lambda b, i, pt: (b, 0)),
                  pl.BlockSpec((s_new, LANES), lambda b, i, pt: (b, 0))]
        + [pl.BlockSpec((None, HEAD, LANES), page_map(k)) for k in range(g)]
        + [pl.BlockSpec((None, HEAD, LANES), lambda b, i, pt: (b, 0, 0))],
        out_specs=pl.BlockSpec((None, s_new, width), lambda b, i, pt: (b, 0, 0)),
        scratch_shapes=[
            pltpu.VMEM((H_I * s_new, HEAD), BF16),
            pltpu.VMEM((H_I * s_new, 1), F32),
            pltpu.VMEM((s_new, width), F32),
        ],
    )
    return pl.pallas_call(
        functools.partial(_dsa_select_dec_kernel, g=g, s_new=s_new, n_steps=n_steps, n_sel=n_sel, n_past=n_past),
        grid_spec=grid_spec,
        out_shape=jax.ShapeDtypeStruct((nb, s_new, width), F32),
        compiler_params=_params(("arbitrary", "arbitrary")),
        name="dsa_select_decode",
    )(page_table, qi, w, *([kit_pool] * g), kit_new)


def _dsa_attn_kernel(q_ref, kt_ref, v_ref, bias_ref, o_ref, *, tq, tk):
    grp = H_B // KV_B
    qi = pl.program_id(2)
    qb = q_ref[...]
    q4 = jnp.concatenate([qb[:, h * HEAD:(h + 1) * HEAD] for h in range(grp)], axis=0)
    n_kv = lax.shift_right_logical((qi + 1) * tq + tk - 1, tk.bit_length() - 1)

    def step(j, carry):
        m, l, acc = carry
        off = pl.multiple_of(j * tk, tk)
        s = _dot(q4, kt_ref[:, pl.ds(off, tk)])
        s = (s.reshape(grp, tq, tk) + bias_ref[:, pl.ds(off, tk)].astype(F32)[None]).reshape(grp * tq, tk)
        m_new = jnp.maximum(m, jnp.max(s, axis=1, keepdims=True))
        alpha = jnp.exp(m - m_new)
        p = jnp.exp(s - m_new)
        l = alpha * l + jnp.sum(p, axis=1, keepdims=True)
        acc = alpha * acc + _dot(p.astype(BF16), v_ref[pl.ds(off, tk), :])
        return m_new, l, acc

    init = (jnp.full((grp * tq, 1), -jnp.inf, F32), jnp.zeros((grp * tq, 1), F32), jnp.zeros((grp * tq, HEAD), F32))
    _, l, acc = lax.fori_loop(0, n_kv, step, init)
    o = acc / l
    o_ref[...] = jnp.concatenate([o[h * tq:(h + 1) * tq] for h in range(grp)], axis=1).astype(o_ref.dtype)


def _dsa_attn_prompt(q, kt, v, bias, *, nb, seq):
    tq = min(seq, 256)
    tk = min(seq, SELECT_WIDTH_STEP)
    nq = seq // tq
    grp = H_B // KV_B
    return pl.pallas_call(
        functools.partial(_dsa_attn_kernel, tq=tq, tk=tk),
        grid=(nb, KV_B, nq),
        in_specs=[
            pl.BlockSpec((tq, grp * HEAD), lambda b, g, i: (b * nq + i, g)),
            pl.BlockSpec((None, HEAD, seq), lambda b, g, i: (b, g, 0)),
            pl.BlockSpec((None, seq, HEAD), lambda b, g, i: (g, b, 0)),
            pl.BlockSpec((None, tq, seq), lambda b, g, i: (b, i, 0)),
        ],
        out_specs=pl.BlockSpec((tq, grp * HEAD), lambda b, g, i: (b * nq + i, g)),
        out_shape=jax.ShapeDtypeStruct(q.shape, BF16),
        compiler_params=_params(("arbitrary", "arbitrary", "arbitrary")),
        name="dsa_attn_prompt",
    )(q, kt, v, bias)


def _dsa_attn_dec_kernel(pt_ref, q_ref, bias_ref, *refs, g, s_new, n_steps, n_past):
    kt_refs, vt_refs = refs[:g], refs[g:2 * g]
    ktn_ref, vtn_ref, o_ref, qbd_ref, m_ref, l_ref, acc_ref = refs[2 * g:]
    step = pl.program_id(1)
    nr = H_B * s_new
    grp = H_B // KV_B

    @pl.when(step == 0)
    def _():
        q = q_ref[...]
        rows = jnp.concatenate([q[:, h * HEAD:(h + 1) * HEAD] for h in range(H_B)], axis=0)
        wide = jnp.concatenate([rows] * KV_B, axis=1)
        r = _div_pow2(lax.broadcasted_iota(I32, wide.shape, 0), grp * s_new)
        c = _div_pow2(lax.broadcasted_iota(I32, wide.shape, 1), HEAD)
        qbd_ref[...] = jnp.where(r == c, wide, 0.0).astype(BF16)
        m_ref[...] = jnp.full(m_ref.shape, -jnp.inf, F32)
        l_ref[...] = jnp.zeros(l_ref.shape, F32)
        acc_ref[...] = jnp.zeros(acc_ref.shape, F32)

    def consume(kt, vt, bias):
        keys = kt.shape[1]
        s = _dot(qbd_ref[...], kt.astype(BF16))
        s = (s.reshape(H_B, s_new, keys) + bias[None]).reshape(nr, keys)
        m = m_ref[...]
        m_new = jnp.maximum(m, jnp.max(s, axis=1, keepdims=True))
        alpha = jnp.exp(m - m_new)
        p = jnp.exp(s - m_new)
        l_ref[...] = alpha * l_ref[...] + jnp.sum(p, axis=1, keepdims=True)
        m_ref[...] = m_new
        acc_ref[...] = alpha * acc_ref[...] + _dot_nt(p.astype(BF16), vt.astype(BF16))

    @pl.when(step < n_steps - 1)
    def _():
        off = pl.multiple_of(step * (g * LANES), g * LANES)
        consume(jnp.concatenate([r[...] for r in kt_refs], axis=1),
                jnp.concatenate([r[...] for r in vt_refs], axis=1), bias_ref[:, pl.ds(off, g * LANES)])

    @pl.when(step == n_steps - 1)
    def _():
        consume(ktn_ref[...], vtn_ref[...], bias_ref[:, n_past:n_past + LANES])
        o = acc_ref[...] / l_ref[...]
        outs = []
        for h in range(H_B):
            kv = h // grp
            outs.append(o[h * s_new:(h + 1) * s_new, kv * HEAD:(kv + 1) * HEAD])
        o_ref[...] = jnp.concatenate(outs, axis=1)


def _dsa_attn_decode(q, bias, kt_pool, vt_pool, page_table, kt_new, vt_new, *, s_new):
    nb, n_pages = page_table.shape
    g = min(PAGES_PER_STEP, n_pages)
    n_steps = n_pages // g + 1
    n_past = n_pages * LANES
    nr = H_B * s_new
    kvw = KV_B * HEAD

    def page_map(k):
        return lambda b, i, pt: (pt[b, jnp.minimum(i, n_steps - 2) * g + k], 0, 0)

    page = lambda k: pl.BlockSpec((None, kvw, LANES), page_map(k))
    new = pl.BlockSpec((None, kvw, LANES), lambda b, i, pt: (b, 0, 0))
    grid_spec = pltpu.PrefetchScalarGridSpec(
        num_scalar_prefetch=1,
        grid=(nb, n_steps),
        in_specs=[pl.BlockSpec((s_new, D_MODEL), lambda b, i, pt: (b, 0)),
                  pl.BlockSpec((None, s_new, n_past + LANES), lambda b, i, pt: (b, 0, 0))]
        + [page(k) for k in range(g)] + [page(k) for k in range(g)] + [new, new],
        out_specs=pl.BlockSpec((s_new, D_MODEL), lambda b, i, pt: (b, 0)),
        scratch_shapes=[
            pltpu.VMEM((nr, kvw), BF16),
            pltpu.VMEM((nr, 1), F32),
            pltpu.VMEM((nr, 1), F32),
            pltpu.VMEM((nr, kvw), F32),
        ],
    )
    return pl.pallas_call(
        functools.partial(_dsa_attn_dec_kernel, g=g, s_new=s_new, n_steps=n_steps, n_past=n_past),
        grid_spec=grid_spec,
        out_shape=jax.ShapeDtypeStruct(q.shape, F32),
        compiler_params=_params(("arbitrary", "arbitrary")),
        name="dsa_attn_decode",
    )(page_table, q, bias, *([kt_pool] * g), *([vt_pool] * g), kt_new, vt_new)


def _outproj_ln_kernel(o_ref, w_ref, x_ref, g_ref, b_ref, *rest, route):
    h = _dot(o_ref[...].astype(BF16), w_ref[...])
    y = _layer_norm(ALPHA * x_ref[...] + h, g_ref[...], b_ref[...])
    if not route:
        rest[0][...] = y
        return
    rwh_ref, rwl_ref, rb_ref, out_ref, cls_ref = rest
    out_ref[...] = y
    gi, i1, i2, _, _ = _route_choice(*_router_rows(y, rwh_ref[...], rwl_ref[...], rb_ref[...]))
    lo, hi = jnp.minimum(i1, i2), jnp.maximum(i1, i2)
    pair = jnp.where(lo == 0, 0, jnp.where(lo == 1, 3, 5)) + hi - lo - 1
    cls_ref[...] = gi * PAIRS_PER_GROUP + pair


def _outproj_ln(o, w, x, g, b, router=None, *, tm):
    t, d = x.shape
    row = lambda n: pl.BlockSpec((tm, n), lambda i: (i, 0))
    full = lambda shape: pl.BlockSpec(shape, lambda i: (0, 0))
    in_specs = [row(o.shape[1]), full(w.shape), row(d), full((1, d)), full((1, d))]
    out_specs, out_shape = row(d), jax.ShapeDtypeStruct((t, d), F32)
    if router is not None:
        in_specs += [full(r.shape) for r in router]
        out_specs = [out_specs, pl.BlockSpec((1, tm), lambda i: (0, i))]
        out_shape = [out_shape, jax.ShapeDtypeStruct((1, t), I32)]
    return pl.pallas_call(
        functools.partial(_outproj_ln_kernel, route=router is not None),
        grid=(t // tm,),
        in_specs=in_specs,
        out_specs=out_specs,
        out_shape=out_shape,
        compiler_params=_params(("arbitrary",)),
        name="outproj_ln",
    )(o, w, x, g, b, *(router or ()))


def _route_choice(sel, aff):
    gs = []
    for g in range(N_GROUPS):
        a, b, c, d = sel[4 * g:4 * g + 4]
        hi1, lo1, hi2, lo2 = jnp.maximum(a, b), jnp.minimum(a, b), jnp.maximum(c, d), jnp.minimum(c, d)
        gs.append(jnp.maximum(hi1, hi2) + jnp.maximum(jnp.minimum(hi1, hi2), jnp.maximum(lo1, lo2)))
    best, gi = gs[0], jnp.zeros(gs[0].shape, I32)
    for g in range(1, N_GROUPS):
        better = gs[g] > best
        best = jnp.where(better, gs[g], best)
        gi = jnp.where(better, g, gi)

    def pick(rows):
        out = []
        for j in range(E_PER_GROUP):
            v = rows[j]
            for g in range(1, N_GROUPS):
                v = jnp.where(gi == g, rows[4 * g + j], v)
            out.append(v)
        return out

    sv, av = pick(sel), pick(aff)

    def argmax_first(vals):
        bv, bi = vals[0], jnp.zeros(vals[0].shape, I32)
        for j in range(1, E_PER_GROUP):
            better = vals[j] > bv
            bv = jnp.where(better, vals[j], bv)
            bi = jnp.where(better, j, bi)
        return bi

    i1 = argmax_first(sv)
    i2 = argmax_first([jnp.where(i1 == j, -jnp.inf, sv[j]) for j in range(E_PER_GROUP)])
    g1, g2 = av[0], av[0]
    for j in range(1, E_PER_GROUP):
        g1 = jnp.where(i1 == j, av[j], g1)
        g2 = jnp.where(i2 == j, av[j], g2)
    tot = g1 + g2
    return gi, i1, i2, g1 / tot, g2 / tot


def _route(sel, aff):
    gi, i1, i2, g1, g2 = _route_choice(sel, aff)
    comb = []
    for e in range(N_EXPERTS):
        g, j = divmod(e, E_PER_GROUP)
        in_g = gi == g
        comb.append(jnp.where(in_g & (i1 == j), g1, jnp.where(in_g & (i2 == j), g2, 0.0)))
    return comb


def _router_rows(x, rwt_hi, rwt_lo, rb):
    xh = x.astype(BF16)
    xl = (x - xh.astype(F32)).astype(BF16)
    logits = _dot_nt(rwt_hi, xh) + (_dot_nt(rwt_hi, xl) + _dot_nt(rwt_lo, xh))
    aff = _sigmoid(logits)
    sel = aff + rb
    return [sel[e:e + 1] for e in range(N_EXPERTS)], [aff[e:e + 1] for e in range(N_EXPERTS)]


def _moe_dense_kernel(x_ref, rwh_ref, rwl_ref, rb_ref, wg_ref, wu_ref, wd_ref, g_ref, b_ref, out_ref,
                      xb_ref, comb_ref, acc_ref, *, tm):
    e = pl.program_id(1)

    @pl.when(e == 0)
    def _():
        x = x_ref[...]
        xb_ref[...] = x.astype(BF16)
        sel, aff = _router_rows(x, rwh_ref[...], rwl_ref[...], rb_ref[...])
        comb = jnp.concatenate(_route(sel, aff) + [jnp.zeros((LANES - N_EXPERTS, tm), F32)], axis=0)
        comb_ref[...] = comb.T
        acc_ref[...] = jnp.zeros(acc_ref.shape, F32)

    xb = xb_ref[...]
    hg = _dot(xb, wg_ref[...])
    hu = _dot(xb, wu_ref[...])
    hdn = (hg * _sigmoid(hg)) * hu
    y = _dot(hdn.astype(BF16), wd_ref[...])
    lane = lax.broadcasted_iota(I32, (tm, LANES), 1)
    col = jnp.sum(jnp.where(lane == e, comb_ref[...], 0.0), axis=1, keepdims=True)
    acc_ref[...] += col * y

    @pl.when(e == N_EXPERTS - 1)
    def _():
        out_ref[...] = _layer_norm(ALPHA * x_ref[...] + acc_ref[...], g_ref[...], b_ref[...])


def _moe_ln(x, rwt_hi, rwt_lo, rb, wg, wu, wd, g, b, *, tm):
    t, d = x.shape
    f = wg.shape[2]
    full = lambda shape: pl.BlockSpec(shape, lambda i, e: (0,) * len(shape))
    return pl.pallas_call(
        functools.partial(_moe_dense_kernel, tm=tm),
        grid=(t // tm, N_EXPERTS),
        in_specs=[
            pl.BlockSpec((tm, d), lambda i, e: (i, 0)),
            full((N_EXPERTS, d)), full((N_EXPERTS, d)), full((N_EXPERTS, 1)),
            pl.BlockSpec((None, d, f), lambda i, e: (e, 0, 0)),
            pl.BlockSpec((None, d, f), lambda i, e: (e, 0, 0)),
            pl.BlockSpec((None, f, d), lambda i, e: (e, 0, 0)),
            full((1, d)), full((1, d)),
        ],
        out_specs=pl.BlockSpec((tm, d), lambda i, e: (i, 0)),
        out_shape=jax.ShapeDtypeStruct((t, d), F32),
        scratch_shapes=[pltpu.VMEM((tm, d), BF16), pltpu.VMEM((tm, LANES), F32), pltpu.VMEM((tm, d), F32)],
        compiler_params=_params(("arbitrary", "arbitrary")),
        name="moe_ln",
    )(x, rwt_hi, rwt_lo, rb, wg, wu, wd, g, b)


PAIRS_PER_GROUP = 6
N_CLASSES = N_GROUPS * PAIRS_PER_GROUP
MOE_TILE = 256


def _moe_plan_kernel(cls_ref, slot_ref, tcls_ref, nused_ref, *, tm):
    cls = cls_ref[...]
    r = cls.shape[0]
    upper = jnp.where(lax.broadcasted_iota(I32, (LANES, LANES), 0) < lax.broadcasted_iota(I32, (LANES, LANES), 1),
                      1.0, 0.0).astype(BF16)
    lower = jnp.where(lax.broadcasted_iota(I32, (r, r), 1) < lax.broadcasted_iota(I32, (r, r), 0),
                      1.0, 0.0).astype(BF16)
    tile_start = lax.broadcasted_iota(I32, (1, LANES), 1).astype(F32) * tm
    base = jnp.zeros((1, 1), F32)
    slot = jnp.zeros((r, LANES), F32)
    tcls = jnp.zeros((1, LANES), F32)
    last = jnp.zeros((1, 1), F32)
    for c in range(N_CLASSES):
        oh = cls == c
        ohf = jnp.where(oh, 1.0, 0.0)
        before_in_row = _dot(ohf.astype(BF16), upper)
        row_total = jnp.sum(ohf, axis=1, keepdims=True)
        before_rows = _dot(lower, jnp.broadcast_to(row_total, (r, LANES)).astype(BF16))
        count = jnp.sum(row_total, axis=0, keepdims=True)
        padded = jnp.ceil(count / tm) * tm
        slot = slot + jnp.where(oh, base + before_in_row + before_rows, 0.0)
        tcls = jnp.where((tile_start >= base) & (tile_start < base + padded), float(c), tcls)
        last = jnp.where(padded > 0, float(c), last)
        base = base + padded
    slot_ref[...] = slot.astype(I32)
    tcls_ref[...] = jnp.where(tile_start >= base, last, tcls).astype(I32)
    nused_ref[...] = jnp.broadcast_to(base / tm, (1, LANES)).astype(I32)


def _moe_plan(cls, *, tm):
    t = cls.shape[1]
    r = t // LANES
    full = lambda shape: pl.BlockSpec(shape, lambda: (0, 0))
    slot, tcls, nused = pl.pallas_call(
        functools.partial(_moe_plan_kernel, tm=tm),
        in_specs=[full((r, LANES))],
        out_specs=[full((r, LANES)), full((1, LANES)), full((1, LANES))],
        out_shape=[jax.ShapeDtypeStruct((r, LANES), I32), jax.ShapeDtypeStruct((1, LANES), I32),
                   jax.ShapeDtypeStruct((1, LANES), I32)],
        name="moe_plan",
    )(cls.reshape(r, LANES))
    return slot.reshape(t // tm, 1, tm), tcls[0], nused[0, :1]


def _wait_rows(src, dst, sem):
    pltpu.make_async_copy(src, dst, sem).wait()


def _moe_scatter_kernel(x_ref, slot_ref, xs_in, xs_out, stage, sems, *, tm, n_steps):
    i = pl.program_id(0)
    b = i & 1
    tile_rows = xs_out.at[pl.ds(0, tm)]

    @pl.when(i >= 2)
    def _():
        _wait_rows(stage.at[b], tile_rows, sems.at[b])

    stage[b] = x_ref[...]

    def issue(r, carry):
        pltpu.make_async_copy(stage.at[b, pl.ds(r, 1)], xs_out.at[pl.ds(slot_ref[0, r], 1)], sems.at[b]).start()
        return carry

    lax.fori_loop(0, tm, issue, 0, unroll=8)

    @pl.when(i == n_steps - 1)
    def _():
        _wait_rows(stage.at[b], tile_rows, sems.at[b])
        if n_steps > 1:
            _wait_rows(stage.at[1 - b], tile_rows, sems.at[1 - b])


def _moe_scatter(x, slot, *, tm, n_rows):
    t, d = x.shape
    n_steps = t // tm
    return pl.pallas_call(
        functools.partial(_moe_scatter_kernel, tm=tm, n_steps=n_steps),
        grid=(n_steps,),
        in_specs=[
            pl.BlockSpec((tm, d), lambda i: (i, 0)),
            pl.BlockSpec((None, 1, tm), lambda i: (i, 0, 0), memory_space=pltpu.SMEM),
            pl.BlockSpec(memory_space=pl.ANY),
        ],
        out_specs=pl.BlockSpec(memory_space=pl.ANY),
        out_shape=jax.ShapeDtypeStruct((n_rows, d), F32),
        scratch_shapes=[pltpu.VMEM((2, tm, d), F32), pltpu.SemaphoreType.DMA((2,))],
        input_output_aliases={2: 0},
        compiler_params=_params(("arbitrary",)),
        name="moe_scatter",
    )(x, slot, jnp.zeros((n_rows, d), F32))


def _class_experts(c):
    g = c // PAIRS_PER_GROUP
    pair = c - g * PAIRS_PER_GROUP
    lo = (pair >= 3).astype(I32) + (pair >= 5).astype(I32)
    hi = pair - (3 * (lo >= 1).astype(I32) + 2 * (lo >= 2).astype(I32)) + lo + 1
    return g * E_PER_GROUP + lo, g * E_PER_GROUP + hi


def _moe_sorted_kernel(tcls_ref, nused_ref, xs_ref, rwh_ref, rwl_ref, wg1, wu1, wd1, wg2, wu2, wd2,
                       g_ref, b_ref, out_ref, *, tm):
    i = pl.program_id(0)

    @pl.when(i >= nused_ref[0])
    def _():
        out_ref[...] = jnp.zeros(out_ref.shape, F32)

    @pl.when(i < nused_ref[0])
    def _():
        e_lo, e_hi = _class_experts(tcls_ref[i])
        x = xs_ref[...]
        xb = x.astype(BF16)
        xl = (x - xb.astype(F32)).astype(BF16)
        logits = _dot(xb, rwh_ref[...]) + (_dot(xl, rwh_ref[...]) + _dot(xb, rwl_ref[...]))
        aff = _sigmoid(logits)
        lane = lax.broadcasted_iota(I32, (tm, LANES), 1)
        a_lo = jnp.sum(jnp.where(lane == e_lo, aff, 0.0), axis=1, keepdims=True)
        a_hi = jnp.sum(jnp.where(lane == e_hi, aff, 0.0), axis=1, keepdims=True)
        tot = a_lo + a_hi

        def expert(wg, wu, wd):
            hg = _dot(xb, wg[...])
            return _dot(((hg * _sigmoid(hg)) * _dot(xb, wu[...])).astype(BF16), wd[...])

        acc = (a_lo / tot) * expert(wg1, wu1, wd1)
        acc = acc + (a_hi / tot) * expert(wg2, wu2, wd2)
        out_ref[...] = _layer_norm(ALPHA * x + acc, g_ref[...], b_ref[...])


def _moe_sorted(xs, tcls, nused, rw_hi, rw_lo, wg, wu, wd, g, b, *, tm):
    n_rows, d = xs.shape
    f = wg.shape[2]
    full = lambda shape: pl.BlockSpec(shape, lambda i, tc, nu: (0,) * len(shape))

    def w_spec(shape, which):
        return pl.BlockSpec((None,) + shape, lambda i, tc, nu: (_class_experts(tc[i])[which], 0, 0))

    grid_spec = pltpu.PrefetchScalarGridSpec(
        num_scalar_prefetch=2,
        grid=(n_rows // tm,),
        in_specs=[pl.BlockSpec((tm, d), lambda i, tc, nu: (i, 0)), full(rw_hi.shape), full(rw_lo.shape),
                  w_spec((d, f), 0), w_spec((d, f), 0), w_spec((f, d), 0),
                  w_spec((d, f), 1), w_spec((d, f), 1), w_spec((f, d), 1),
                  full((1, d)), full((1, d))],
        out_specs=pl.BlockSpec((tm, d), lambda i, tc, nu: (i, 0)),
    )
    return pl.pallas_call(
        functools.partial(_moe_sorted_kernel, tm=tm),
        grid_spec=grid_spec,
        out_shape=jax.ShapeDtypeStruct((n_rows, d), F32),
        compiler_params=_params(("arbitrary",)),
        name="moe_sorted",
    )(tcls, nused, xs, rw_hi, rw_lo, wg, wu, wd, wg, wu, wd, g, b)


def _ple_math(x, p_ref, wg_ref, bg_ref, wp_ref):
    gate = _sigmoid(_dot(x.astype(BF16), wg_ref[...]) + bg_ref[...])
    return x + gate * _dot(p_ref[...].astype(BF16), wp_ref[...])


def _ple_gather_kernel(slot_ref, slot_next_ref, ys_ref, p_ref, wg_ref, bg_ref, wp_ref, out_ref, buf, sems,
                       *, tm, n_steps):
    i = pl.program_id(0)
    b = i & 1

    def issue(slots, bb):
        def body(r, carry):
            pltpu.make_async_copy(ys_ref.at[pl.ds(slots[0, r], 1)], buf.at[bb, pl.ds(r, 1)], sems.at[bb]).start()
            return carry
        lax.fori_loop(0, tm, body, 0, unroll=8)

    @pl.when(i == 0)
    def _():
        issue(slot_ref, 0)

    @pl.when(i + 1 < n_steps)
    def _():
        issue(slot_next_ref, 1 - b)

    _wait_rows(ys_ref.at[pl.ds(0, tm)], buf.at[b], sems.at[b])
    out_ref[...] = _ple_math(buf[b], p_ref, wg_ref, bg_ref, wp_ref)


def _ple_gather(ys, slot, p, wg, bg, wp, *, tm):
    d = ys.shape[1]
    t, pd = p.shape
    n_steps = t // tm
    full = lambda shape: pl.BlockSpec(shape, lambda i: (0, 0))
    return pl.pallas_call(
        functools.partial(_ple_gather_kernel, tm=tm, n_steps=n_steps),
        grid=(n_steps,),
        in_specs=[
            pl.BlockSpec((None, 1, tm), lambda i: (i, 0, 0), memory_space=pltpu.SMEM),
            pl.BlockSpec((None, 1, tm), lambda i: (jnp.minimum(i + 1, n_steps - 1), 0, 0), memory_space=pltpu.SMEM),
            pl.BlockSpec(memory_space=pl.ANY),
            pl.BlockSpec((tm, pd), lambda i: (i, 0)),
            full((d, d)), full((1, d)), full((pd, d)),
        ],
        out_specs=pl.BlockSpec((tm, d), lambda i: (i, 0)),
        out_shape=jax.ShapeDtypeStruct((t, d), F32),
        scratch_shapes=[pltpu.VMEM((2, tm, d), F32), pltpu.SemaphoreType.DMA((2,))],
        compiler_params=_params(("arbitrary",)),
        name="ple_gather",
    )(slot, slot, ys, p, wg, bg, wp)


def _ple_kernel(x_ref, p_ref, wg_ref, bg_ref, wp_ref, out_ref):
    out_ref[...] = _ple_math(x_ref[...], p_ref, wg_ref, bg_ref, wp_ref)


def _ple(x, p, wg, bg, wp, *, tm):
    t, d = x.shape
    pd = p.shape[1]
    full = lambda shape: pl.BlockSpec(shape, lambda i: (0, 0))
    return pl.pallas_call(
        _ple_kernel,
        grid=(t // tm,),
        in_specs=[pl.BlockSpec((tm, d), lambda i: (i, 0)), pl.BlockSpec((tm, pd), lambda i: (i, 0)),
                  full((d, d)), full((1, d)), full((pd, d))],
        out_specs=pl.BlockSpec((tm, d), lambda i: (i, 0)),
        out_shape=jax.ShapeDtypeStruct((t, d), F32),
        compiler_params=_params(("arbitrary",)),
        name="ple",
    )(x, p, wg, bg, wp)


def _pad_page_t(x, nb, s_new):
    xt = x.reshape(nb, s_new, -1).transpose(0, 2, 1)
    return jnp.pad(xt, ((0, 0), (0, 0), (0, LANES - s_new)))


def _layer_tail(o, w_out, x, p, w, layer, *, sort_tokens, tm):
    t = x.shape[0]
    ln_g, ln_b = w["ln_g"][layer], w["ln_b"][layer]
    wg, wu, wd = w["moe_w_gate"][layer], w["moe_w_up"][layer], w["moe_w_down"][layer]
    ple = (w["ple_gate_w"][layer], w["ple_gate_b"][layer], w["ple_proj"][layer])
    router = (w["rwt_hi"], w["rwt_lo"], w["rb"])
    if not sort_tokens:
        x = _outproj_ln(o, w_out, x, ln_g[0], ln_b[0], tm=tm)
        x = _moe_ln(x, *router, wg, wu, wd, ln_g[1], ln_b[1], tm=512 if t % 512 == 0 else tm)
        return _ple(x, p, *ple, tm=tm)
    n_rows = t + N_CLASSES * MOE_TILE
    assert n_rows // MOE_TILE <= LANES, "the plan kernel lists at most 128 tiles"
    x, cls = _outproj_ln(o, w_out, x, ln_g[0], ln_b[0], router, tm=tm)
    slot, tile_cls, n_used = _moe_plan(cls, tm=MOE_TILE)
    xs = _moe_scatter(x, slot, tm=MOE_TILE, n_rows=n_rows)
    ys = _moe_sorted(xs, tile_cls, n_used, w["rw_hi"], w["rw_lo"], wg, wu, wd, ln_g[1], ln_b[1], tm=MOE_TILE)
    return _ple_gather(ys, slot, p, *ple, tm=MOE_TILE)


def _trunk(x3, p4, pos_off, past, page_table, w):
    nb, seq, d = x3.shape
    t = nb * seq
    prompt = past is None
    x = x3.reshape(t, d)
    p = p4.reshape(DEPTH, t, -1)
    tm = min(256, t)
    if prompt:
        tabs = _rope_tables(seq, seq, pos_off)
    else:
        tabs = _rope_tables(t, seq, pos_off)
    outs = {}

    lam_init = 0.8 - 0.6 * math.exp(-0.3 * 0)
    w_in = w["a_w_in"]
    nq = 2 * H_A * HD_A
    wq, wk, wv = w_in[:, :nq], w_in[:, nq:2 * nq], w_in[:, 2 * nq:]
    lams = [w[n] for n in ("a_lam_q1", "a_lam_k1", "a_lam_q2", "a_lam_k2")]
    if prompt:
        segs = [
            dict(w=wq, tr=False, rot=8, scale=HD_A ** -0.5, outs=[(BF16, False)]),
            dict(w=wk.T, tr=True, rot=8, scale=1.0, outs=[(F32, False), (BF16, False)]),
            dict(w=wv, tr=False, rot=0, scale=1.0, outs=[(F32, False), (BF16, False)]),
        ]
        q, kt, kt_b, v, v_b = _inproj(x, tabs, segs, seq=seq, tm=tm)
        o = _diff_attn_prompt(q, kt_b, v_b, lams, w["a_subln"], nb=nb, seq=seq, lam_init=lam_init)
        outs["a_k"] = kt.reshape(nb, 2 * H_A, HD_A, seq).transpose(0, 3, 1, 2)
    else:
        segs = [
            dict(w=wq, tr=False, rot=8, scale=HD_A ** -0.5, outs=[(F32, False)]),
            dict(w=wk, tr=False, rot=8, scale=1.0, outs=[(F32, False)]),
            dict(w=wv, tr=False, rot=0, scale=1.0, outs=[(F32, False)]),
        ]
        q, k, v = _inproj(x, tabs, segs, seq=seq, tm=tm)
        v_new = jnp.pad(v.reshape(nb, seq, H_A, 2 * HD_A), ((0, 0), (0, LANES - seq), (0, 0), (0, 0)))
        o = _diff_attn_decode(q, past["a_kt"], past["a_v"], page_table, _pad_page_t(k, nb, seq),
                              v_new.reshape(nb, LANES * H_A, 2 * HD_A), lams, w["a_subln"],
                              s_new=seq, lam_init=lam_init)
        outs["a_k"] = k.reshape(nb, seq, 2 * H_A, HD_A)
    outs["a_v"] = v.reshape(nb, seq, H_A, 2 * HD_A)
    x = _layer_tail(o, w["a_w_out"], x, p[0], w, 0, sort_tokens=prompt, tm=tm)

    w_in = w["b_w_in"]
    sizes = (H_B * HD_B, KV_B * HD_B, KV_B * HD_B, H_I * D_I, D_I, H_I)
    offs = [sum(sizes[:m]) for m in range(len(sizes) + 1)]
    wq, wk, wv, wqi, wki, wwi = [w_in[:, offs[m]:offs[m + 1]] for m in range(len(sizes))]
    wwi = jnp.pad(wwi, ((0, 0), (0, LANES - H_I)))
    if prompt:
        n_sel = min(TOPK_MAX, seq // 4)
        segs = [
            dict(w=wq, tr=False, rot=8, scale=HD_B ** -0.5, outs=[(BF16, False)]),
            dict(w=wk.T, tr=True, rot=8, scale=1.0, outs=[(F32, False), (BF16, False)]),
            dict(w=wv.T, tr=True, rot=0, scale=1.0, outs=[(F32, False)]),
            dict(w=wv, tr=False, rot=0, scale=1.0, outs=[(BF16, True)]),
            dict(w=wqi, tr=False, rot=16, scale=1.0, outs=[(BF16, False)]),
            dict(w=wki.T, tr=True, rot=16, scale=1.0, outs=[(F32, False), (BF16, False)]),
            dict(w=wwi, tr=False, rot=0, scale=IDX_SCALE, outs=[(F32, False)]),
        ]
        q, kt, kt_b, vt, v_b, qi, kit, kit_b, wi = _inproj(x, tabs, segs, seq=seq, tm=tm)
        bias = _dsa_select_prompt(qi, wi, kit_b, nb=nb, seq=seq, n_sel=n_sel)
        o = _dsa_attn_prompt(q, kt_b, v_b, bias, nb=nb, seq=seq)
        tr4 = lambda a, h: a.reshape(nb, h, HEAD, seq).transpose(0, 3, 1, 2)
        outs["b_k"] = tr4(kt, KV_B)
        outs["b_v"] = tr4(vt, KV_B)
        outs["b_ki"] = kit.transpose(0, 2, 1)
    else:
        n_past = page_table.shape[1] * LANES
        n_sel = min(TOPK_MAX, (n_past + seq) // 4)
        segs = [
            dict(w=wq, tr=False, rot=8, scale=HD_B ** -0.5, outs=[(F32, False)]),
            dict(w=wk, tr=False, rot=8, scale=1.0, outs=[(F32, False)]),
            dict(w=wv, tr=False, rot=0, scale=1.0, outs=[(F32, False)]),
            dict(w=wqi, tr=False, rot=16, scale=1.0, outs=[(F32, False)]),
            dict(w=jnp.pad(wki, ((0, 0), (0, LANES - D_I))), tr=False, rot=16, scale=1.0, outs=[(F32, False)]),
            dict(w=wwi, tr=False, rot=0, scale=IDX_SCALE, outs=[(F32, False)]),
        ]
        q, k, v, qi, ki, wi = _inproj(x, tabs, segs, seq=seq, tm=tm)
        ki = ki[:, :D_I]
        bias = _dsa_select_decode(qi, wi, past["b_kit"], page_table, _pad_page_t(ki, nb, seq), s_new=seq, n_sel=n_sel)
        o = _dsa_attn_decode(q, bias, past["b_kt"], past["b_vt"], page_table, _pad_page_t(k, nb, seq),
                             _pad_page_t(v, nb, seq), s_new=seq)
        outs["b_k"] = k.reshape(nb, seq, KV_B, HD_B)
        outs["b_v"] = v.reshape(nb, seq, KV_B, HD_B)
        outs["b_ki"] = ki.reshape(nb, seq, D_I)
    x = _layer_tail(o, w["b_w_out"], x, p[1], w, 1, sort_tokens=prompt, tm=tm)
    return x.reshape(nb, seq, d), outs


def _prep_weights(a_w_in, a_w_out, a_lam_q1, a_lam_k1, a_lam_q2, a_lam_k2, a_subln, b_w_in, b_w_out, ln_g, ln_b,
                  router_w, router_b, moe_w_gate, moe_w_up, moe_w_down, ple_proj, ple_gate_w, ple_gate_b):
    rwt = router_w.T
    rwt_hi = rwt.astype(BF16)
    rw = jnp.pad(router_w, ((0, 0), (0, LANES - N_EXPERTS)))
    rw_hi = rw.astype(BF16)
    return {
        "a_w_in": a_w_in[0].astype(BF16), "a_w_out": a_w_out[0].astype(BF16),
        "a_lam_q1": a_lam_q1, "a_lam_k1": a_lam_k1, "a_lam_q2": a_lam_q2, "a_lam_k2": a_lam_k2,
        "a_subln": a_subln,
        "b_w_in": b_w_in[0].astype(BF16), "b_w_out": b_w_out[0].astype(BF16),
        "ln_g": ln_g[:, :, None, :], "ln_b": ln_b[:, :, None, :],
        "rwt_hi": rwt_hi, "rwt_lo": (rwt - rwt_hi.astype(F32)).astype(BF16), "rb": router_b.reshape(N_EXPERTS, 1),
        "rw_hi": rw_hi, "rw_lo": (rw - rw_hi.astype(F32)).astype(BF16),
        "moe_w_gate": moe_w_gate.astype(BF16), "moe_w_up": moe_w_up.astype(BF16),
        "moe_w_down": moe_w_down.astype(BF16),
        "ple_proj": ple_proj.astype(BF16), "ple_gate_w": ple_gate_w.astype(BF16),
        "ple_gate_b": ple_gate_b[:, None, :],
    }


def kernel(x_prompt, x_sample, cache_a_k, cache_a_v, cache_b_k, cache_b_v, cache_b_kidx, page_table, p_prompt,
           p_sample, a_w_in, a_w_out, a_lam_q1, a_lam_k1, a_lam_q2, a_lam_k2, a_subln, b_w_in, b_w_out, ln_g, ln_b,
           router_w, router_b, moe_w_gate, moe_w_up, moe_w_down, ple_proj, ple_gate_w, ple_gate_b):
    w = _prep_weights(a_w_in, a_w_out, a_lam_q1, a_lam_k1, a_lam_q2, a_lam_k2, a_subln, b_w_in, b_w_out, ln_g, ln_b,
                      router_w, router_b, moe_w_gate, moe_w_up, moe_w_down, ple_proj, ple_gate_w, ple_gate_b)
    n_pool, page = cache_a_k.shape[1], cache_a_k.shape[2]
    past_len = page_table.shape[1] * page
    past = {
        "a_kt": cache_a_k[0].transpose(0, 2, 3, 1).reshape(n_pool, 2 * H_A * HD_A, page),
        "a_v": cache_a_v[0].reshape(n_pool, page * H_A, 2 * HD_A),
        "b_kt": cache_b_k[0].transpose(0, 2, 3, 1).reshape(n_pool, KV_B * HD_B, page),
        "b_vt": cache_b_v[0].transpose(0, 2, 3, 1).reshape(n_pool, KV_B * HD_B, page),
        "b_kit": cache_b_kidx[0].transpose(0, 2, 1),
    }
    y_p, op = _trunk(x_prompt, p_prompt, 0, None, None, w)
    y_s, os_ = _trunk(x_sample, p_sample, past_len, past, page_table, w)
    lead = lambda a: a[None]
    return (y_p, y_s,
            lead(op["a_k"]), lead(op["a_v"]), lead(op["b_k"]), lead(op["b_v"]), lead(op["b_ki"]),
            lead(os_["a_k"]), lead(os_["a_v"]), lead(os_["b_k"]), lead(os_["b_v"]), lead(os_["b_ki"]))
```

```python
import functools
import math

import jax
import jax.numpy as jnp
from jax import lax
from jax.experimental import pallas as pl
from jax.experimental.pallas import tpu as pltpu

F32 = jnp.float32
BF16 = jnp.bfloat16
I32 = jnp.int32

D_MODEL = 1024
DEPTH = 2
H_A = 8
HD_A = 64
H_B = 16
KV_B = 4
HD_B = 64
H_I = 8
D_I = 64
TOPK_MAX = 256
ROPE_THETA = 500000.0
N_EXPERTS = 16
N_GROUPS = 4
E_PER_GROUP = 4
D_FF = 512
ALPHA = (2 * DEPTH) ** 0.25
LN_EPS = 1e-5
RMS_EPS = 1e-5
IDX_SCALE = (H_I ** -0.5) * (D_I ** -0.5)

LANES = 128
HEAD = 64
VMEM_LIMIT = 56 * 1024 * 1024
INT_MIN = -2 ** 31
NEG_BIG = -1e30


def _params(sem):
    return pltpu.CompilerParams(dimension_semantics=sem, vmem_limit_bytes=VMEM_LIMIT)


def _dot(a, b):
    return jnp.dot(a, b, preferred_element_type=F32)


def _dot_nt(a, b):
    return lax.dot_general(a, b, (((1,), (1,)), ((), ())), preferred_element_type=F32)


def _div_pow2(x, n):
    assert n & (n - 1) == 0, n
    return x >> (n.bit_length() - 1)


def _layer_norm(y, g, b):
    mu = jnp.mean(y, axis=-1, keepdims=True)
    var = jnp.mean(jnp.square(y - mu), axis=-1, keepdims=True)
    return (y - mu) * lax.rsqrt(var + LN_EPS) * g + b


def _sigmoid(x):
    return 1.0 / (1.0 + jnp.exp(-x))


def _rope_tables_kernel(inv8l, inv16l, inv8c, inv16c, tok8, tok16, tr8, tr16, *, seq, off, tp):
    base = pl.program_id(0) * tp
    row = base + lax.broadcasted_iota(I32, (tp, LANES), 0)
    lane = lax.broadcasted_iota(I32, (tp, LANES), 1)
    pos = (off + (row & (seq - 1))).astype(F32)
    d = lane & (HEAD - 1)
    for half, inv, out in ((8, inv8l, tok8), (16, inv16l, tok16)):
        ang = pos * inv[...]
        c = jnp.cos(ang)
        s = jnp.sin(ang)
        out[0] = jnp.where(d < 2 * half, c, 1.0)
        out[1] = jnp.where(d < half, -s, 0.0)
        out[2] = jnp.where((d >= half) & (d < 2 * half), s, 0.0)
    col = base + lax.broadcasted_iota(I32, (1, tp), 1)
    posr = (off + (col & (seq - 1))).astype(F32)
    for inv, out in ((inv8c, tr8), (inv16c, tr16)):
        ang = posr * inv[...]
        out[0] = jnp.cos(ang)
        out[1] = jnp.sin(ang)


def _rope_tables(n_pos, seq, off):
    tp = min(n_pos, 256)
    lane = jnp.arange(LANES) % HEAD
    invs = []
    for half in (8, 16):
        inv = ROPE_THETA ** (-jnp.arange(half, dtype=F32) / half)
        invs.append((jnp.where(lane < 2 * half, inv[lane % half], 0.0).reshape(1, LANES), inv.reshape(half, 1)))
    full = lambda shape: pl.BlockSpec(shape, lambda i: (0,) * len(shape))
    return pl.pallas_call(
        functools.partial(_rope_tables_kernel, seq=seq, off=off, tp=tp),
        grid=(n_pos // tp,),
        in_specs=[full((1, LANES)), full((1, LANES)), full((8, 1)), full((16, 1))],
        out_specs=[
            pl.BlockSpec((3, tp, LANES), lambda i: (0, i, 0)),
            pl.BlockSpec((3, tp, LANES), lambda i: (0, i, 0)),
            pl.BlockSpec((2, 8, tp), lambda i: (0, 0, i)),
            pl.BlockSpec((2, 16, tp), lambda i: (0, 0, i)),
        ],
        out_shape=[
            jax.ShapeDtypeStruct((3, n_pos, LANES), F32),
            jax.ShapeDtypeStruct((3, n_pos, LANES), F32),
            jax.ShapeDtypeStruct((2, 8, n_pos), F32),
            jax.ShapeDtypeStruct((2, 16, n_pos), F32),
        ],
        compiler_params=_params(("arbitrary",)),
        name="rope_tables",
    )(invs[0][0], invs[1][0], invs[0][1], invs[1][1])


def _rope_tok(y, tab, half):
    c, a, b = tab[0], tab[1], tab[2]
    outs = []
    for k in range(y.shape[1] // LANES):
        yc = y[:, k * LANES:(k + 1) * LANES]
        outs.append(yc * c + pltpu.roll(yc, LANES - half, 1) * a + pltpu.roll(yc, half, 1) * b)
    return outs[0] if len(outs) == 1 else jnp.concatenate(outs, axis=1)


def _rope_tr(yt, tr, half):
    cos, sin = tr[0], tr[1]
    parts = []
    for h in range(yt.shape[0] // HEAD):
        b = h * HEAD
        x1 = yt[b:b + half]
        x2 = yt[b + half:b + 2 * half]
        parts += [x1 * cos - x2 * sin, x2 * cos + x1 * sin, yt[b + 2 * half:b + HEAD]]
    return jnp.concatenate(parts, axis=0)


def _inproj_kernel(*refs, segs):
    x_ref, tok8, tok16, tr8, tr16 = refs[:5]
    w_refs = refs[5:5 + len(segs)]
    out_refs = list(refs[5 + len(segs):])
    xb = x_ref[...].astype(BF16)
    for seg, w_ref in zip(segs, w_refs):
        if seg["tr"]:
            y = _dot_nt(w_ref[...], xb)
            if seg["rot"]:
                y = _rope_tr(y, tr8 if seg["rot"] == 8 else tr16, seg["rot"])
        else:
            y = _dot(xb, w_ref[...])
            if seg["rot"]:
                y = _rope_tok(y, tok8 if seg["rot"] == 8 else tok16, seg["rot"])
        if seg["scale"] != 1.0:
            y = y * seg["scale"]
        for dt, split in seg["outs"]:
            o_ref = out_refs.pop(0)
            if split:
                for g in range(y.shape[1] // HEAD):
                    o_ref[g] = y[:, g * HEAD:(g + 1) * HEAD].astype(dt)
            else:
                o_ref[...] = y.astype(dt)


def _inproj(x, tabs, segs, *, seq, tm):
    t, k = x.shape
    nb = t // seq
    tps = max(seq // tm, 1)
    n_tab = tabs[0].shape[1] // tm
    in_specs = [
        pl.BlockSpec((tm, k), lambda i: (i, 0)),
        pl.BlockSpec((3, tm, LANES), lambda i: (0, i % n_tab, 0)),
        pl.BlockSpec((3, tm, LANES), lambda i: (0, i % n_tab, 0)),
        pl.BlockSpec((2, 8, tm), lambda i: (0, 0, i % n_tab)),
        pl.BlockSpec((2, 16, tm), lambda i: (0, 0, i % n_tab)),
    ]
    out_specs, out_shape = [], []
    for seg in segs:
        w = seg["w"]
        in_specs.append(pl.BlockSpec(w.shape, lambda i: (0, 0)))
        n = w.shape[0] if seg["tr"] else w.shape[1]
        for dt, split in seg["outs"]:
            if seg["tr"]:
                out_specs.append(pl.BlockSpec((None, n, tm), lambda i: (i // tps, 0, i % tps)))
                out_shape.append(jax.ShapeDtypeStruct((nb, n, seq), dt))
            elif split:
                out_specs.append(pl.BlockSpec((n // HEAD, tm, HEAD), lambda i: (0, i, 0)))
                out_shape.append(jax.ShapeDtypeStruct((n // HEAD, t, HEAD), dt))
            else:
                out_specs.append(pl.BlockSpec((tm, n), lambda i: (i, 0)))
                out_shape.append(jax.ShapeDtypeStruct((t, n), dt))
    kern_segs = tuple({k2: v for k2, v in seg.items() if k2 != "w"} for seg in segs)
    return pl.pallas_call(
        functools.partial(_inproj_kernel, segs=kern_segs),
        grid=(t // tm,),
        in_specs=in_specs,
        out_specs=out_specs,
        out_shape=out_shape,
        compiler_params=_params(("arbitrary",)),
        name="inproj",
    )(x, *tabs, *[seg["w"] for seg in segs])


def _lambda(lq1, lk1, lq2, lk2, lam_init):
    return (jnp.exp(jnp.sum(lq1[...] * lk1[...], axis=1, keepdims=True))
            - jnp.exp(jnp.sum(lq2[...] * lk2[...], axis=1, keepdims=True)) + lam_init)


def _sub_norm(o, sub, lam_init):
    return o * lax.rsqrt(jnp.mean(o * o, axis=-1, keepdims=True) + RMS_EPS) * sub * (1.0 - lam_init)


def _diff_attn_kernel(q_ref, kt_ref, v_ref, lq1, lk1, lq2, lk2, sub_ref, o_ref, *, tq, tk, lam_init):
    qi = pl.program_id(2)
    q = q_ref[...]
    ratio = tq // tk

    def step(j, carry, masked):
        off = pl.multiple_of(j * tk, tk)
        v = v_ref[pl.ds(off, tk), :]
        new = []
        for c in range(2):
            m, l, acc = carry[c]
            kt = kt_ref[c * HEAD:(c + 1) * HEAD, pl.ds(off, tk)]
            s = _dot(q[:, c * HEAD:(c + 1) * HEAD], kt)
            if masked:
                row = lax.broadcasted_iota(I32, (tq, tk), 0)
                col = lax.broadcasted_iota(I32, (tq, tk), 1)
                s = jnp.where(col <= row + (qi * tq - j * tk), s, -jnp.inf)
            m_new = jnp.maximum(m, jnp.max(s, axis=1, keepdims=True))
            alpha = jnp.exp(m - m_new)
            p = jnp.exp(s - m_new)
            l = alpha * l + jnp.sum(p, axis=1, keepdims=True)
            acc = alpha * acc + _dot(p.astype(BF16), v)
            new.append((m_new, l, acc))
        return tuple(new)

    init = tuple((jnp.full((tq, 1), -jnp.inf, F32), jnp.zeros((tq, 1), F32), jnp.zeros((tq, 2 * HEAD), F32))
                 for _ in range(2))
    carry = lax.fori_loop(0, qi * ratio, functools.partial(step, masked=False), init)
    if ratio == 1:
        carry = step(qi, carry, True)
    else:
        carry = lax.fori_loop(qi * ratio, (qi + 1) * ratio, functools.partial(step, masked=True), carry)
    (_, l0, a0), (_, l1, a1) = carry
    lam = _lambda(lq1, lk1, lq2, lk2, lam_init)
    o = a0 / l0 - lam * (a1 / l1)
    o_ref[...] = _sub_norm(o, sub_ref[...], lam_init).astype(o_ref.dtype)


DIFF_TQ = 1024
DIFF_TK = 1024


def _diff_attn_prompt(q, kt, v, lams, sub, *, nb, seq, lam_init):
    tq = min(seq, DIFF_TQ)
    tk = min(seq, DIFF_TK)
    nq = seq // tq
    small = lambda shape: pl.BlockSpec(shape, lambda b, h, i: (0, 0))
    return pl.pallas_call(
        functools.partial(_diff_attn_kernel, tq=tq, tk=tk, lam_init=lam_init),
        grid=(nb, H_A, nq),
        in_specs=[
            pl.BlockSpec((tq, 2 * HEAD), lambda b, h, i: (b * nq + i, h)),
            pl.BlockSpec((None, 2 * HEAD, seq), lambda b, h, i: (b, h, 0)),
            pl.BlockSpec((seq, 2 * HEAD), lambda b, h, i: (b, h)),
            small((1, HEAD)), small((1, HEAD)), small((1, HEAD)), small((1, HEAD)), small((1, 2 * HEAD)),
        ],
        out_specs=pl.BlockSpec((tq, 2 * HEAD), lambda b, h, i: (b * nq + i, h)),
        out_shape=jax.ShapeDtypeStruct(q.shape, BF16),
        compiler_params=_params(("arbitrary", "arbitrary", "arbitrary")),
        name="diff_attn_prompt",
    )(q, kt, v, *lams, sub)


PAGES_PER_STEP = 8
DSA_PAGES_PER_STEP = 16


def _block_diag_q(q, n_heads, width):
    s = q.shape[0]
    rows = jnp.concatenate([q] * n_heads, axis=0)
    r = _div_pow2(lax.broadcasted_iota(I32, rows.shape, 0), s)
    c = _div_pow2(lax.broadcasted_iota(I32, rows.shape, 1), HEAD)
    return jnp.where(r == c, rows, 0.0)


def _diff_attn_dec_kernel(pt_ref, q_ref, *refs, g, s_new, n_steps, lam_init):
    kt_refs, v_refs = refs[:g], refs[g:2 * g]
    ktn_ref, vn_ref, lq1, lk1, lq2, lk2, sub_ref, o_ref, qbd_ref, m_ref, l_ref, acc_ref = refs[2 * g:]
    step = pl.program_id(1)
    nr = 2 * H_A * s_new

    @pl.when(step == 0)
    def _():
        qbd_ref[...] = _block_diag_q(q_ref[...], 2 * H_A, D_MODEL).astype(BF16)
        m_ref[...] = jnp.full(m_ref.shape, -jnp.inf, F32)
        l_ref[...] = jnp.zeros(l_ref.shape, F32)
        acc_ref[...] = jnp.zeros(acc_ref.shape, F32)

    def consume(kts, page_v_refs, mask):
        kt = kts[0] if len(kts) == 1 else jnp.concatenate(kts, axis=1)
        s = _dot(qbd_ref[...], kt.astype(BF16))
        if mask is not None:
            s = jnp.where(mask, s, -jnp.inf)
        m = m_ref[...]
        m_new = jnp.maximum(m, jnp.max(s, axis=1, keepdims=True))
        alpha = jnp.exp(m - m_new)
        p = jnp.exp(s - m_new)
        l_ref[...] = alpha * l_ref[...] + jnp.sum(p, axis=1, keepdims=True)
        m_ref[...] = m_new
        pb = p.astype(BF16)
        for h in range(H_A):
            r = slice(2 * s_new * h, 2 * s_new * (h + 1))
            vs = [v_ref[pl.ds(h, LANES, stride=H_A), :] for v_ref in page_v_refs]
            v = vs[0] if len(vs) == 1 else jnp.concatenate(vs, axis=0)
            acc_ref[r, :] = alpha[r] * acc_ref[r, :] + _dot(pb[r], v.astype(BF16))

    @pl.when(step < n_steps - 1)
    def _():
        consume([kt_ref[...] for kt_ref in kt_refs], v_refs, None)

    @pl.when(step == n_steps - 1)
    def _():
        key = lax.broadcasted_iota(I32, (nr, LANES), 1)
        tok = lax.broadcasted_iota(I32, (nr, LANES), 0) & (s_new - 1)
        consume([ktn_ref[...]], [vn_ref], key <= tok)
        lam = _lambda(lq1, lk1, lq2, lk2, lam_init)
        o = acc_ref[...] / l_ref[...]
        outs = []
        for h in range(H_A):
            b = 2 * s_new * h
            outs.append(_sub_norm(o[b:b + s_new] - lam * o[b + s_new:b + 2 * s_new], sub_ref[...], lam_init))
        o_ref[...] = jnp.concatenate(outs, axis=1)


def _diff_attn_decode(q, kt_pool, v_pool, page_table, kt_new, v_new, lams, sub, *, s_new, lam_init):
    nb, n_pages = page_table.shape
    g = min(PAGES_PER_STEP, n_pages)
    n_steps = n_pages // g + 1
    nr = 2 * H_A * s_new

    def page_map(k):
        return lambda b, i, pt: (pt[b, jnp.minimum(i, n_steps - 2) * g + k], 0, 0)

    page = lambda k: pl.BlockSpec((None, D_MODEL, LANES), page_map(k))
    new = pl.BlockSpec((None, D_MODEL, LANES), lambda b, i, pt: (b, 0, 0))
    small = lambda shape: pl.BlockSpec(shape, lambda b, i, pt: (0, 0))
    grid_spec = pltpu.PrefetchScalarGridSpec(
        num_scalar_prefetch=1,
        grid=(nb, n_steps),
        in_specs=[pl.BlockSpec((s_new, D_MODEL), lambda b, i, pt: (b, 0))]
        + [page(k) for k in range(g)] + [page(k) for k in range(g)] + [new, new]
        + [small((1, HEAD))] * 4 + [small((1, 2 * HEAD))],
        out_specs=pl.BlockSpec((s_new, D_MODEL), lambda b, i, pt: (b, 0)),
        scratch_shapes=[
            pltpu.VMEM((nr, D_MODEL), BF16),
            pltpu.VMEM((nr, 1), F32),
            pltpu.VMEM((nr, 1), F32),
            pltpu.VMEM((nr, 2 * HEAD), F32),
        ],
    )
    return pl.pallas_call(
        functools.partial(_diff_attn_dec_kernel, g=g, s_new=s_new, n_steps=n_steps, lam_init=lam_init),
        grid_spec=grid_spec,
        out_shape=jax.ShapeDtypeStruct(q.shape, F32),
        compiler_params=_params(("arbitrary", "arbitrary")),
        name="diff_attn_decode",
    )(page_table, q, *([kt_pool] * g), *([v_pool] * g), kt_new, v_new, *lams, sub)


def _float_of_rank(u):
    key = u ^ INT_MIN
    bits = key ^ ((key >> 31) & 0x7FFFFFFF)
    return lax.bitcast_convert_type(bits, F32)


def _count(pred):
    return jnp.sum(pred.astype(I32), axis=1, keepdims=True)


def _write_bias(store, score, valid, col, n_valid, n_sel, idx_bits):
    score = jnp.where(valid, score, jnp.nan)

    def body(i, t_u):
        cand = t_u | lax.shift_left(jnp.int32(1), 31 - i)
        return jnp.where(_count(score >= _float_of_rank(cand)) >= n_sel, cand, t_u)

    t = _float_of_rank(lax.fori_loop(0, 32, body, jnp.zeros((score.shape[0], 1), I32)))
    keep_all = n_valid <= n_sel
    t = jnp.where(keep_all, -jnp.inf, t)
    tie_rows = jnp.logical_not(keep_all) & (_count(score >= t) > n_sel)
    store(jnp.where(score >= t, 0.0, NEG_BIG))

    @pl.when(jnp.max(tie_rows.astype(I32)) > 0)
    def _():
        need = n_sel - _count(score > t)
        eq = score == t

        def idx_body(i, c):
            cand = c | lax.shift_left(jnp.int32(1), idx_bits - 1 - i)
            return jnp.where(_count(eq & (col < cand)) <= need - 1, cand, c)

        c = jnp.where(tie_rows, lax.fori_loop(0, idx_bits, idx_body, jnp.zeros_like(need)), 2 ** 30)
        store(jnp.where((score > t) | (eq & (col <= c)), 0.0, NEG_BIG))


SELECT_WIDTH_STEP = 512
SCORE_CHUNK = 256


def _dsa_select_kernel(qi_ref, w_ref, kit_ref, bias_ref, *, tq, n_sel, wstep):
    seq = kit_ref.shape[1]
    j = pl.program_id(1)
    variant = lax.shift_right_logical((j + 1) * tq + wstep - 1, wstep.bit_length() - 1) - 1

    def run(width):
        qi = qi_ref[...]
        w = w_ref[...]
        chunks = []
        for c in range(width // SCORE_CHUNK):
            kit = kit_ref[:, c * SCORE_CHUNK:(c + 1) * SCORE_CHUNK]
            sc = w[:, 0:1] * jnp.maximum(_dot(qi[:, 0:HEAD], kit), 0.0)
            for h in range(1, H_I):
                sc = sc + w[:, h:h + 1] * jnp.maximum(_dot(qi[:, h * HEAD:(h + 1) * HEAD], kit), 0.0)
            chunks.append(sc)
        score = chunks[0] if len(chunks) == 1 else jnp.concatenate(chunks, axis=1)
        row = j * tq + lax.broadcasted_iota(I32, (tq, width), 0)
        col = lax.broadcasted_iota(I32, (tq, width), 1)

        def store(bias):
            bias_ref[:, :width] = bias.astype(bias_ref.dtype)

        _write_bias(store, score, col <= row, col, row[:, :1] + 1, n_sel, width.bit_length())
        if width < seq:
            bias_ref[:, width:] = jnp.full((tq, seq - width), NEG_BIG, bias_ref.dtype)

    for k in range(seq // wstep):
        pl.when(variant == k)(functools.partial(run, (k + 1) * wstep))


def _dsa_select_prompt(qi, w, kit, *, nb, seq, n_sel):
    tq = min(seq, SELECT_WIDTH_STEP)
    nq = seq // tq
    return pl.pallas_call(
        functools.partial(_dsa_select_kernel, tq=tq, n_sel=n_sel, wstep=min(seq, SELECT_WIDTH_STEP)),
        grid=(nb, nq),
        in_specs=[
            pl.BlockSpec((tq, H_I * HEAD), lambda b, i: (b * nq + i, 0)),
            pl.BlockSpec((tq, LANES), lambda b, i: (b * nq + i, 0)),
            pl.BlockSpec((None, HEAD, seq), lambda b, i: (b, 0, 0)),
        ],
        out_specs=pl.BlockSpec((None, tq, seq), lambda b, i: (b, i, 0)),
        out_shape=jax.ShapeDtypeStruct((nb, seq, seq), BF16),
        compiler_params=_params(("arbitrary", "arbitrary")),
        name="dsa_select_prompt",
    )(qi, w, kit)


def _dsa_score_dec_kernel(pt_ref, qi_ref, w_ref, *refs, g, s_new, n_steps, n_past):
    kit_refs = refs[:g]
    kitn_ref, score_ref, qs_ref, ws_ref = refs[g:]
    step = pl.program_id(1)

    @pl.when(step == 0)
    def _():
        qi = qi_ref[...]
        w = w_ref[...]
        qs_ref[...] = jnp.concatenate([qi[:, h * HEAD:(h + 1) * HEAD] for h in range(H_I)], axis=0).astype(BF16)
        ws_ref[...] = jnp.concatenate([w[:, h:h + 1] for h in range(H_I)], axis=0)

    def page_score(kit):
        d = jnp.maximum(_dot(qs_ref[...], kit.astype(BF16)), 0.0) * ws_ref[...]
        sc = d[0:s_new]
        for h in range(1, H_I):
            sc = sc + d[h * s_new:(h + 1) * s_new]
        return sc

    @pl.when(step < n_steps - 1)
    def _():
        off = pl.multiple_of(step * (g * LANES), g * LANES)
        score_ref[:, pl.ds(off, g * LANES)] = page_score(jnp.concatenate([r[...] for r in kit_refs], axis=1))

    @pl.when(step == n_steps - 1)
    def _():
        score_ref[:, n_past:n_past + LANES] = page_score(kitn_ref[...])


def _dsa_select_rows_kernel(score_ref, bias_ref, *, s_new, n_sel, n_past):
    rows, width = score_ref.shape
    col = lax.broadcasted_iota(I32, (rows, width), 1)
    tok = lax.broadcasted_iota(I32, (rows, width), 0) & (s_new - 1)

    def store(bias):
        bias_ref[...] = bias

    _write_bias(store, score_ref[...], col <= n_past + tok, col, n_past + tok[:, :1] + 1, n_sel, width.bit_length())


SELECT_PAGES_PER_STEP = 32


def _dsa_select_decode(qi, w, kit_pool, page_table, kit_new, *, s_new, n_sel):
    nb, n_pages = page_table.shape
    g = min(SELECT_PAGES_PER_STEP, n_pages)
    n_steps = n_pages // g + 1
    n_past = n_pages * LANES
    width = n_past + LANES

    def page_map(k):
        return lambda b, i, pt: (pt[b, jnp.minimum(i, n_steps - 2) * g + k], 0, 0)

    grid_spec = pltpu.PrefetchScalarGridSpec(
        num_scalar_prefetch=1,
        grid=(nb, n_steps),
        in_specs=[pl.BlockSpec((s_new, H_I * HEAD), lambda b, i, pt: (b, 0)),
                  pl.BlockSpec((s_new, LANES), lambda b, i, pt: (b, 0))]
        + [pl.BlockSpec((None, HEAD, LANES), page_map(k)) for k in range(g)]
        + [pl.BlockSpec((None, HEAD, LANES), lambda b, i, pt: (b, 0, 0))],
        out_specs=pl.BlockSpec((None, s_new, width), lambda b, i, pt: (b, 0, 0)),
        scratch_shapes=[
            pltpu.VMEM((H_I * s_new, HEAD), BF16),
            pltpu.VMEM((H_I * s_new, 1), F32),
        ],
    )
    score = pl.pallas_call(
        functools.partial(_dsa_score_dec_kernel, g=g, s_new=s_new, n_steps=n_steps, n_past=n_past),
        grid_spec=grid_spec,
        out_shape=jax.ShapeDtypeStruct((nb, s_new, width), F32),
        compiler_params=_params(("arbitrary", "arbitrary")),
        name="dsa_score_decode",
    )(page_table, qi, w, *([kit_pool] * g), kit_new)
    rows = nb * s_new
    tr = min(rows, LANES)
    bias = pl.pallas_call(
        functools.partial(_dsa_select_rows_kernel, s_new=s_new, n_sel=n_sel, n_past=n_past),
        grid=(rows // tr,),
        in_specs=[pl.BlockSpec((tr, width), lambda i: (i, 0))],
        out_specs=pl.BlockSpec((tr, width), lambda i: (i, 0)),
        out_shape=jax.ShapeDtypeStruct((rows, width), F32),
        compiler_params=_params(("arbitrary",)),
        name="dsa_select_decode",
    )(score.reshape(rows, width))
    return bias.reshape(nb, s_new, width)


def _dsa_attn_kernel(q_ref, kt_ref, v_ref, bias_ref, o_ref, *, tq, tk):
    grp = H_B // KV_B
    qi = pl.program_id(2)
    qb = q_ref[...]
    q4 = jnp.concatenate([qb[:, h * HEAD:(h + 1) * HEAD] for h in range(grp)], axis=0)
    n_kv = lax.shift_right_logical((qi + 1) * tq + tk - 1, tk.bit_length() - 1)

    def step(j, carry):
        m, l, acc = carry
        off = pl.multiple_of(j * tk, tk)
        s = _dot(q4, kt_ref[:, pl.ds(off, tk)])
        s = (s.reshape(grp, tq, tk) + bias_ref[:, pl.ds(off, tk)].astype(F32)[None]).reshape(grp * tq, tk)
        m_new = jnp.maximum(m, jnp.max(s, axis=1, keepdims=True))
        alpha = jnp.exp(m - m_new)
        p = jnp.exp(s - m_new)
        l = alpha * l + jnp.sum(p, axis=1, keepdims=True)
        acc = alpha * acc + _dot(p.astype(BF16), v_ref[pl.ds(off, tk), :])
        return m_new, l, acc

    init = (jnp.full((grp * tq, 1), -jnp.inf, F32), jnp.zeros((grp * tq, 1), F32), jnp.zeros((grp * tq, HEAD), F32))
    _, l, acc = lax.fori_loop(0, n_kv, step, init)
    o = acc / l
    o_ref[...] = jnp.concatenate([o[h * tq:(h + 1) * tq] for h in range(grp)], axis=1).astype(o_ref.dtype)


DSA_TQ = 256
DSA_TK = 512


def _dsa_attn_prompt(q, kt, v, bias, *, nb, seq):
    tq = min(seq, DSA_TQ)
    tk = min(seq, DSA_TK)
    nq = seq // tq
    grp = H_B // KV_B
    return pl.pallas_call(
        functools.partial(_dsa_attn_kernel, tq=tq, tk=tk),
        grid=(nb, KV_B, nq),
        in_specs=[
            pl.BlockSpec((tq, grp * HEAD), lambda b, g, i: (b * nq + i, g)),
            pl.BlockSpec((None, HEAD, seq), lambda b, g, i: (b, g, 0)),
            pl.BlockSpec((None, seq, HEAD), lambda b, g, i: (g, b, 0)),
            pl.BlockSpec((None, tq, seq), lambda b, g, i: (b, i, 0)),
        ],
        out_specs=pl.BlockSpec((tq, grp * HEAD), lambda b, g, i: (b * nq + i, g)),
        out_shape=jax.ShapeDtypeStruct(q.shape, BF16),
        compiler_params=_params(("arbitrary", "arbitrary", "arbitrary")),
        name="dsa_attn_prompt",
    )(q, kt, v, bias)


def _dsa_attn_dec_kernel(pt_ref, q_ref, bias_ref, *refs, g, s_new, n_steps, n_past):
    kt_refs, vt_refs = refs[:g], refs[g:2 * g]
    ktn_ref, vtn_ref, o_ref, qbd_ref, m_ref, l_ref, acc_ref = refs[2 * g:]
    step = pl.program_id(1)
    nr = H_B * s_new
    grp = H_B // KV_B

    @pl.when(step == 0)
    def _():
        q = q_ref[...]
        rows = jnp.concatenate([q[:, h * HEAD:(h + 1) * HEAD] for h in range(H_B)], axis=0)
        wide = jnp.concatenate([rows] * KV_B, axis=1)
        r = _div_pow2(lax.broadcasted_iota(I32, wide.shape, 0), grp * s_new)
        c = _div_pow2(lax.broadcasted_iota(I32, wide.shape, 1), HEAD)
        qbd_ref[...] = jnp.where(r == c, wide, 0.0).astype(BF16)
        m_ref[...] = jnp.full(m_ref.shape, -jnp.inf, F32)
        l_ref[...] = jnp.zeros(l_ref.shape, F32)
        acc_ref[...] = jnp.zeros(acc_ref.shape, F32)

    def consume(ch, kt, vt, bias):
        keys = kt.shape[1]
        s = _dot(qbd_ref[...], kt.astype(BF16))
        s = (s.reshape(H_B, s_new, keys) + bias[None]).reshape(nr, keys)
        m = m_ref[ch]
        m_new = jnp.maximum(m, jnp.max(s, axis=1, keepdims=True))
        alpha = jnp.exp(m - m_new)
        p = jnp.exp(s - m_new)
        l_ref[ch] = alpha * l_ref[ch] + jnp.sum(p, axis=1, keepdims=True)
        m_ref[ch] = m_new
        acc_ref[ch] = alpha * acc_ref[ch] + _dot_nt(p.astype(BF16), vt.astype(BF16))

    @pl.when(step < n_steps - 1)
    def _():
        half = g // 2
        for ch in range(2):
            off = pl.multiple_of(step * (g * LANES) + ch * (half * LANES), half * LANES)
            pages = slice(ch * half, (ch + 1) * half)
            consume(ch, jnp.concatenate([r[...] for r in kt_refs[pages]], axis=1),
                    jnp.concatenate([r[...] for r in vt_refs[pages]], axis=1), bias_ref[:, pl.ds(off, half * LANES)])

    @pl.when(step == n_steps - 1)
    def _():
        consume(0, ktn_ref[...], vtn_ref[...], bias_ref[:, n_past:n_past + LANES])
        m = jnp.maximum(m_ref[0], m_ref[1])
        a0, a1 = jnp.exp(m_ref[0] - m), jnp.exp(m_ref[1] - m)
        o = (a0 * acc_ref[0] + a1 * acc_ref[1]) / (a0 * l_ref[0] + a1 * l_ref[1])
        outs = []
        for h in range(H_B):
            kv = h // grp
            outs.append(o[h * s_new:(h + 1) * s_new, kv * HEAD:(kv + 1) * HEAD])
        o_ref[...] = jnp.concatenate(outs, axis=1)


def _dsa_attn_decode(q, bias, kt_pool, vt_pool, page_table, kt_new, vt_new, *, s_new):
    nb, n_pages = page_table.shape
    g = min(DSA_PAGES_PER_STEP, n_pages)
    assert g % 2 == 0 and n_pages % g == 0
    n_steps = n_pages // g + 1
    n_past = n_pages * LANES
    nr = H_B * s_new
    kvw = KV_B * HEAD

    def page_map(k):
        return lambda b, i, pt: (pt[b, jnp.minimum(i, n_steps - 2) * g + k], 0, 0)

    page = lambda k: pl.BlockSpec((None, kvw, LANES), page_map(k))
    new = pl.BlockSpec((None, kvw, LANES), lambda b, i, pt: (b, 0, 0))
    grid_spec = pltpu.PrefetchScalarGridSpec(
        num_scalar_prefetch=1,
        grid=(nb, n_steps),
        in_specs=[pl.BlockSpec((s_new, D_MODEL), lambda b, i, pt: (b, 0)),
                  pl.BlockSpec((None, s_new, n_past + LANES), lambda b, i, pt: (b, 0, 0))]
        + [page(k) for k in range(g)] + [page(k) for k in range(g)] + [new, new],
        out_specs=pl.BlockSpec((s_new, D_MODEL), lambda b, i, pt: (b, 0)),
        scratch_shapes=[
            pltpu.VMEM((nr, kvw), BF16),
            pltpu.VMEM((2, nr, 1), F32),
            pltpu.VMEM((2, nr, 1), F32),
            pltpu.VMEM((2, nr, kvw), F32),
        ],
    )
    return pl.pallas_call(
        functools.partial(_dsa_attn_dec_kernel, g=g, s_new=s_new, n_steps=n_steps, n_past=n_past),
        grid_spec=grid_spec,
        out_shape=jax.ShapeDtypeStruct(q.shape, F32),
        compiler_params=_params(("arbitrary", "arbitrary")),
        name="dsa_attn_decode",
    )(page_table, q, bias, *([kt_pool] * g), *([vt_pool] * g), kt_new, vt_new)


def _outproj_ln_kernel(o_ref, w_ref, x_ref, g_ref, b_ref, *rest, route):
    h = _dot(o_ref[...].astype(BF16), w_ref[...])
    y = _layer_norm(ALPHA * x_ref[...] + h, g_ref[...], b_ref[...])
    if not route:
        rest[0][...] = y
        return
    rwh_ref, rwl_ref, rb_ref, out_ref, cls_ref = rest
    out_ref[...] = y
    gi, i1, i2, _, _ = _route_choice(*_router_rows(y, rwh_ref[...], rwl_ref[...], rb_ref[...]))
    lo, hi = jnp.minimum(i1, i2), jnp.maximum(i1, i2)
    pair = jnp.where(lo == 0, 0, jnp.where(lo == 1, 3, 5)) + hi - lo - 1
    cls_ref[...] = gi * PAIRS_PER_GROUP + pair


def _outproj_ln(o, w, x, g, b, router=None, *, tm):
    t, d = x.shape
    row = lambda n: pl.BlockSpec((tm, n), lambda i: (i, 0))
    full = lambda shape: pl.BlockSpec(shape, lambda i: (0, 0))
    in_specs = [row(o.shape[1]), full(w.shape), row(d), full((1, d)), full((1, d))]
    out_specs, out_shape = row(d), jax.ShapeDtypeStruct((t, d), F32)
    if router is not None:
        in_specs += [full(r.shape) for r in router]
        out_specs = [out_specs, pl.BlockSpec((1, tm), lambda i: (0, i))]
        out_shape = [out_shape, jax.ShapeDtypeStruct((1, t), I32)]
    return pl.pallas_call(
        functools.partial(_outproj_ln_kernel, route=router is not None),
        grid=(t // tm,),
        in_specs=in_specs,
        out_specs=out_specs,
        out_shape=out_shape,
        compiler_params=_params(("arbitrary",)),
        name="outproj_ln",
    )(o, w, x, g, b, *(router or ()))


def _route_choice(sel, aff):
    gs = []
    for g in range(N_GROUPS):
        a, b, c, d = sel[4 * g:4 * g + 4]
        hi1, lo1, hi2, lo2 = jnp.maximum(a, b), jnp.minimum(a, b), jnp.maximum(c, d), jnp.minimum(c, d)
        gs.append(jnp.maximum(hi1, hi2) + jnp.maximum(jnp.minimum(hi1, hi2), jnp.maximum(lo1, lo2)))
    best, gi = gs[0], jnp.zeros(gs[0].shape, I32)
    for g in range(1, N_GROUPS):
        better = gs[g] > best
        best = jnp.where(better, gs[g], best)
        gi = jnp.where(better, g, gi)

    def pick(rows):
        out = []
        for j in range(E_PER_GROUP):
            v = rows[j]
            for g in range(1, N_GROUPS):
                v = jnp.where(gi == g, rows[4 * g + j], v)
            out.append(v)
        return out

    sv, av = pick(sel), pick(aff)

    def argmax_first(vals):
        bv, bi = vals[0], jnp.zeros(vals[0].shape, I32)
        for j in range(1, E_PER_GROUP):
            better = vals[j] > bv
            bv = jnp.where(better, vals[j], bv)
            bi = jnp.where(better, j, bi)
        return bi

    i1 = argmax_first(sv)
    i2 = argmax_first([jnp.where(i1 == j, -jnp.inf, sv[j]) for j in range(E_PER_GROUP)])
    g1, g2 = av[0], av[0]
    for j in range(1, E_PER_GROUP):
        g1 = jnp.where(i1 == j, av[j], g1)
        g2 = jnp.where(i2 == j, av[j], g2)
    tot = g1 + g2
    return gi, i1, i2, g1 / tot, g2 / tot


def _route(sel, aff):
    gi, i1, i2, g1, g2 = _route_choice(sel, aff)
    comb = []
    for e in range(N_EXPERTS):
        g, j = divmod(e, E_PER_GROUP)
        in_g = gi == g
        comb.append(jnp.where(in_g & (i1 == j), g1, jnp.where(in_g & (i2 == j), g2, 0.0)))
    return comb


def _router_rows(x, rwt_hi, rwt_lo, rb):
    xh = x.astype(BF16)
    xl = (x - xh.astype(F32)).astype(BF16)
    logits = _dot_nt(rwt_hi, xh) + (_dot_nt(rwt_hi, xl) + _dot_nt(rwt_lo, xh))
    aff = _sigmoid(logits)
    sel = aff + rb
    return [sel[e:e + 1] for e in range(N_EXPERTS)], [aff[e:e + 1] for e in range(N_EXPERTS)]


def _moe_dense_kernel(x_ref, rwh_ref, rwl_ref, rb_ref, wg_ref, wu_ref, wd_ref, g_ref, b_ref, out_ref,
                      xb_ref, comb_ref, acc_ref, *, tm):
    e = pl.program_id(1)

    @pl.when(e == 0)
    def _():
        x = x_ref[...]
        xb_ref[...] = x.astype(BF16)
        sel, aff = _router_rows(x, rwh_ref[...], rwl_ref[...], rb_ref[...])
        comb = jnp.concatenate(_route(sel, aff) + [jnp.zeros((LANES - N_EXPERTS, tm), F32)], axis=0)
        comb_ref[...] = comb.T
        acc_ref[...] = jnp.zeros(acc_ref.shape, F32)

    xb = xb_ref[...]
    hg = _dot(xb, wg_ref[...])
    hu = _dot(xb, wu_ref[...])
    hdn = (hg * _sigmoid(hg)) * hu
    y = _dot(hdn.astype(BF16), wd_ref[...])
    lane = lax.broadcasted_iota(I32, (tm, LANES), 1)
    col = jnp.sum(jnp.where(lane == e, comb_ref[...], 0.0), axis=1, keepdims=True)
    acc_ref[...] += col * y

    @pl.when(e == N_EXPERTS - 1)
    def _():
        out_ref[...] = _layer_norm(ALPHA * x_ref[...] + acc_ref[...], g_ref[...], b_ref[...])


def _moe_ln(x, rwt_hi, rwt_lo, rb, wg, wu, wd, g, b, *, tm):
    t, d = x.shape
    f = wg.shape[2]
    full = lambda shape: pl.BlockSpec(shape, lambda i, e: (0,) * len(shape))
    return pl.pallas_call(
        functools.partial(_moe_dense_kernel, tm=tm),
        grid=(t // tm, N_EXPERTS),
        in_specs=[
            pl.BlockSpec((tm, d), lambda i, e: (i, 0)),
            full((N_EXPERTS, d)), full((N_EXPERTS, d)), full((N_EXPERTS, 1)),
            pl.BlockSpec((None, d, f), lambda i, e: (e, 0, 0)),
            pl.BlockSpec((None, d, f), lambda i, e: (e, 0, 0)),
            pl.BlockSpec((None, f, d), lambda i, e: (e, 0, 0)),
            full((1, d)), full((1, d)),
        ],
        out_specs=pl.BlockSpec((tm, d), lambda i, e: (i, 0)),
        out_shape=jax.ShapeDtypeStruct((t, d), F32),
        scratch_shapes=[pltpu.VMEM((tm, d), BF16), pltpu.VMEM((tm, LANES), F32), pltpu.VMEM((tm, d), F32)],
        compiler_params=_params(("arbitrary", "arbitrary")),
        name="moe_ln",
    )(x, rwt_hi, rwt_lo, rb, wg, wu, wd, g, b)


PAIRS_PER_GROUP = 6
N_CLASSES = N_GROUPS * PAIRS_PER_GROUP
MOE_TILE = 256


def _moe_plan_kernel(cls_ref, slot_ref, tcls_ref, nused_ref, *, tm):
    cls = cls_ref[...]
    r = cls.shape[0]
    upper = jnp.where(lax.broadcasted_iota(I32, (LANES, LANES), 0) < lax.broadcasted_iota(I32, (LANES, LANES), 1),
                      1.0, 0.0).astype(BF16)
    lower = jnp.where(lax.broadcasted_iota(I32, (r, r), 1) < lax.broadcasted_iota(I32, (r, r), 0),
                      1.0, 0.0).astype(BF16)
    tile_start = lax.broadcasted_iota(I32, (1, LANES), 1).astype(F32) * tm
    base = jnp.zeros((1, 1), F32)
    slot = jnp.zeros((r, LANES), F32)
    tcls = jnp.zeros((1, LANES), F32)
    last = jnp.zeros((1, 1), F32)
    for c in range(N_CLASSES):
        oh = cls == c
        ohf = jnp.where(oh, 1.0, 0.0)
        before_in_row = _dot(ohf.astype(BF16), upper)
        row_total = jnp.sum(ohf, axis=1, keepdims=True)
        before_rows = _dot(lower, jnp.broadcast_to(row_total, (r, LANES)).astype(BF16))
        count = jnp.sum(row_total, axis=0, keepdims=True)
        padded = jnp.ceil(count / tm) * tm
        slot = slot + jnp.where(oh, base + before_in_row + before_rows, 0.0)
        tcls = jnp.where((tile_start >= base) & (tile_start < base + padded), float(c), tcls)
        last = jnp.where(padded > 0, float(c), last)
        base = base + padded
    slot_ref[...] = slot.astype(I32)
    tcls_ref[...] = jnp.where(tile_start >= base, last, tcls).astype(I32)
    nused_ref[...] = jnp.broadcast_to(base / tm, (1, LANES)).astype(I32)


def _moe_plan(cls, *, tm):
    t = cls.shape[1]
    r = t // LANES
    full = lambda shape: pl.BlockSpec(shape, lambda: (0, 0))
    slot, tcls, nused = pl.pallas_call(
        functools.partial(_moe_plan_kernel, tm=tm),
        in_specs=[full((r, LANES))],
        out_specs=[full((r, LANES)), full((1, LANES)), full((1, LANES))],
        out_shape=[jax.ShapeDtypeStruct((r, LANES), I32), jax.ShapeDtypeStruct((1, LANES), I32),
                   jax.ShapeDtypeStruct((1, LANES), I32)],
        name="moe_plan",
    )(cls.reshape(r, LANES))
    return slot.reshape(t // tm, 1, tm), tcls[0], nused[0, :1]


def _wait_rows(src, dst, sem):
    pltpu.make_async_copy(src, dst, sem).wait()


def _moe_scatter_kernel(x_ref, slot_ref, xs_in, xs_out, stage, sems, *, tm, n_steps):
    i = pl.program_id(0)
    b = i & 1
    tile_rows = xs_out.at[pl.ds(0, tm)]

    @pl.when(i >= 2)
    def _():
        _wait_rows(stage.at[b], tile_rows, sems.at[b])

    stage[b] = x_ref[...]

    def issue(r, carry):
        pltpu.make_async_copy(stage.at[b, pl.ds(r, 1)], xs_out.at[pl.ds(slot_ref[0, r], 1)], sems.at[b]).start()
        return carry

    lax.fori_loop(0, tm, issue, 0, unroll=8)

    @pl.when(i == n_steps - 1)
    def _():
        _wait_rows(stage.at[b], tile_rows, sems.at[b])
        if n_steps > 1:
            _wait_rows(stage.at[1 - b], tile_rows, sems.at[1 - b])


def _moe_scatter(x, slot, *, tm, n_rows):
    t, d = x.shape
    n_steps = t // tm
    return pl.pallas_call(
        functools.partial(_moe_scatter_kernel, tm=tm, n_steps=n_steps),
        grid=(n_steps,),
        in_specs=[
            pl.BlockSpec((tm, d), lambda i: (i, 0)),
            pl.BlockSpec((None, 1, tm), lambda i: (i, 0, 0), memory_space=pltpu.SMEM),
            pl.BlockSpec(memory_space=pl.ANY),
        ],
        out_specs=pl.BlockSpec(memory_space=pl.ANY),
        out_shape=jax.ShapeDtypeStruct((n_rows, d), F32),
        scratch_shapes=[pltpu.VMEM((2, tm, d), F32), pltpu.SemaphoreType.DMA((2,))],
        input_output_aliases={2: 0},
        compiler_params=_params(("arbitrary",)),
        name="moe_scatter",
    )(x, slot, jnp.zeros((n_rows, d), F32))


def _class_experts(c):
    g = c // PAIRS_PER_GROUP
    pair = c - g * PAIRS_PER_GROUP
    lo = (pair >= 3).astype(I32) + (pair >= 5).astype(I32)
    hi = pair - (3 * (lo >= 1).astype(I32) + 2 * (lo >= 2).astype(I32)) + lo + 1
    return g * E_PER_GROUP + lo, g * E_PER_GROUP + hi


def _moe_sorted_kernel(tcls_ref, nused_ref, xs_ref, rwh_ref, rwl_ref, wg1, wu1, wd1, wg2, wu2, wd2,
                       g_ref, b_ref, out_ref, *, tm):
    i = pl.program_id(0)

    @pl.when(i >= nused_ref[0])
    def _():
        out_ref[...] = jnp.zeros(out_ref.shape, F32)

    @pl.when(i < nused_ref[0])
    def _():
        e_lo, e_hi = _class_experts(tcls_ref[i])
        x = xs_ref[...]
        xb = x.astype(BF16)
        xl = (x - xb.astype(F32)).astype(BF16)
        logits = _dot(xb, rwh_ref[...]) + (_dot(xl, rwh_ref[...]) + _dot(xb, rwl_ref[...]))
        aff = _sigmoid(logits)
        lane = lax.broadcasted_iota(I32, (tm, LANES), 1)
        a_lo = jnp.sum(jnp.where(lane == e_lo, aff, 0.0), axis=1, keepdims=True)
        a_hi = jnp.sum(jnp.where(lane == e_hi, aff, 0.0), axis=1, keepdims=True)
        tot = a_lo + a_hi

        def expert(wg, wu, wd):
            hg = _dot(xb, wg[...])
            return _dot(((hg * _sigmoid(hg)) * _dot(xb, wu[...])).astype(BF16), wd[...])

        acc = (a_lo / tot) * expert(wg1, wu1, wd1)
        acc = acc + (a_hi / tot) * expert(wg2, wu2, wd2)
        out_ref[...] = _layer_norm(ALPHA * x + acc, g_ref[...], b_ref[...])


def _moe_sorted(xs, tcls, nused, rw_hi, rw_lo, wg, wu, wd, g, b, *, tm):
    n_rows, d = xs.shape
    f = wg.shape[2]
    full = lambda shape: pl.BlockSpec(shape, lambda i, tc, nu: (0,) * len(shape))

    def w_spec(shape, which):
        return pl.BlockSpec((None,) + shape, lambda i, tc, nu: (_class_experts(tc[i])[which], 0, 0))

    grid_spec = pltpu.PrefetchScalarGridSpec(
        num_scalar_prefetch=2,
        grid=(n_rows // tm,),
        in_specs=[pl.BlockSpec((tm, d), lambda i, tc, nu: (i, 0)), full(rw_hi.shape), full(rw_lo.shape),
                  w_spec((d, f), 0), w_spec((d, f), 0), w_spec((f, d), 0),
                  w_spec((d, f), 1), w_spec((d, f), 1), w_spec((f, d), 1),
                  full((1, d)), full((1, d))],
        out_specs=pl.BlockSpec((tm, d), lambda i, tc, nu: (i, 0)),
    )
    return pl.pallas_call(
        functools.partial(_moe_sorted_kernel, tm=tm),
        grid_spec=grid_spec,
        out_shape=jax.ShapeDtypeStruct((n_rows, d), F32),
        compiler_params=_params(("arbitrary",)),
        name="moe_sorted",
    )(tcls, nused, xs, rw_hi, rw_lo, wg, wu, wd, wg, wu, wd, g, b)


def _ple_math(x, p_ref, wg_ref, bg_ref, wp_ref):
    gate = _sigmoid(_dot(x.astype(BF16), wg_ref[...]) + bg_ref[...])
    return x + gate * _dot(p_ref[...].astype(BF16), wp_ref[...])


def _ple_gather_kernel(slot_ref, slot_next_ref, ys_ref, p_ref, wg_ref, bg_ref, wp_ref, out_ref, buf, sems,
                       *, tm, n_steps):
    i = pl.program_id(0)
    b = i & 1

    def issue(slots, bb):
        def body(r, carry):
            pltpu.make_async_copy(ys_ref.at[pl.ds(slots[0, r], 1)], buf.at[bb, pl.ds(r, 1)], sems.at[bb]).start()
            return carry
        lax.fori_loop(0, tm, body, 0, unroll=8)

    @pl.when(i == 0)
    def _():
        issue(slot_ref, 0)

    @pl.when(i + 1 < n_steps)
    def _():
        issue(slot_next_ref, 1 - b)

    _wait_rows(ys_ref.at[pl.ds(0, tm)], buf.at[b], sems.at[b])
    out_ref[...] = _ple_math(buf[b], p_ref, wg_ref, bg_ref, wp_ref)


def _ple_gather(ys, slot, p, wg, bg, wp, *, tm):
    d = ys.shape[1]
    t, pd = p.shape
    n_steps = t // tm
    full = lambda shape: pl.BlockSpec(shape, lambda i: (0, 0))
    return pl.pallas_call(
        functools.partial(_ple_gather_kernel, tm=tm, n_steps=n_steps),
        grid=(n_steps,),
        in_specs=[
            pl.BlockSpec((None, 1, tm), lambda i: (i, 0, 0), memory_space=pltpu.SMEM),
            pl.BlockSpec((None, 1, tm), lambda i: (jnp.minimum(i + 1, n_steps - 1), 0, 0), memory_space=pltpu.SMEM),
            pl.BlockSpec(memory_space=pl.ANY),
            pl.BlockSpec((tm, pd), lambda i: (i, 0)),
            full((d, d)), full((1, d)), full((pd, d)),
        ],
        out_specs=pl.BlockSpec((tm, d), lambda i: (i, 0)),
        out_shape=jax.ShapeDtypeStruct((t, d), F32),
        scratch_shapes=[pltpu.VMEM((2, tm, d), F32), pltpu.SemaphoreType.DMA((2,))],
        compiler_params=_params(("arbitrary",)),
        name="ple_gather",
    )(slot, slot, ys, p, wg, bg, wp)


def _ple_kernel(x_ref, p_ref, wg_ref, bg_ref, wp_ref, out_ref):
    out_ref[...] = _ple_math(x_ref[...], p_ref, wg_ref, bg_ref, wp_ref)


def _ple(x, p, wg, bg, wp, *, tm):
    t, d = x.shape
    pd = p.shape[1]
    full = lambda shape: pl.BlockSpec(shape, lambda i: (0, 0))
    return pl.pallas_call(
        _ple_kernel,
        grid=(t // tm,),
        in_specs=[pl.BlockSpec((tm, d), lambda i: (i, 0)), pl.BlockSpec((tm, pd), lambda i: (i, 0)),
                  full((d, d)), full((1, d)), full((pd, d))],
        out_specs=pl.BlockSpec((tm, d), lambda i: (i, 0)),
        out_shape=jax.ShapeDtypeStruct((t, d), F32),
        compiler_params=_params(("arbitrary",)),
        name="ple",
    )(x, p, wg, bg, wp)


def _pad_page_t(x, nb, s_new):
    xt = x.reshape(nb, s_new, -1).transpose(0, 2, 1)
    return jnp.pad(xt, ((0, 0), (0, 0), (0, LANES - s_new)))


def _layer_tail(o, w_out, x, p, w, layer, *, sort_tokens, tm):
    t = x.shape[0]
    ln_g, ln_b = w["ln_g"][layer], w["ln_b"][layer]
    wg, wu, wd = w["moe_w_gate"][layer], w["moe_w_up"][layer], w["moe_w_down"][layer]
    ple = (w["ple_gate_w"][layer], w["ple_gate_b"][layer], w["ple_proj"][layer])
    router = (w["rwt_hi"], w["rwt_lo"], w["rb"])
    if not sort_tokens:
        x = _outproj_ln(o, w_out, x, ln_g[0], ln_b[0], tm=tm)
        x = _moe_ln(x, *router, wg, wu, wd, ln_g[1], ln_b[1], tm=512 if t % 512 == 0 else tm)
        return _ple(x, p, *ple, tm=tm)
    n_rows = t + N_CLASSES * MOE_TILE
    assert n_rows // MOE_TILE <= LANES, "the plan kernel lists at most 128 tiles"
    x, cls = _outproj_ln(o, w_out, x, ln_g[0], ln_b[0], router, tm=tm)
    slot, tile_cls, n_used = _moe_plan(cls, tm=MOE_TILE)
    xs = _moe_scatter(x, slot, tm=MOE_TILE, n_rows=n_rows)
    ys = _moe_sorted(xs, tile_cls, n_used, w["rw_hi"], w["rw_lo"], wg, wu, wd, ln_g[1], ln_b[1], tm=MOE_TILE)
    return _ple_gather(ys, slot, p, *ple, tm=MOE_TILE)


def _trunk(x3, p4, pos_off, past, page_table, w):
    nb, seq, d = x3.shape
    t = nb * seq
    prompt = past is None
    x = x3.reshape(t, d)
    p = p4.reshape(DEPTH, t, -1)
    tm = min(256, t)
    if prompt:
        tabs = _rope_tables(seq, seq, pos_off)
    else:
        tabs = _rope_tables(t, seq, pos_off)
    outs = {}

    lam_init = 0.8 - 0.6 * math.exp(-0.3 * 0)
    w_in = w["a_w_in"]
    nq = 2 * H_A * HD_A
    wq, wk, wv = w_in[:, :nq], w_in[:, nq:2 * nq], w_in[:, 2 * nq:]
    lams = [w[n] for n in ("a_lam_q1", "a_lam_k1", "a_lam_q2", "a_lam_k2")]
    if prompt:
        segs = [
            dict(w=wq, tr=False, rot=8, scale=HD_A ** -0.5, outs=[(BF16, False)]),
            dict(w=wk.T, tr=True, rot=8, scale=1.0, outs=[(F32, False), (BF16, False)]),
            dict(w=wv, tr=False, rot=0, scale=1.0, outs=[(F32, False), (BF16, False)]),
        ]
        q, kt, kt_b, v, v_b = _inproj(x, tabs, segs, seq=seq, tm=tm)
        o = _diff_attn_prompt(q, kt_b, v_b, lams, w["a_subln"], nb=nb, seq=seq, lam_init=lam_init)
        outs["a_k"] = kt.reshape(nb, 2 * H_A, HD_A, seq).transpose(0, 3, 1, 2)
    else:
        segs = [
            dict(w=wq, tr=False, rot=8, scale=HD_A ** -0.5, outs=[(F32, False)]),
            dict(w=wk, tr=False, rot=8, scale=1.0, outs=[(F32, False)]),
            dict(w=wv, tr=False, rot=0, scale=1.0, outs=[(F32, False)]),
        ]
        q, k, v = _inproj(x, tabs, segs, seq=seq, tm=tm)
        v_new = jnp.pad(v.reshape(nb, seq, H_A, 2 * HD_A), ((0, 0), (0, LANES - seq), (0, 0), (0, 0)))
        o = _diff_attn_decode(q, past["a_kt"], past["a_v"], page_table, _pad_page_t(k, nb, seq),
                              v_new.reshape(nb, LANES * H_A, 2 * HD_A), lams, w["a_subln"],
                              s_new=seq, lam_init=lam_init)
        outs["a_k"] = k.reshape(nb, seq, 2 * H_A, HD_A)
    outs["a_v"] = v.reshape(nb, seq, H_A, 2 * HD_A)
    x = _layer_tail(o, w["a_w_out"], x, p[0], w, 0, sort_tokens=prompt, tm=tm)

    w_in = w["b_w_in"]
    sizes = (H_B * HD_B, KV_B * HD_B, KV_B * HD_B, H_I * D_I, D_I, H_I)
    offs = [sum(sizes[:m]) for m in range(len(sizes) + 1)]
    wq, wk, wv, wqi, wki, wwi = [w_in[:, offs[m]:offs[m + 1]] for m in range(len(sizes))]
    wwi = jnp.pad(wwi, ((0, 0), (0, LANES - H_I)))
    if prompt:
        n_sel = min(TOPK_MAX, seq // 4)
        segs = [
            dict(w=wq, tr=False, rot=8, scale=HD_B ** -0.5, outs=[(BF16, False)]),
            dict(w=wk.T, tr=True, rot=8, scale=1.0, outs=[(F32, False), (BF16, False)]),
            dict(w=wv.T, tr=True, rot=0, scale=1.0, outs=[(F32, False)]),
            dict(w=wv, tr=False, rot=0, scale=1.0, outs=[(BF16, True)]),
            dict(w=wqi, tr=False, rot=16, scale=1.0, outs=[(BF16, False)]),
            dict(w=wki.T, tr=True, rot=16, scale=1.0, outs=[(F32, False), (BF16, False)]),
            dict(w=wwi, tr=False, rot=0, scale=IDX_SCALE, outs=[(F32, False)]),
        ]
        q, kt, kt_b, vt, v_b, qi, kit, kit_b, wi = _inproj(x, tabs, segs, seq=seq, tm=tm)
        bias = _dsa_select_prompt(qi, wi, kit_b, nb=nb, seq=seq, n_sel=n_sel)
        o = _dsa_attn_prompt(q, kt_b, v_b, bias, nb=nb, seq=seq)
        tr4 = lambda a, h: a.reshape(nb, h, HEAD, seq).transpose(0, 3, 1, 2)
        outs["b_k"] = tr4(kt, KV_B)
        outs["b_v"] = tr4(vt, KV_B)
        outs["b_ki"] = kit.transpose(0, 2, 1)
    else:
        n_past = page_table.shape[1] * LANES
        n_sel = min(TOPK_MAX, (n_past + seq) // 4)
        segs = [
            dict(w=wq, tr=False, rot=8, scale=HD_B ** -0.5, outs=[(F32, False)]),
            dict(w=wk, tr=False, rot=8, scale=1.0, outs=[(F32, False)]),
            dict(w=wv, tr=False, rot=0, scale=1.0, outs=[(F32, False)]),
            dict(w=wqi, tr=False, rot=16, scale=1.0, outs=[(F32, False)]),
            dict(w=jnp.pad(wki, ((0, 0), (0, LANES - D_I))), tr=False, rot=16, scale=1.0, outs=[(F32, False)]),
            dict(w=wwi, tr=False, rot=0, scale=IDX_SCALE, outs=[(F32, False)]),
        ]
        q, k, v, qi, ki, wi = _inproj(x, tabs, segs, seq=seq, tm=tm)
        ki = ki[:, :D_I]
        bias = _dsa_select_decode(qi, wi, past["b_kit"], page_table, _pad_page_t(ki, nb, seq), s_new=seq, n_sel=n_sel)
        o = _dsa_attn_decode(q, bias, past["b_kt"], past["b_vt"], page_table, _pad_page_t(k, nb, seq),
                             _pad_page_t(v, nb, seq), s_new=seq)
        outs["b_k"] = k.reshape(nb, seq, KV_B, HD_B)
        outs["b_v"] = v.reshape(nb, seq, KV_B, HD_B)
        outs["b_ki"] = ki.reshape(nb, seq, D_I)
    x = _layer_tail(o, w["b_w_out"], x, p[1], w, 1, sort_tokens=prompt, tm=tm)
    return x.reshape(nb, seq, d), outs


def _prep_weights(a_w_in, a_w_out, a_lam_q1, a_lam_k1, a_lam_q2, a_lam_k2, a_subln, b_w_in, b_w_out, ln_g, ln_b,
                  router_w, router_b, moe_w_gate, moe_w_up, moe_w_down, ple_proj, ple_gate_w, ple_gate_b):
    rwt = router_w.T
    rwt_hi = rwt.astype(BF16)
    rw = jnp.pad(router_w, ((0, 0), (0, LANES - N_EXPERTS)))
    rw_hi = rw.astype(BF16)
    return {
        "a_w_in": a_w_in[0].astype(BF16), "a_w_out": a_w_out[0].astype(BF16),
        "a_lam_q1": a_lam_q1, "a_lam_k1": a_lam_k1, "a_lam_q2": a_lam_q2, "a_lam_k2": a_lam_k2,
        "a_subln": a_subln,
        "b_w_in": b_w_in[0].astype(BF16), "b_w_out": b_w_out[0].astype(BF16),
        "ln_g": ln_g[:, :, None, :], "ln_b": ln_b[:, :, None, :],
        "rwt_hi": rwt_hi, "rwt_lo": (rwt - rwt_hi.astype(F32)).astype(BF16), "rb": router_b.reshape(N_EXPERTS, 1),
        "rw_hi": rw_hi, "rw_lo": (rw - rw_hi.astype(F32)).astype(BF16),
        "moe_w_gate": moe_w_gate.astype(BF16), "moe_w_up": moe_w_up.astype(BF16),
        "moe_w_down": moe_w_down.astype(BF16),
        "ple_proj": ple_proj.astype(BF16), "ple_gate_w": ple_gate_w.astype(BF16),
        "ple_gate_b": ple_gate_b[:, None, :],
    }


def kernel(x_prompt, x_sample, cache_a_k, cache_a_v, cache_b_k, cache_b_v, cache_b_kidx, page_table, p_prompt,
           p_sample, a_w_in, a_w_out, a_lam_q1, a_lam_k1, a_lam_q2, a_lam_k2, a_subln, b_w_in, b_w_out, ln_g, ln_b,
           router_w, router_b, moe_w_gate, moe_w_up, moe_w_down, ple_proj, ple_gate_w, ple_gate_b):
    w = _prep_weights(a_w_in, a_w_out, a_lam_q1, a_lam_k1, a_lam_q2, a_lam_k2, a_subln, b_w_in, b_w_out, ln_g, ln_b,
                      router_w, router_b, moe_w_gate, moe_w_up, moe_w_down, ple_proj, ple_gate_w, ple_gate_b)
    n_pool, page = cache_a_k.shape[1], cache_a_k.shape[2]
    past_len = page_table.shape[1] * page
    past = {
        "a_kt": cache_a_k[0].transpose(0, 2, 3, 1).reshape(n_pool, 2 * H_A * HD_A, page),
        "a_v": cache_a_v[0].reshape(n_pool, page * H_A, 2 * HD_A),
        "b_kt": cache_b_k[0].transpose(0, 2, 3, 1).reshape(n_pool, KV_B * HD_B, page),
        "b_vt": cache_b_v[0].transpose(0, 2, 3, 1).reshape(n_pool, KV_B * HD_B, page),
        "b_kit": cache_b_kidx[0].transpose(0, 2, 1),
    }
    y_p, op = _trunk(x_prompt, p_prompt, 0, None, None, w)
    y_s, os_ = _trunk(x_sample, p_sample, past_len, past, page_table, w)
    lead = lambda a: a[None]
    return (y_p, y_s,
            lead(op["a_k"]), lead(op["a_v"]), lead(op["b_k"]), lead(op["b_v"]), lead(op["b_ki"]),
            lead(os_["a_k"]), lead(os_["a_v"]), lead(os_["b_k"]), lead(os_["b_v"]), lead(os_["b_ki"]))
```

```python
import functools
import math

import jax
import jax.numpy as jnp
from jax import lax
from jax.experimental import pallas as pl
from jax.experimental.pallas import tpu as pltpu

F32 = jnp.float32
BF16 = jnp.bfloat16
I32 = jnp.int32

D_MODEL = 1024
DEPTH = 2
H_A = 8
HD_A = 64
H_B = 16
KV_B = 4
HD_B = 64
H_I = 8
D_I = 64
TOPK_MAX = 256
ROPE_THETA = 500000.0
N_EXPERTS = 16
N_GROUPS = 4
E_PER_GROUP = 4
D_FF = 512
ALPHA = (2 * DEPTH) ** 0.25
LN_EPS = 1e-5
RMS_EPS = 1e-5
IDX_SCALE = (H_I ** -0.5) * (D_I ** -0.5)

LANES = 128
HEAD = 64
VMEM_LIMIT = 56 * 1024 * 1024
INT_MIN = -2 ** 31
NEG_BIG = -1e30


def _params(sem):
    return pltpu.CompilerParams(dimension_semantics=sem, vmem_limit_bytes=VMEM_LIMIT)


def _dot(a, b):
    return jnp.dot(a, b, preferred_element_type=F32)


def _dot_nt(a, b):
    return lax.dot_general(a, b, (((1,), (1,)), ((), ())), preferred_element_type=F32)


def _div_pow2(x, n):
    assert n & (n - 1) == 0, n
    return x >> (n.bit_length() - 1)


def _layer_norm(y, g, b):
    mu = jnp.mean(y, axis=-1, keepdims=True)
    var = jnp.mean(jnp.square(y - mu), axis=-1, keepdims=True)
    return (y - mu) * lax.rsqrt(var + LN_EPS) * g + b


def _sigmoid(x):
    return 1.0 / (1.0 + jnp.exp(-x))


def _rope_tables_kernel(inv8l, inv16l, inv8c, inv16c, tok8, tok16, tr8, tr16, *, seq, off, tp):
    base = pl.program_id(0) * tp
    row = base + lax.broadcasted_iota(I32, (tp, LANES), 0)
    lane = lax.broadcasted_iota(I32, (tp, LANES), 1)
    pos = (off + (row & (seq - 1))).astype(F32)
    d = lane & (HEAD - 1)
    for half, inv, out in ((8, inv8l, tok8), (16, inv16l, tok16)):
        ang = pos * inv[...]
        c = jnp.cos(ang)
        s = jnp.sin(ang)
        out[0] = jnp.where(d < 2 * half, c, 1.0)
        out[1] = jnp.where(d < half, -s, 0.0)
        out[2] = jnp.where((d >= half) & (d < 2 * half), s, 0.0)
    col = base + lax.broadcasted_iota(I32, (1, tp), 1)
    posr = (off + (col & (seq - 1))).astype(F32)
    for inv, out in ((inv8c, tr8), (inv16c, tr16)):
        ang = posr * inv[...]
        out[0] = jnp.cos(ang)
        out[1] = jnp.sin(ang)


def _rope_tables(n_pos, seq, off):
    tp = min(n_pos, 256)
    lane = jnp.arange(LANES) % HEAD
    invs = []
    for half in (8, 16):
        inv = ROPE_THETA ** (-jnp.arange(half, dtype=F32) / half)
        invs.append((jnp.where(lane < 2 * half, inv[lane % half], 0.0).reshape(1, LANES), inv.reshape(half, 1)))
    full = lambda shape: pl.BlockSpec(shape, lambda i: (0,) * len(shape))
    return pl.pallas_call(
        functools.partial(_rope_tables_kernel, seq=seq, off=off, tp=tp),
        grid=(n_pos // tp,),
        in_specs=[full((1, LANES)), full((1, LANES)), full((8, 1)), full((16, 1))],
        out_specs=[
            pl.BlockSpec((3, tp, LANES), lambda i: (0, i, 0)),
            pl.BlockSpec((3, tp, LANES), lambda i: (0, i, 0)),
            pl.BlockSpec((2, 8, tp), lambda i: (0, 0, i)),
            pl.BlockSpec((2, 16, tp), lambda i: (0, 0, i)),
        ],
        out_shape=[
            jax.ShapeDtypeStruct((3, n_pos, LANES), F32),
            jax.ShapeDtypeStruct((3, n_pos, LANES), F32),
            jax.ShapeDtypeStruct((2, 8, n_pos), F32),
            jax.ShapeDtypeStruct((2, 16, n_pos), F32),
        ],
        compiler_params=_params(("arbitrary",)),
        name="rope_tables",
    )(invs[0][0], invs[1][0], invs[0][1], invs[1][1])


def _rope_tok(y, tab, half):
    c, a, b = tab[0], tab[1], tab[2]
    outs = []
    for k in range(y.shape[1] // LANES):
        yc = y[:, k * LANES:(k + 1) * LANES]
        outs.append(yc * c + pltpu.roll(yc, LANES - half, 1) * a + pltpu.roll(yc, half, 1) * b)
    return outs[0] if len(outs) == 1 else jnp.concatenate(outs, axis=1)


def _rope_tr(yt, tr, half):
    cos, sin = tr[0], tr[1]
    parts = []
    for h in range(yt.shape[0] // HEAD):
        b = h * HEAD
        x1 = yt[b:b + half]
        x2 = yt[b + half:b + 2 * half]
        parts += [x1 * cos - x2 * sin, x2 * cos + x1 * sin, yt[b + 2 * half:b + HEAD]]
    return jnp.concatenate(parts, axis=0)


def _inproj_kernel(*refs, segs):
    x_ref, tok8, tok16, tr8, tr16 = refs[:5]
    w_refs = refs[5:5 + len(segs)]
    out_refs = list(refs[5 + len(segs):])
    xb = x_ref[...].astype(BF16)
    for seg, w_ref in zip(segs, w_refs):
        if seg["tr"]:
            y = _dot_nt(w_ref[...], xb)
            if seg["rot"]:
                y = _rope_tr(y, tr8 if seg["rot"] == 8 else tr16, seg["rot"])
        else:
            y = _dot(xb, w_ref[...])
            if seg["rot"]:
                y = _rope_tok(y, tok8 if seg["rot"] == 8 else tok16, seg["rot"])
        if seg["scale"] != 1.0:
            y = y * seg["scale"]
        for dt, split in seg["outs"]:
            o_ref = out_refs.pop(0)
            if split:
                for g in range(y.shape[1] // HEAD):
                    o_ref[g] = y[:, g * HEAD:(g + 1) * HEAD].astype(dt)
            else:
                o_ref[...] = y.astype(dt)


INPROJ_TILE = 512


def _inproj(x, tabs, segs, *, seq, tm):
    t, k = x.shape
    nb = t // seq
    tps = max(seq // tm, 1)
    n_tab = tabs[0].shape[1] // tm
    in_specs = [
        pl.BlockSpec((tm, k), lambda i: (i, 0)),
        pl.BlockSpec((3, tm, LANES), lambda i: (0, i % n_tab, 0)),
        pl.BlockSpec((3, tm, LANES), lambda i: (0, i % n_tab, 0)),
        pl.BlockSpec((2, 8, tm), lambda i: (0, 0, i % n_tab)),
        pl.BlockSpec((2, 16, tm), lambda i: (0, 0, i % n_tab)),
    ]
    out_specs, out_shape = [], []
    for seg in segs:
        w = seg["w"]
        in_specs.append(pl.BlockSpec(w.shape, lambda i: (0, 0)))
        n = w.shape[0] if seg["tr"] else w.shape[1]
        for dt, split in seg["outs"]:
            if seg["tr"]:
                out_specs.append(pl.BlockSpec((None, n, tm), lambda i: (i // tps, 0, i % tps)))
                out_shape.append(jax.ShapeDtypeStruct((nb, n, seq), dt))
            elif split:
                out_specs.append(pl.BlockSpec((n // HEAD, tm, HEAD), lambda i: (0, i, 0)))
                out_shape.append(jax.ShapeDtypeStruct((n // HEAD, t, HEAD), dt))
            else:
                out_specs.append(pl.BlockSpec((tm, n), lambda i: (i, 0)))
                out_shape.append(jax.ShapeDtypeStruct((t, n), dt))
    kern_segs = tuple({k2: v for k2, v in seg.items() if k2 != "w"} for seg in segs)
    return pl.pallas_call(
        functools.partial(_inproj_kernel, segs=kern_segs),
        grid=(t // tm,),
        in_specs=in_specs,
        out_specs=out_specs,
        out_shape=out_shape,
        compiler_params=_params(("arbitrary",)),
        name="inproj",
    )(x, *tabs, *[seg["w"] for seg in segs])


def _lambda(lq1, lk1, lq2, lk2, lam_init):
    return (jnp.exp(jnp.sum(lq1[...] * lk1[...], axis=1, keepdims=True))
            - jnp.exp(jnp.sum(lq2[...] * lk2[...], axis=1, keepdims=True)) + lam_init)


def _sub_norm(o, sub, lam_init):
    return o * lax.rsqrt(jnp.mean(o * o, axis=-1, keepdims=True) + RMS_EPS) * sub * (1.0 - lam_init)


def _diff_attn_kernel(q_ref, kt_ref, v_ref, lq1, lk1, lq2, lk2, sub_ref, o_ref, *, tq, tk, lam_init):
    qi = pl.program_id(2)
    q = q_ref[...]
    ratio = tq // tk

    def step(j, carry, masked):
        off = pl.multiple_of(j * tk, tk)
        v = v_ref[pl.ds(off, tk), :]
        new = []
        for c in range(2):
            m, l, acc = carry[c]
            kt = kt_ref[c * HEAD:(c + 1) * HEAD, pl.ds(off, tk)]
            s = _dot(q[:, c * HEAD:(c + 1) * HEAD], kt)
            if masked:
                row = lax.broadcasted_iota(I32, (tq, tk), 0)
                col = lax.broadcasted_iota(I32, (tq, tk), 1)
                s = jnp.where(col <= row + (qi * tq - j * tk), s, -jnp.inf)
            m_new = jnp.maximum(m, jnp.max(s, axis=1, keepdims=True))
            alpha = jnp.exp(m - m_new)
            p = jnp.exp(s - m_new)
            l = alpha * l + jnp.sum(p, axis=1, keepdims=True)
            acc = alpha * acc + _dot(p.astype(BF16), v)
            new.append((m_new, l, acc))
        return tuple(new)

    init = tuple((jnp.full((tq, 1), -jnp.inf, F32), jnp.zeros((tq, 1), F32), jnp.zeros((tq, 2 * HEAD), F32))
                 for _ in range(2))
    carry = lax.fori_loop(0, qi * ratio, functools.partial(step, masked=False), init)
    if ratio == 1:
        carry = step(qi, carry, True)
    else:
        carry = lax.fori_loop(qi * ratio, (qi + 1) * ratio, functools.partial(step, masked=True), carry)
    (_, l0, a0), (_, l1, a1) = carry
    lam = _lambda(lq1, lk1, lq2, lk2, lam_init)
    o = a0 / l0 - lam * (a1 / l1)
    o_ref[...] = _sub_norm(o, sub_ref[...], lam_init).astype(o_ref.dtype)


DIFF_TQ = 1024
DIFF_TK = 1024


def _diff_attn_prompt(q, kt, v, lams, sub, *, nb, seq, lam_init):
    tq = min(seq, DIFF_TQ)
    tk = min(seq, DIFF_TK)
    nq = seq // tq
    small = lambda shape: pl.BlockSpec(shape, lambda b, h, i: (0, 0))
    return pl.pallas_call(
        functools.partial(_diff_attn_kernel, tq=tq, tk=tk, lam_init=lam_init),
        grid=(nb, H_A, nq),
        in_specs=[
            pl.BlockSpec((tq, 2 * HEAD), lambda b, h, i: (b * nq + i, h)),
            pl.BlockSpec((None, 2 * HEAD, seq), lambda b, h, i: (b, h, 0)),
            pl.BlockSpec((seq, 2 * HEAD), lambda b, h, i: (b, h)),
            small((1, HEAD)), small((1, HEAD)), small((1, HEAD)), small((1, HEAD)), small((1, 2 * HEAD)),
        ],
        out_specs=pl.BlockSpec((tq, 2 * HEAD), lambda b, h, i: (b * nq + i, h)),
        out_shape=jax.ShapeDtypeStruct(q.shape, BF16),
        compiler_params=_params(("arbitrary", "arbitrary", "arbitrary")),
        name="diff_attn_prompt",
    )(q, kt, v, *lams, sub)


PAGES_PER_STEP = 8
DSA_PAGES_PER_STEP = 16


def _block_diag_q(q, n_heads, width):
    s = q.shape[0]
    rows = jnp.concatenate([q] * n_heads, axis=0)
    r = _div_pow2(lax.broadcasted_iota(I32, rows.shape, 0), s)
    c = _div_pow2(lax.broadcasted_iota(I32, rows.shape, 1), HEAD)
    return jnp.where(r == c, rows, 0.0)


def _diff_attn_dec_kernel(pt_ref, q_ref, *refs, g, s_new, n_steps, lam_init):
    kt_refs, v_refs = refs[:g], refs[g:2 * g]
    ktn_ref, vn_ref, lq1, lk1, lq2, lk2, sub_ref, o_ref, qbd_ref, m_ref, l_ref, acc_ref = refs[2 * g:]
    step = pl.program_id(1)
    nr = 2 * H_A * s_new

    @pl.when(step == 0)
    def _():
        qbd_ref[...] = _block_diag_q(q_ref[...], 2 * H_A, D_MODEL).astype(BF16)
        m_ref[...] = jnp.full(m_ref.shape, -jnp.inf, F32)
        l_ref[...] = jnp.zeros(l_ref.shape, F32)
        acc_ref[...] = jnp.zeros(acc_ref.shape, F32)

    def consume(kts, page_v_refs, mask):
        kt = kts[0] if len(kts) == 1 else jnp.concatenate(kts, axis=1)
        s = _dot(qbd_ref[...], kt.astype(BF16))
        if mask is not None:
            s = jnp.where(mask, s, -jnp.inf)
        m = m_ref[...]
        m_new = jnp.maximum(m, jnp.max(s, axis=1, keepdims=True))
        alpha = jnp.exp(m - m_new)
        p = jnp.exp(s - m_new)
        l_ref[...] = alpha * l_ref[...] + jnp.sum(p, axis=1, keepdims=True)
        m_ref[...] = m_new
        pb = p.astype(BF16)
        pvs = []
        for h in range(H_A):
            vs = [v_ref[pl.ds(h, LANES, stride=H_A), :] for v_ref in page_v_refs]
            v = vs[0] if len(vs) == 1 else jnp.concatenate(vs, axis=0)
            pvs.append(_dot(pb[2 * s_new * h:2 * s_new * (h + 1)], v.astype(BF16)))
        acc_ref[...] = alpha * acc_ref[...] + jnp.concatenate(pvs, axis=0)

    @pl.when(step < n_steps - 1)
    def _():
        consume([kt_ref[...] for kt_ref in kt_refs], v_refs, None)

    @pl.when(step == n_steps - 1)
    def _():
        key = lax.broadcasted_iota(I32, (nr, LANES), 1)
        tok = lax.broadcasted_iota(I32, (nr, LANES), 0) & (s_new - 1)
        consume([ktn_ref[...]], [vn_ref], key <= tok)
        lam = _lambda(lq1, lk1, lq2, lk2, lam_init)
        o = acc_ref[...] / l_ref[...]
        outs = []
        for h in range(H_A):
            b = 2 * s_new * h
            outs.append(_sub_norm(o[b:b + s_new] - lam * o[b + s_new:b + 2 * s_new], sub_ref[...], lam_init))
        o_ref[...] = jnp.concatenate(outs, axis=1)


def _diff_attn_decode(q, kt_pool, v_pool, page_table, kt_new, v_new, lams, sub, *, s_new, lam_init):
    nb, n_pages = page_table.shape
    g = min(PAGES_PER_STEP, n_pages)
    assert g % 2 == 0 and n_pages % g == 0
    n_steps = n_pages // g + 1
    nr = 2 * H_A * s_new

    def page_map(k):
        return lambda b, i, pt: (pt[b, jnp.minimum(i, n_steps - 2) * g + k], 0, 0)

    page = lambda k: pl.BlockSpec((None, D_MODEL, LANES), page_map(k))
    new = pl.BlockSpec((None, D_MODEL, LANES), lambda b, i, pt: (b, 0, 0))
    small = lambda shape: pl.BlockSpec(shape, lambda b, i, pt: (0, 0))
    grid_spec = pltpu.PrefetchScalarGridSpec(
        num_scalar_prefetch=1,
        grid=(nb, n_steps),
        in_specs=[pl.BlockSpec((s_new, D_MODEL), lambda b, i, pt: (b, 0))]
        + [page(k) for k in range(g)] + [page(k) for k in range(g)] + [new, new]
        + [small((1, HEAD))] * 4 + [small((1, 2 * HEAD))],
        out_specs=pl.BlockSpec((s_new, D_MODEL), lambda b, i, pt: (b, 0)),
        scratch_shapes=[
            pltpu.VMEM((nr, D_MODEL), BF16),
            pltpu.VMEM((nr, 1), F32),
            pltpu.VMEM((nr, 1), F32),
            pltpu.VMEM((nr, 2 * HEAD), F32),
        ],
    )
    return pl.pallas_call(
        functools.partial(_diff_attn_dec_kernel, g=g, s_new=s_new, n_steps=n_steps, lam_init=lam_init),
        grid_spec=grid_spec,
        out_shape=jax.ShapeDtypeStruct(q.shape, F32),
        compiler_params=_params(("arbitrary", "arbitrary")),
        name="diff_attn_decode",
    )(page_table, q, *([kt_pool] * g), *([v_pool] * g), kt_new, v_new, *lams, sub)


def _float_of_rank(u):
    key = u ^ INT_MIN
    bits = key ^ ((key >> 31) & 0x7FFFFFFF)
    return lax.bitcast_convert_type(bits, F32)


def _count(pred):
    return jnp.sum(pred.astype(I32), axis=1, keepdims=True)


def _write_bias(store, score, valid, col, n_valid, n_sel, idx_bits):
    score = jnp.where(valid, score, jnp.nan)

    def body(i, t_u):
        cand = t_u | lax.shift_left(jnp.int32(1), 31 - i)
        return jnp.where(_count(score >= _float_of_rank(cand)) >= n_sel, cand, t_u)

    t = _float_of_rank(lax.fori_loop(0, 32, body, jnp.zeros((score.shape[0], 1), I32)))
    keep_all = n_valid <= n_sel
    t = jnp.where(keep_all, -jnp.inf, t)
    tie_rows = jnp.logical_not(keep_all) & (_count(score >= t) > n_sel)
    store(jnp.where(score >= t, 0.0, NEG_BIG))

    @pl.when(jnp.max(tie_rows.astype(I32)) > 0)
    def _():
        need = n_sel - _count(score > t)
        eq = score == t

        def idx_body(i, c):
            cand = c | lax.shift_left(jnp.int32(1), idx_bits - 1 - i)
            return jnp.where(_count(eq & (col < cand)) <= need - 1, cand, c)

        c = jnp.where(tie_rows, lax.fori_loop(0, idx_bits, idx_body, jnp.zeros_like(need)), 2 ** 30)
        store(jnp.where((score > t) | (eq & (col <= c)), 0.0, NEG_BIG))


SELECT_WIDTH_STEP = 512
SCORE_CHUNK = 256


def _dsa_select_kernel(qi_ref, w_ref, kit_ref, bias_ref, *, tq, n_sel, wstep):
    seq = kit_ref.shape[1]
    j = pl.program_id(1)
    variant = lax.shift_right_logical((j + 1) * tq + wstep - 1, wstep.bit_length() - 1) - 1

    def run(width):
        qi = qi_ref[...]
        w = w_ref[...]
        chunks = []
        for c in range(width // SCORE_CHUNK):
            kit = kit_ref[:, c * SCORE_CHUNK:(c + 1) * SCORE_CHUNK]
            sc = w[:, 0:1] * jnp.maximum(_dot(qi[:, 0:HEAD], kit), 0.0)
            for h in range(1, H_I):
                sc = sc + w[:, h:h + 1] * jnp.maximum(_dot(qi[:, h * HEAD:(h + 1) * HEAD], kit), 0.0)
            chunks.append(sc)
        score = chunks[0] if len(chunks) == 1 else jnp.concatenate(chunks, axis=1)
        row = j * tq + lax.broadcasted_iota(I32, (tq, width), 0)
        col = lax.broadcasted_iota(I32, (tq, width), 1)

        def store(bias):
            bias_ref[:, :width] = bias.astype(bias_ref.dtype)

        _write_bias(store, score, col <= row, col, row[:, :1] + 1, n_sel, width.bit_length())
        if width < seq:
            bias_ref[:, width:] = jnp.full((tq, seq - width), NEG_BIG, bias_ref.dtype)

    for k in range(seq // wstep):
        pl.when(variant == k)(functools.partial(run, (k + 1) * wstep))


def _dsa_select_prompt(qi, w, kit, *, nb, seq, n_sel):
    tq = min(seq, SELECT_WIDTH_STEP)
    nq = seq // tq
    return pl.pallas_call(
        functools.partial(_dsa_select_kernel, tq=tq, n_sel=n_sel, wstep=min(seq, SELECT_WIDTH_STEP)),
        grid=(nb, nq),
        in_specs=[
            pl.BlockSpec((tq, H_I * HEAD), lambda b, i: (b * nq + i, 0)),
            pl.BlockSpec((tq, LANES), lambda b, i: (b * nq + i, 0)),
            pl.BlockSpec((None, HEAD, seq), lambda b, i: (b, 0, 0)),
        ],
        out_specs=pl.BlockSpec((None, tq, seq), lambda b, i: (b, i, 0)),
        out_shape=jax.ShapeDtypeStruct((nb, seq, seq), BF16),
        compiler_params=_params(("arbitrary", "arbitrary")),
        name="dsa_select_prompt",
    )(qi, w, kit)


def _dsa_score_dec_kernel(pt_ref, qi_ref, w_ref, *refs, g, s_new, n_steps, n_past):
    kit_refs = refs[:g]
    kitn_ref, score_ref, qs_ref, ws_ref = refs[g:]
    step = pl.program_id(1)

    @pl.when(step == 0)
    def _():
        qi = qi_ref[...]
        w = w_ref[...]
        qs_ref[...] = jnp.concatenate([qi[:, h * HEAD:(h + 1) * HEAD] for h in range(H_I)], axis=0).astype(BF16)
        ws_ref[...] = jnp.concatenate([w[:, h:h + 1] for h in range(H_I)], axis=0)

    def page_score(kit):
        d = jnp.maximum(_dot(qs_ref[...], kit.astype(BF16)), 0.0) * ws_ref[...]
        sc = d[0:s_new]
        for h in range(1, H_I):
            sc = sc + d[h * s_new:(h + 1) * s_new]
        return sc

    @pl.when(step < n_steps - 1)
    def _():
        off = pl.multiple_of(step * (g * LANES), g * LANES)
        score_ref[:, pl.ds(off, g * LANES)] = page_score(jnp.concatenate([r[...] for r in kit_refs], axis=1))

    @pl.when(step == n_steps - 1)
    def _():
        score_ref[:, n_past:n_past + LANES] = page_score(kitn_ref[...])


def _dsa_select_rows_kernel(score_ref, bias_ref, *, s_new, n_sel, n_past):
    rows, width = score_ref.shape
    col = lax.broadcasted_iota(I32, (rows, width), 1)
    tok = lax.broadcasted_iota(I32, (rows, width), 0) & (s_new - 1)

    def store(bias):
        bias_ref[...] = bias

    _write_bias(store, score_ref[...], col <= n_past + tok, col, n_past + tok[:, :1] + 1, n_sel, width.bit_length())


SELECT_PAGES_PER_STEP = 32


def _dsa_select_decode(qi, w, kit_pool, page_table, kit_new, *, s_new, n_sel):
    nb, n_pages = page_table.shape
    g = min(SELECT_PAGES_PER_STEP, n_pages)
    n_steps = n_pages // g + 1
    n_past = n_pages * LANES
    width = n_past + LANES

    def page_map(k):
        return lambda b, i, pt: (pt[b, jnp.minimum(i, n_steps - 2) * g + k], 0, 0)

    grid_spec = pltpu.PrefetchScalarGridSpec(
        num_scalar_prefetch=1,
        grid=(nb, n_steps),
        in_specs=[pl.BlockSpec((s_new, H_I * HEAD), lambda b, i, pt: (b, 0)),
                  pl.BlockSpec((s_new, LANES), lambda b, i, pt: (b, 0))]
        + [pl.BlockSpec((None, HEAD, LANES), page_map(k)) for k in range(g)]
        + [pl.BlockSpec((None, HEAD, LANES), lambda b, i, pt: (b, 0, 0))],
        out_specs=pl.BlockSpec((None, s_new, width), lambda b, i, pt: (b, 0, 0)),
        scratch_shapes=[
            pltpu.VMEM((H_I * s_new, HEAD), BF16),
            pltpu.VMEM((H_I * s_new, 1), F32),
        ],
    )
    score = pl.pallas_call(
        functools.partial(_dsa_score_dec_kernel, g=g, s_new=s_new, n_steps=n_steps, n_past=n_past),
        grid_spec=grid_spec,
        out_shape=jax.ShapeDtypeStruct((nb, s_new, width), F32),
        compiler_params=_params(("arbitrary", "arbitrary")),
        name="dsa_score_decode",
    )(page_table, qi, w, *([kit_pool] * g), kit_new)
    rows = nb * s_new
    tr = min(rows, LANES)
    bias = pl.pallas_call(
        functools.partial(_dsa_select_rows_kernel, s_new=s_new, n_sel=n_sel, n_past=n_past),
        grid=(rows // tr,),
        in_specs=[pl.BlockSpec((tr, width), lambda i: (i, 0))],
        out_specs=pl.BlockSpec((tr, width), lambda i: (i, 0)),
        out_shape=jax.ShapeDtypeStruct((rows, width), F32),
        compiler_params=_params(("arbitrary",)),
        name="dsa_select_decode",
    )(score.reshape(rows, width))
    return bias.reshape(nb, s_new, width)


def _dsa_attn_kernel(q_ref, kt_ref, v_ref, bias_ref, o_ref, *, tq, tk):
    grp = H_B // KV_B
    qi = pl.program_id(2)
    qb = q_ref[...]
    q4 = jnp.concatenate([qb[:, h * HEAD:(h + 1) * HEAD] for h in range(grp)], axis=0)
    n_kv = lax.shift_right_logical((qi + 1) * tq + tk - 1, tk.bit_length() - 1)

    def step(j, carry):
        m, l, acc = carry
        off = pl.multiple_of(j * tk, tk)
        s = _dot(q4, kt_ref[:, pl.ds(off, tk)])
        s = (s.reshape(grp, tq, tk) + bias_ref[:, pl.ds(off, tk)].astype(F32)[None]).reshape(grp * tq, tk)
        m_new = jnp.maximum(m, jnp.max(s, axis=1, keepdims=True))
        alpha = jnp.exp(m - m_new)
        p = jnp.exp(s - m_new)
        l = alpha * l + jnp.sum(p, axis=1, keepdims=True)
        acc = alpha * acc + _dot(p.astype(BF16), v_ref[pl.ds(off, tk), :])
        return m_new, l, acc

    init = (jnp.full((grp * tq, 1), -jnp.inf, F32), jnp.zeros((grp * tq, 1), F32), jnp.zeros((grp * tq, HEAD), F32))
    _, l, acc = lax.fori_loop(0, n_kv, step, init)
    o = acc / l
    o_ref[...] = jnp.concatenate([o[h * tq:(h + 1) * tq] for h in range(grp)], axis=1).astype(o_ref.dtype)


DSA_TQ = 256
DSA_TK = 512


def _dsa_attn_prompt(q, kt, v, bias, *, nb, seq):
    tq = min(seq, DSA_TQ)
    tk = min(seq, DSA_TK)
    nq = seq // tq
    grp = H_B // KV_B
    return pl.pallas_call(
        functools.partial(_dsa_attn_kernel, tq=tq, tk=tk),
        grid=(nb, KV_B, nq),
        in_specs=[
            pl.BlockSpec((tq, grp * HEAD), lambda b, g, i: (b * nq + i, g)),
            pl.BlockSpec((None, HEAD, seq), lambda b, g, i: (b, g, 0)),
            pl.BlockSpec((None, seq, HEAD), lambda b, g, i: (g, b, 0)),
            pl.BlockSpec((None, tq, seq), lambda b, g, i: (b, i, 0)),
        ],
        out_specs=pl.BlockSpec((tq, grp * HEAD), lambda b, g, i: (b * nq + i, g)),
        out_shape=jax.ShapeDtypeStruct(q.shape, BF16),
        compiler_params=_params(("arbitrary", "arbitrary", "arbitrary")),
        name="dsa_attn_prompt",
    )(q, kt, v, bias)


def _dsa_attn_dec_kernel(pt_ref, q_ref, bias_ref, *refs, g, s_new, n_steps, n_past):
    kt_refs, vt_refs = refs[:g], refs[g:2 * g]
    ktn_ref, vtn_ref, o_ref, qbd_ref, m_ref, l_ref, acc_ref = refs[2 * g:]
    step = pl.program_id(1)
    nr = H_B * s_new
    grp = H_B // KV_B

    @pl.when(step == 0)
    def _():
        q = q_ref[...]
        rows = jnp.concatenate([q[:, h * HEAD:(h + 1) * HEAD] for h in range(H_B)], axis=0)
        wide = jnp.concatenate([rows] * KV_B, axis=1)
        r = _div_pow2(lax.broadcasted_iota(I32, wide.shape, 0), grp * s_new)
        c = _div_pow2(lax.broadcasted_iota(I32, wide.shape, 1), HEAD)
        qbd_ref[...] = jnp.where(r == c, wide, 0.0).astype(BF16)
        m_ref[...] = jnp.full(m_ref.shape, -jnp.inf, F32)
        l_ref[...] = jnp.zeros(l_ref.shape, F32)
        acc_ref[...] = jnp.zeros(acc_ref.shape, F32)

    def consume(ch, kt, vt, bias):
        keys = kt.shape[1]
        s = _dot(qbd_ref[...], kt.astype(BF16))
        s = (s.reshape(H_B, s_new, keys) + bias[None]).reshape(nr, keys)
        m = m_ref[ch]
        m_new = jnp.maximum(m, jnp.max(s, axis=1, keepdims=True))
        alpha = jnp.exp(m - m_new)
        p = jnp.exp(s - m_new)
        l_ref[ch] = alpha * l_ref[ch] + jnp.sum(p, axis=1, keepdims=True)
        m_ref[ch] = m_new
        acc_ref[ch] = alpha * acc_ref[ch] + _dot_nt(p.astype(BF16), vt.astype(BF16))

    @pl.when(step < n_steps - 1)
    def _():
        half = g // 2
        for ch in range(2):
            off = pl.multiple_of(step * (g * LANES) + ch * (half * LANES), half * LANES)
            pages = slice(ch * half, (ch + 1) * half)
            consume(ch, jnp.concatenate([r[...] for r in kt_refs[pages]], axis=1),
                    jnp.concatenate([r[...] for r in vt_refs[pages]], axis=1), bias_ref[:, pl.ds(off, half * LANES)])

    @pl.when(step == n_steps - 1)
    def _():
        consume(0, ktn_ref[...], vtn_ref[...], bias_ref[:, n_past:n_past + LANES])
        m = jnp.maximum(m_ref[0], m_ref[1])
        a0, a1 = jnp.exp(m_ref[0] - m), jnp.exp(m_ref[1] - m)
        o = (a0 * acc_ref[0] + a1 * acc_ref[1]) / (a0 * l_ref[0] + a1 * l_ref[1])
        outs = []
        for h in range(H_B):
            kv = h // grp
            outs.append(o[h * s_new:(h + 1) * s_new, kv * HEAD:(kv + 1) * HEAD])
        o_ref[...] = jnp.concatenate(outs, axis=1)


def _dsa_attn_decode(q, bias, kt_pool, vt_pool, page_table, kt_new, vt_new, *, s_new):
    nb, n_pages = page_table.shape
    g = min(DSA_PAGES_PER_STEP, n_pages)
    assert g % 2 == 0 and n_pages % g == 0
    n_steps = n_pages // g + 1
    n_past = n_pages * LANES
    nr = H_B * s_new
    kvw = KV_B * HEAD

    def page_map(k):
        return lambda b, i, pt: (pt[b, jnp.minimum(i, n_steps - 2) * g + k], 0, 0)

    page = lambda k: pl.BlockSpec((None, kvw, LANES), page_map(k))
    new = pl.BlockSpec((None, kvw, LANES), lambda b, i, pt: (b, 0, 0))
    grid_spec = pltpu.PrefetchScalarGridSpec(
        num_scalar_prefetch=1,
        grid=(nb, n_steps),
        in_specs=[pl.BlockSpec((s_new, D_MODEL), lambda b, i, pt: (b, 0)),
                  pl.BlockSpec((None, s_new, n_past + LANES), lambda b, i, pt: (b, 0, 0))]
        + [page(k) for k in range(g)] + [page(k) for k in range(g)] + [new, new],
        out_specs=pl.BlockSpec((s_new, D_MODEL), lambda b, i, pt: (b, 0)),
        scratch_shapes=[
            pltpu.VMEM((nr, kvw), BF16),
            pltpu.VMEM((2, nr, 1), F32),
            pltpu.VMEM((2, nr, 1), F32),
            pltpu.VMEM((2, nr, kvw), F32),
        ],
    )
    return pl.pallas_call(
        functools.partial(_dsa_attn_dec_kernel, g=g, s_new=s_new, n_steps=n_steps, n_past=n_past),
        grid_spec=grid_spec,
        out_shape=jax.ShapeDtypeStruct(q.shape, F32),
        compiler_params=_params(("arbitrary", "arbitrary")),
        name="dsa_attn_decode",
    )(page_table, q, bias, *([kt_pool] * g), *([vt_pool] * g), kt_new, vt_new)


def _outproj_ln_kernel(o_ref, w_ref, x_ref, g_ref, b_ref, *rest, route):
    h = _dot(o_ref[...].astype(BF16), w_ref[...])
    y = _layer_norm(ALPHA * x_ref[...] + h, g_ref[...], b_ref[...])
    if not route:
        rest[0][...] = y
        return
    rwh_ref, rwl_ref, rb_ref, out_ref, cls_ref = rest
    out_ref[...] = y
    gi, i1, i2, _, _ = _route_choice(*_router_rows(y, rwh_ref[...], rwl_ref[...], rb_ref[...]))
    lo, hi = jnp.minimum(i1, i2), jnp.maximum(i1, i2)
    pair = jnp.where(lo == 0, 0, jnp.where(lo == 1, 3, 5)) + hi - lo - 1
    cls_ref[...] = gi * PAIRS_PER_GROUP + pair


def _outproj_ln(o, w, x, g, b, router=None, *, tm):
    t, d = x.shape
    row = lambda n: pl.BlockSpec((tm, n), lambda i: (i, 0))
    full = lambda shape: pl.BlockSpec(shape, lambda i: (0, 0))
    in_specs = [row(o.shape[1]), full(w.shape), row(d), full((1, d)), full((1, d))]
    out_specs, out_shape = row(d), jax.ShapeDtypeStruct((t, d), F32)
    if router is not None:
        in_specs += [full(r.shape) for r in router]
        out_specs = [out_specs, pl.BlockSpec((1, tm), lambda i: (0, i))]
        out_shape = [out_shape, jax.ShapeDtypeStruct((1, t), I32)]
    return pl.pallas_call(
        functools.partial(_outproj_ln_kernel, route=router is not None),
        grid=(t // tm,),
        in_specs=in_specs,
        out_specs=out_specs,
        out_shape=out_shape,
        compiler_params=_params(("arbitrary",)),
        name="outproj_ln",
    )(o, w, x, g, b, *(router or ()))


def _route_choice(sel, aff):
    gs = []
    for g in range(N_GROUPS):
        a, b, c, d = sel[4 * g:4 * g + 4]
        hi1, lo1, hi2, lo2 = jnp.maximum(a, b), jnp.minimum(a, b), jnp.maximum(c, d), jnp.minimum(c, d)
        gs.append(jnp.maximum(hi1, hi2) + jnp.maximum(jnp.minimum(hi1, hi2), jnp.maximum(lo1, lo2)))
    best, gi = gs[0], jnp.zeros(gs[0].shape, I32)
    for g in range(1, N_GROUPS):
        better = gs[g] > best
        best = jnp.where(better, gs[g], best)
        gi = jnp.where(better, g, gi)

    def pick(rows):
        out = []
        for j in range(E_PER_GROUP):
            v = rows[j]
            for g in range(1, N_GROUPS):
                v = jnp.where(gi == g, rows[4 * g + j], v)
            out.append(v)
        return out

    sv, av = pick(sel), pick(aff)

    def argmax_first(vals):
        bv, bi = vals[0], jnp.zeros(vals[0].shape, I32)
        for j in range(1, E_PER_GROUP):
            better = vals[j] > bv
            bv = jnp.where(better, vals[j], bv)
            bi = jnp.where(better, j, bi)
        return bi

    i1 = argmax_first(sv)
    i2 = argmax_first([jnp.where(i1 == j, -jnp.inf, sv[j]) for j in range(E_PER_GROUP)])
    g1, g2 = av[0], av[0]
    for j in range(1, E_PER_GROUP):
        g1 = jnp.where(i1 == j, av[j], g1)
        g2 = jnp.where(i2 == j, av[j], g2)
    tot = g1 + g2
    return gi, i1, i2, g1 / tot, g2 / tot


def _route(sel, aff):
    gi, i1, i2, g1, g2 = _route_choice(sel, aff)
    comb = []
    for e in range(N_EXPERTS):
        g, j = divmod(e, E_PER_GROUP)
        in_g = gi == g
        comb.append(jnp.where(in_g & (i1 == j), g1, jnp.where(in_g & (i2 == j), g2, 0.0)))
    return comb


def _router_rows(x, rwt_hi, rwt_lo, rb):
    xh = x.astype(BF16)
    xl = (x - xh.astype(F32)).astype(BF16)
    logits = _dot_nt(rwt_hi, xh) + (_dot_nt(rwt_hi, xl) + _dot_nt(rwt_lo, xh))
    aff = _sigmoid(logits)
    sel = aff + rb
    return [sel[e:e + 1] for e in range(N_EXPERTS)], [aff[e:e + 1] for e in range(N_EXPERTS)]


def _moe_dense_kernel(x_ref, rwh_ref, rwl_ref, rb_ref, wg_ref, wu_ref, wd_ref, g_ref, b_ref, out_ref,
                      xb_ref, comb_ref, acc_ref, *, tm):
    e = pl.program_id(1)

    @pl.when(e == 0)
    def _():
        x = x_ref[...]
        xb_ref[...] = x.astype(BF16)
        sel, aff = _router_rows(x, rwh_ref[...], rwl_ref[...], rb_ref[...])
        comb = jnp.concatenate(_route(sel, aff) + [jnp.zeros((LANES - N_EXPERTS, tm), F32)], axis=0)
        comb_ref[...] = comb.T
        acc_ref[...] = jnp.zeros(acc_ref.shape, F32)

    xb = xb_ref[...]
    hg = _dot(xb, wg_ref[...])
    hu = _dot(xb, wu_ref[...])
    hdn = (hg * _sigmoid(hg)) * hu
    y = _dot(hdn.astype(BF16), wd_ref[...])
    lane = lax.broadcasted_iota(I32, (tm, LANES), 1)
    col = jnp.sum(jnp.where(lane == e, comb_ref[...], 0.0), axis=1, keepdims=True)
    acc_ref[...] += col * y

    @pl.when(e == N_EXPERTS - 1)
    def _():
        out_ref[...] = _layer_norm(ALPHA * x_ref[...] + acc_ref[...], g_ref[...], b_ref[...])


def _moe_ln(x, rwt_hi, rwt_lo, rb, wg, wu, wd, g, b, *, layer, tm):
    t, d = x.shape
    f = wg.shape[3]
    full = lambda shape: pl.BlockSpec(shape, lambda i, e: (0,) * len(shape))
    return pl.pallas_call(
        functools.partial(_moe_dense_kernel, tm=tm),
        grid=(t // tm, N_EXPERTS),
        in_specs=[
            pl.BlockSpec((tm, d), lambda i, e: (i, 0)),
            full((N_EXPERTS, d)), full((N_EXPERTS, d)), full((N_EXPERTS, 1)),
            pl.BlockSpec((None, None, d, f), lambda i, e: (layer, e, 0, 0)),
            pl.BlockSpec((None, None, d, f), lambda i, e: (layer, e, 0, 0)),
            pl.BlockSpec((None, None, f, d), lambda i, e: (layer, e, 0, 0)),
            full((1, d)), full((1, d)),
        ],
        out_specs=pl.BlockSpec((tm, d), lambda i, e: (i, 0)),
        out_shape=jax.ShapeDtypeStruct((t, d), F32),
        scratch_shapes=[pltpu.VMEM((tm, d), BF16), pltpu.VMEM((tm, LANES), F32), pltpu.VMEM((tm, d), F32)],
        compiler_params=_params(("arbitrary", "arbitrary")),
        name="moe_ln",
    )(x, rwt_hi, rwt_lo, rb, wg, wu, wd, g, b)


PAIRS_PER_GROUP = 6
N_CLASSES = N_GROUPS * PAIRS_PER_GROUP
MOE_TILE = 256


def _moe_plan_kernel(cls_ref, slot_ref, tcls_ref, nused_ref, *, tm):
    cls = cls_ref[...]
    r = cls.shape[0]
    upper = jnp.where(lax.broadcasted_iota(I32, (LANES, LANES), 0) < lax.broadcasted_iota(I32, (LANES, LANES), 1),
                      1.0, 0.0).astype(BF16)
    lower = jnp.where(lax.broadcasted_iota(I32, (r, r), 1) < lax.broadcasted_iota(I32, (r, r), 0),
                      1.0, 0.0).astype(BF16)
    tile_start = lax.broadcasted_iota(I32, (1, LANES), 1).astype(F32) * tm
    base = jnp.zeros((1, 1), F32)
    slot = jnp.zeros((r, LANES), F32)
    tcls = jnp.zeros((1, LANES), F32)
    last = jnp.zeros((1, 1), F32)
    for c in range(N_CLASSES):
        oh = cls == c
        ohf = jnp.where(oh, 1.0, 0.0)
        before_in_row = _dot(ohf.astype(BF16), upper)
        row_total = jnp.sum(ohf, axis=1, keepdims=True)
        before_rows = _dot(lower, jnp.broadcast_to(row_total, (r, LANES)).astype(BF16))
        count = jnp.sum(row_total, axis=0, keepdims=True)
        padded = jnp.ceil(count / tm) * tm
        slot = slot + jnp.where(oh, base + before_in_row + before_rows, 0.0)
        tcls = jnp.where((tile_start >= base) & (tile_start < base + padded), float(c), tcls)
        last = jnp.where(padded > 0, float(c), last)
        base = base + padded
    slot_ref[...] = slot.astype(I32)
    tcls_ref[...] = jnp.where(tile_start >= base, last, tcls).astype(I32)
    nused_ref[...] = jnp.broadcast_to(base / tm, (1, LANES)).astype(I32)


def _moe_plan(cls, *, tm):
    t = cls.shape[1]
    r = t // LANES
    full = lambda shape: pl.BlockSpec(shape, lambda: (0, 0))
    slot, tcls, nused = pl.pallas_call(
        functools.partial(_moe_plan_kernel, tm=tm),
        in_specs=[full((r, LANES))],
        out_specs=[full((r, LANES)), full((1, LANES)), full((1, LANES))],
        out_shape=[jax.ShapeDtypeStruct((r, LANES), I32), jax.ShapeDtypeStruct((1, LANES), I32),
                   jax.ShapeDtypeStruct((1, LANES), I32)],
        name="moe_plan",
    )(cls.reshape(r, LANES))
    return slot.reshape(t // tm, 1, tm), tcls[0], nused[0, :1]


def _wait_rows(src, dst, sem):
    pltpu.make_async_copy(src, dst, sem).wait()


def _moe_scatter_kernel(x_ref, slot_ref, xs_in, xs_out, stage, sems, *, tm, n_steps):
    i = pl.program_id(0)
    b = i & 1
    tile_rows = xs_out.at[pl.ds(0, tm)]

    @pl.when(i >= 2)
    def _():
        _wait_rows(stage.at[b], tile_rows, sems.at[b])

    stage[b] = x_ref[...]

    def issue(r, carry):
        pltpu.make_async_copy(stage.at[b, pl.ds(r, 1)], xs_out.at[pl.ds(slot_ref[0, r], 1)], sems.at[b]).start()
        return carry

    lax.fori_loop(0, tm, issue, 0, unroll=8)

    @pl.when(i == n_steps - 1)
    def _():
        _wait_rows(stage.at[b], tile_rows, sems.at[b])
        if n_steps > 1:
            _wait_rows(stage.at[1 - b], tile_rows, sems.at[1 - b])


def _moe_scatter(x, slot, *, tm, n_rows):
    t, d = x.shape
    n_steps = t // tm
    return pl.pallas_call(
        functools.partial(_moe_scatter_kernel, tm=tm, n_steps=n_steps),
        grid=(n_steps,),
        in_specs=[
            pl.BlockSpec((tm, d), lambda i: (i, 0)),
            pl.BlockSpec((None, 1, tm), lambda i: (i, 0, 0), memory_space=pltpu.SMEM),
            pl.BlockSpec(memory_space=pl.ANY),
        ],
        out_specs=pl.BlockSpec(memory_space=pl.ANY),
        out_shape=jax.ShapeDtypeStruct((n_rows, d), F32),
        scratch_shapes=[pltpu.VMEM((2, tm, d), F32), pltpu.SemaphoreType.DMA((2,))],
        input_output_aliases={2: 0},
        compiler_params=_params(("arbitrary",)),
        name="moe_scatter",
    )(x, slot, jnp.zeros((n_rows, d), F32))


def _class_experts(c):
    g = c // PAIRS_PER_GROUP
    pair = c - g * PAIRS_PER_GROUP
    lo = (pair >= 3).astype(I32) + (pair >= 5).astype(I32)
    hi = pair - (3 * (lo >= 1).astype(I32) + 2 * (lo >= 2).astype(I32)) + lo + 1
    return g * E_PER_GROUP + lo, g * E_PER_GROUP + hi


def _moe_sorted_kernel(tcls_ref, nused_ref, xs_ref, rwh_ref, rwl_ref, wg1, wu1, wd1, wg2, wu2, wd2,
                       g_ref, b_ref, out_ref, *, tm):
    i = pl.program_id(0)

    @pl.when(i >= nused_ref[0])
    def _():
        out_ref[...] = jnp.zeros(out_ref.shape, F32)

    @pl.when(i < nused_ref[0])
    def _():
        e_lo, e_hi = _class_experts(tcls_ref[i])
        x = xs_ref[...]
        xb = x.astype(BF16)
        xl = (x - xb.astype(F32)).astype(BF16)
        logits = _dot(xb, rwh_ref[...]) + (_dot(xl, rwh_ref[...]) + _dot(xb, rwl_ref[...]))
        aff = _sigmoid(logits)
        lane = lax.broadcasted_iota(I32, (tm, LANES), 1)
        a_lo = jnp.sum(jnp.where(lane == e_lo, aff, 0.0), axis=1, keepdims=True)
        a_hi = jnp.sum(jnp.where(lane == e_hi, aff, 0.0), axis=1, keepdims=True)
        tot = a_lo + a_hi

        def expert(wg, wu, wd):
            hg = _dot(xb, wg[...])
            return _dot(((hg * _sigmoid(hg)) * _dot(xb, wu[...])).astype(BF16), wd[...])

        acc = (a_lo / tot) * expert(wg1, wu1, wd1)
        acc = acc + (a_hi / tot) * expert(wg2, wu2, wd2)
        out_ref[...] = _layer_norm(ALPHA * x + acc, g_ref[...], b_ref[...])


def _moe_sorted(xs, tcls, nused, rw_hi, rw_lo, wg, wu, wd, g, b, *, layer, tm):
    n_rows, d = xs.shape
    f = wg.shape[3]
    full = lambda shape: pl.BlockSpec(shape, lambda i, tc, nu: (0,) * len(shape))

    def w_spec(shape, which):
        return pl.BlockSpec((None, None) + shape, lambda i, tc, nu: (layer, _class_experts(tc[i])[which], 0, 0))

    grid_spec = pltpu.PrefetchScalarGridSpec(
        num_scalar_prefetch=2,
        grid=(n_rows // tm,),
        in_specs=[pl.BlockSpec((tm, d), lambda i, tc, nu: (i, 0)), full(rw_hi.shape), full(rw_lo.shape),
                  w_spec((d, f), 0), w_spec((d, f), 0), w_spec((f, d), 0),
                  w_spec((d, f), 1), w_spec((d, f), 1), w_spec((f, d), 1),
                  full((1, d)), full((1, d))],
        out_specs=pl.BlockSpec((tm, d), lambda i, tc, nu: (i, 0)),
    )
    return pl.pallas_call(
        functools.partial(_moe_sorted_kernel, tm=tm),
        grid_spec=grid_spec,
        out_shape=jax.ShapeDtypeStruct((n_rows, d), F32),
        compiler_params=_params(("arbitrary",)),
        name="moe_sorted",
    )(tcls, nused, xs, rw_hi, rw_lo, wg, wu, wd, wg, wu, wd, g, b)


def _ple_math(x, p_ref, wg_ref, bg_ref, wp_ref):
    gate = _sigmoid(_dot(x.astype(BF16), wg_ref[...]) + bg_ref[...])
    return x + gate * _dot(p_ref[...].astype(BF16), wp_ref[...])


def _ple_gather_kernel(slot_ref, slot_next_ref, ys_ref, p_ref, wg_ref, bg_ref, wp_ref, out_ref, buf, sems,
                       *, tm, n_steps):
    i = pl.program_id(0)
    b = i & 1

    def row_copy(slots, r, bb):
        return pltpu.make_async_copy(ys_ref.at[pl.ds(slots[0, r], 1)], buf.at[bb, pl.ds(r, 1)], sems.at[bb])

    @pl.when(i == 0)
    def _():
        def body(r, carry):
            row_copy(slot_ref, r, 0).start()
            return carry
        lax.fori_loop(0, tm, body, 0, unroll=8)

    _wait_rows(ys_ref.at[pl.ds(0, tm)], buf.at[b], sems.at[b])
    for r in range(tm):
        row_copy(slot_next_ref, r, 1 - b).start()
    out_ref[...] = _ple_math(buf[b], p_ref, wg_ref, bg_ref, wp_ref)

    @pl.when(i == n_steps - 1)
    def _():
        _wait_rows(ys_ref.at[pl.ds(0, tm)], buf.at[1 - b], sems.at[1 - b])


def _ple_gather(ys, slot, p, wg, bg, wp, *, tm):
    d = ys.shape[1]
    t, pd = p.shape
    n_steps = t // tm
    full = lambda shape: pl.BlockSpec(shape, lambda i: (0, 0))
    return pl.pallas_call(
        functools.partial(_ple_gather_kernel, tm=tm, n_steps=n_steps),
        grid=(n_steps,),
        in_specs=[
            pl.BlockSpec((None, 1, tm), lambda i: (i, 0, 0), memory_space=pltpu.SMEM),
            pl.BlockSpec((None, 1, tm), lambda i: (jnp.minimum(i + 1, n_steps - 1), 0, 0), memory_space=pltpu.SMEM),
            pl.BlockSpec(memory_space=pl.ANY),
            pl.BlockSpec((tm, pd), lambda i: (i, 0)),
            full((d, d)), full((1, d)), full((pd, d)),
        ],
        out_specs=pl.BlockSpec((tm, d), lambda i: (i, 0)),
        out_shape=jax.ShapeDtypeStruct((t, d), F32),
        scratch_shapes=[pltpu.VMEM((2, tm, d), F32), pltpu.SemaphoreType.DMA((2,))],
        compiler_params=_params(("arbitrary",)),
        name="ple_gather",
    )(slot, slot, ys, p, wg, bg, wp)


def _ple_kernel(x_ref, p_ref, wg_ref, bg_ref, wp_ref, out_ref):
    out_ref[...] = _ple_math(x_ref[...], p_ref, wg_ref, bg_ref, wp_ref)


def _ple(x, p, wg, bg, wp, *, tm):
    t, d = x.shape
    pd = p.shape[1]
    full = lambda shape: pl.BlockSpec(shape, lambda i: (0, 0))
    return pl.pallas_call(
        _ple_kernel,
        grid=(t // tm,),
        in_specs=[pl.BlockSpec((tm, d), lambda i: (i, 0)), pl.BlockSpec((tm, pd), lambda i: (i, 0)),
                  full((d, d)), full((1, d)), full((pd, d))],
        out_specs=pl.BlockSpec((tm, d), lambda i: (i, 0)),
        out_shape=jax.ShapeDtypeStruct((t, d), F32),
        compiler_params=_params(("arbitrary",)),
        name="ple",
    )(x, p, wg, bg, wp)


def _pad_page_t(x, nb, s_new):
    xt = x.reshape(nb, s_new, -1).transpose(0, 2, 1)
    return jnp.pad(xt, ((0, 0), (0, 0), (0, LANES - s_new)))


def _layer_tail(o, w_out, x, p, w, layer, *, sort_tokens, tm):
    t = x.shape[0]
    ln_g, ln_b = w["ln_g"][layer], w["ln_b"][layer]
    wg, wu, wd = w["moe_w_gate"], w["moe_w_up"], w["moe_w_down"]
    ple = (w["ple_gate_w"][layer], w["ple_gate_b"][layer], w["ple_proj"][layer])
    router = (w["rwt_hi"], w["rwt_lo"], w["rb"])
    if not sort_tokens:
        x = _outproj_ln(o, w_out, x, ln_g[0], ln_b[0], tm=tm)
        x = _moe_ln(x, *router, wg, wu, wd, ln_g[1], ln_b[1], layer=layer, tm=512 if t % 512 == 0 else tm)
        return _ple(x, p, *ple, tm=tm)
    n_rows = t + N_CLASSES * MOE_TILE
    assert n_rows // MOE_TILE <= LANES, "the plan kernel lists at most 128 tiles"
    x, cls = _outproj_ln(o, w_out, x, ln_g[0], ln_b[0], router, tm=tm)
    slot, tile_cls, n_used = _moe_plan(cls, tm=MOE_TILE)
    xs = _moe_scatter(x, slot, tm=MOE_TILE, n_rows=n_rows)
    ys = _moe_sorted(xs, tile_cls, n_used, w["rw_hi"], w["rw_lo"], wg, wu, wd, ln_g[1], ln_b[1], layer=layer,
                     tm=MOE_TILE)
    return _ple_gather(ys, slot, p, *ple, tm=MOE_TILE)


def _trunk(x3, p4, pos_off, past, page_table, w):
    nb, seq, d = x3.shape
    t = nb * seq
    prompt = past is None
    x = x3.reshape(t, d)
    p = p4.reshape(DEPTH, t, -1)
    tm = min(256, t)
    tm_in = min(INPROJ_TILE, seq) if prompt else tm
    if prompt:
        tabs = _rope_tables(seq, seq, pos_off)
    else:
        tabs = _rope_tables(t, seq, pos_off)
    outs = {}

    lam_init = 0.8 - 0.6 * math.exp(-0.3 * 0)
    w_in = w["a_w_in"]
    nq = 2 * H_A * HD_A
    wq, wk, wv = w_in[:, :nq], w_in[:, nq:2 * nq], w_in[:, 2 * nq:]
    lams = [w[n] for n in ("a_lam_q1", "a_lam_k1", "a_lam_q2", "a_lam_k2")]
    if prompt:
        segs = [
            dict(w=wq, tr=False, rot=8, scale=HD_A ** -0.5, outs=[(BF16, False)]),
            dict(w=wk.T, tr=True, rot=8, scale=1.0, outs=[(F32, False), (BF16, False)]),
            dict(w=wv, tr=False, rot=0, scale=1.0, outs=[(F32, False), (BF16, False)]),
        ]
        q, kt, kt_b, v, v_b = _inproj(x, tabs, segs, seq=seq, tm=tm_in)
        o = _diff_attn_prompt(q, kt_b, v_b, lams, w["a_subln"], nb=nb, seq=seq, lam_init=lam_init)
        outs["a_k"] = kt.reshape(nb, 2 * H_A, HD_A, seq).transpose(0, 3, 1, 2)
    else:
        segs = [
            dict(w=wq, tr=False, rot=8, scale=HD_A ** -0.5, outs=[(F32, False)]),
            dict(w=wk, tr=False, rot=8, scale=1.0, outs=[(F32, False)]),
            dict(w=wv, tr=False, rot=0, scale=1.0, outs=[(F32, False)]),
        ]
        q, k, v = _inproj(x, tabs, segs, seq=seq, tm=tm)
        v_new = jnp.pad(v.reshape(nb, seq, H_A, 2 * HD_A), ((0, 0), (0, LANES - seq), (0, 0), (0, 0)))
        o = _diff_attn_decode(q, past["a_kt"], past["a_v"], page_table, _pad_page_t(k, nb, seq),
                              v_new.reshape(nb, LANES * H_A, 2 * HD_A), lams, w["a_subln"],
                              s_new=seq, lam_init=lam_init)
        outs["a_k"] = k.reshape(nb, seq, 2 * H_A, HD_A)
    outs["a_v"] = v.reshape(nb, seq, H_A, 2 * HD_A)
    x = _layer_tail(o, w["a_w_out"], x, p[0], w, 0, sort_tokens=prompt, tm=tm)

    w_in = w["b_w_in"]
    sizes = (H_B * HD_B, KV_B * HD_B, KV_B * HD_B, H_I * D_I, D_I, H_I)
    offs = [sum(sizes[:m]) for m in range(len(sizes) + 1)]
    wq, wk, wv, wqi, wki, wwi = [w_in[:, offs[m]:offs[m + 1]] for m in range(len(sizes))]
    wwi = jnp.pad(wwi, ((0, 0), (0, LANES - H_I)))
    if prompt:
        n_sel = min(TOPK_MAX, seq // 4)
        segs = [
            dict(w=wq, tr=False, rot=8, scale=HD_B ** -0.5, outs=[(BF16, False)]),
            dict(w=wk.T, tr=True, rot=8, scale=1.0, outs=[(F32, False), (BF16, False)]),
            dict(w=wv.T, tr=True, rot=0, scale=1.0, outs=[(F32, False)]),
            dict(w=wv, tr=False, rot=0, scale=1.0, outs=[(BF16, True)]),
            dict(w=wqi, tr=False, rot=16, scale=1.0, outs=[(BF16, False)]),
            dict(w=wki.T, tr=True, rot=16, scale=1.0, outs=[(F32, False), (BF16, False)]),
            dict(w=wwi, tr=False, rot=0, scale=IDX_SCALE, outs=[(F32, False)]),
        ]
        q, kt, kt_b, vt, v_b, qi, kit, kit_b, wi = _inproj(x, tabs, segs, seq=seq, tm=tm_in)
        bias = _dsa_select_prompt(qi, wi, kit_b, nb=nb, seq=seq, n_sel=n_sel)
        o = _dsa_attn_prompt(q, kt_b, v_b, bias, nb=nb, seq=seq)
        tr4 = lambda a, h: a.reshape(nb, h, HEAD, seq).transpose(0, 3, 1, 2)
        outs["b_k"] = tr4(kt, KV_B)
        outs["b_v"] = tr4(vt, KV_B)
        outs["b_ki"] = kit.transpose(0, 2, 1)
    else:
        n_past = page_table.shape[1] * LANES
        n_sel = min(TOPK_MAX, (n_past + seq) // 4)
        segs = [
            dict(w=wq, tr=False, rot=8, scale=HD_B ** -0.5, outs=[(F32, False)]),
            dict(w=wk, tr=False, rot=8, scale=1.0, outs=[(F32, False)]),
            dict(w=wv, tr=False, rot=0, scale=1.0, outs=[(F32, False)]),
            dict(w=wqi, tr=False, rot=16, scale=1.0, outs=[(F32, False)]),
            dict(w=jnp.pad(wki, ((0, 0), (0, LANES - D_I))), tr=False, rot=16, scale=1.0, outs=[(F32, False)]),
            dict(w=wwi, tr=False, rot=0, scale=IDX_SCALE, outs=[(F32, False)]),
        ]
        q, k, v, qi, ki, wi = _inproj(x, tabs, segs, seq=seq, tm=tm)
        ki = ki[:, :D_I]
        bias = _dsa_select_decode(qi, wi, past["b_kit"], page_table, _pad_page_t(ki, nb, seq), s_new=seq, n_sel=n_sel)
        o = _dsa_attn_decode(q, bias, past["b_kt"], past["b_vt"], page_table, _pad_page_t(k, nb, seq),
                             _pad_page_t(v, nb, seq), s_new=seq)
        outs["b_k"] = k.reshape(nb, seq, KV_B, HD_B)
        outs["b_v"] = v.reshape(nb, seq, KV_B, HD_B)
        outs["b_ki"] = ki.reshape(nb, seq, D_I)
    x = _layer_tail(o, w["b_w_out"], x, p[1], w, 1, sort_tokens=prompt, tm=tm)
    return x.reshape(nb, seq, d), outs


def _prep_weights(a_w_in, a_w_out, a_lam_q1, a_lam_k1, a_lam_q2, a_lam_k2, a_subln, b_w_in, b_w_out, ln_g, ln_b,
                  router_w, router_b, moe_w_gate, moe_w_up, moe_w_down, ple_proj, ple_gate_w, ple_gate_b):
    rwt = router_w.T
    rwt_hi = rwt.astype(BF16)
    rw = jnp.pad(router_w, ((0, 0), (0, LANES - N_EXPERTS)))
    rw_hi = rw.astype(BF16)
    return {
        "a_w_in": a_w_in[0].astype(BF16), "a_w_out": a_w_out[0].astype(BF16),
        "a_lam_q1": a_lam_q1, "a_lam_k1": a_lam_k1, "a_lam_q2": a_lam_q2, "a_lam_k2": a_lam_k2,
        "a_subln": a_subln,
        "b_w_in": b_w_in[0].astype(BF16), "b_w_out": b_w_out[0].astype(BF16),
        "ln_g": ln_g[:, :, None, :], "ln_b": ln_b[:, :, None, :],
        "rwt_hi": rwt_hi, "rwt_lo": (rwt - rwt_hi.astype(F32)).astype(BF16), "rb": router_b.reshape(N_EXPERTS, 1),
        "rw_hi": rw_hi, "rw_lo": (rw - rw_hi.astype(F32)).astype(BF16),
        "moe_w_gate": moe_w_gate.astype(BF16), "moe_w_up": moe_w_up.astype(BF16),
        "moe_w_down": moe_w_down.astype(BF16),
        "ple_proj": ple_proj.astype(BF16), "ple_gate_w": ple_gate_w.astype(BF16),
        "ple_gate_b": ple_gate_b[:, None, :],
    }


def kernel(x_prompt, x_sample, cache_a_k, cache_a_v, cache_b_k, cache_b_v, cache_b_kidx, page_table, p_prompt,
           p_sample, a_w_in, a_w_out, a_lam_q1, a_lam_k1, a_lam_q2, a_lam_k2, a_subln, b_w_in, b_w_out, ln_g, ln_b,
           router_w, router_b, moe_w_gate, moe_w_up, moe_w_down, ple_proj, ple_gate_w, ple_gate_b):
    w = _prep_weights(a_w_in, a_w_out, a_lam_q1, a_lam_k1, a_lam_q2, a_lam_k2, a_subln, b_w_in, b_w_out, ln_g, ln_b,
                      router_w, router_b, moe_w_gate, moe_w_up, moe_w_down, ple_proj, ple_gate_w, ple_gate_b)
    n_pool, page = cache_a_k.shape[1], cache_a_k.shape[2]
    past_len = page_table.shape[1] * page
    past = {
        "a_kt": cache_a_k[0].transpose(0, 2, 3, 1).reshape(n_pool, 2 * H_A * HD_A, page),
        "a_v": cache_a_v[0].reshape(n_pool, page * H_A, 2 * HD_A),
        "b_kt": cache_b_k[0].transpose(0, 2, 3, 1).reshape(n_pool, KV_B * HD_B, page),
        "b_vt": cache_b_v[0].transpose(0, 2, 3, 1).reshape(n_pool, KV_B * HD_B, page),
        "b_kit": cache_b_kidx[0].transpose(0, 2, 1),
    }
    y_p, op = _trunk(x_prompt, p_prompt, 0, None, None, w)
    y_s, os_ = _trunk(x_sample, p_sample, past_len, past, page_table, w)
    lead = lambda a: a[None]
    return (y_p, y_s,
            lead(op["a_k"]), lead(op["a_v"]), lead(op["b_k"]), lead(op["b_v"]), lead(op["b_ki"]),
            lead(os_["a_k"]), lead(os_["a_v"]), lead(os_["b_k"]), lead(os_["b_v"]), lead(os_["b_ki"]))
```

```python
import functools
import math

import jax
import jax.numpy as jnp
from jax import lax
from jax.experimental import pallas as pl
from jax.experimental.pallas import tpu as pltpu

F32 = jnp.float32
BF16 = jnp.bfloat16
I32 = jnp.int32

D_MODEL = 1024
DEPTH = 2
H_A = 8
HD_A = 64
H_B = 16
KV_B = 4
HD_B = 64
H_I = 8
D_I = 64
TOPK_MAX = 256
ROPE_THETA = 500000.0
N_EXPERTS = 16
N_GROUPS = 4
E_PER_GROUP = 4
D_FF = 512
ALPHA = (2 * DEPTH) ** 0.25
LN_EPS = 1e-5
RMS_EPS = 1e-5
IDX_SCALE = (H_I ** -0.5) * (D_I ** -0.5)

LANES = 128
HEAD = 64
VMEM_LIMIT = 56 * 1024 * 1024
INT_MIN = -2 ** 31
NEG_BIG = -1e30


def _params(sem):
    return pltpu.CompilerParams(dimension_semantics=sem, vmem_limit_bytes=VMEM_LIMIT)


def _dot(a, b):
    return jnp.dot(a, b, preferred_element_type=F32)


def _dot_nt(a, b):
    return lax.dot_general(a, b, (((1,), (1,)), ((), ())), preferred_element_type=F32)


def _div_pow2(x, n):
    assert n & (n - 1) == 0, n
    return x >> (n.bit_length() - 1)


def _layer_norm(y, g, b):
    mu = jnp.mean(y, axis=-1, keepdims=True)
    var = jnp.mean(jnp.square(y - mu), axis=-1, keepdims=True)
    return (y - mu) * lax.rsqrt(var + LN_EPS) * g + b


def _sigmoid(x):
    return 1.0 / (1.0 + jnp.exp(-x))


def _rope_tables_kernel(inv8l, inv16l, inv8c, inv16c, tok8, tok16, tr8, tr16, *, seq, off, tp):
    base = pl.program_id(0) * tp
    row = base + lax.broadcasted_iota(I32, (tp, LANES), 0)
    lane = lax.broadcasted_iota(I32, (tp, LANES), 1)
    pos = (off + (row & (seq - 1))).astype(F32)
    d = lane & (HEAD - 1)
    for half, inv, out in ((8, inv8l, tok8), (16, inv16l, tok16)):
        ang = pos * inv[...]
        c = jnp.cos(ang)
        s = jnp.sin(ang)
        out[0] = jnp.where(d < 2 * half, c, 1.0)
        out[1] = jnp.where(d < half, -s, 0.0)
        out[2] = jnp.where((d >= half) & (d < 2 * half), s, 0.0)
    col = base + lax.broadcasted_iota(I32, (1, tp), 1)
    posr = (off + (col & (seq - 1))).astype(F32)
    for inv, out in ((inv8c, tr8), (inv16c, tr16)):
        ang = posr * inv[...]
        out[0] = jnp.cos(ang)
        out[1] = jnp.sin(ang)


def _rope_tables(n_pos, seq, off):
    tp = min(n_pos, 256)
    lane = jnp.arange(LANES) % HEAD
    invs = []
    for half in (8, 16):
        inv = ROPE_THETA ** (-jnp.arange(half, dtype=F32) / half)
        invs.append((jnp.where(lane < 2 * half, inv[lane % half], 0.0).reshape(1, LANES), inv.reshape(half, 1)))
    full = lambda shape: pl.BlockSpec(shape, lambda i: (0,) * len(shape))
    return pl.pallas_call(
        functools.partial(_rope_tables_kernel, seq=seq, off=off, tp=tp),
        grid=(n_pos // tp,),
        in_specs=[full((1, LANES)), full((1, LANES)), full((8, 1)), full((16, 1))],
        out_specs=[
            pl.BlockSpec((3, tp, LANES), lambda i: (0, i, 0)),
            pl.BlockSpec((3, tp, LANES), lambda i: (0, i, 0)),
            pl.BlockSpec((2, 8, tp), lambda i: (0, 0, i)),
            pl.BlockSpec((2, 16, tp), lambda i: (0, 0, i)),
        ],
        out_shape=[
            jax.ShapeDtypeStruct((3, n_pos, LANES), F32),
            jax.ShapeDtypeStruct((3, n_pos, LANES), F32),
            jax.ShapeDtypeStruct((2, 8, n_pos), F32),
            jax.ShapeDtypeStruct((2, 16, n_pos), F32),
        ],
        compiler_params=_params(("arbitrary",)),
        name="rope_tables",
    )(invs[0][0], invs[1][0], invs[0][1], invs[1][1])


def _rope_tok(y, tab, half):
    c, a, b = tab[0], tab[1], tab[2]
    outs = []
    for k in range(y.shape[1] // LANES):
        yc = y[:, k * LANES:(k + 1) * LANES]
        outs.append(yc * c + pltpu.roll(yc, LANES - half, 1) * a + pltpu.roll(yc, half, 1) * b)
    return outs[0] if len(outs) == 1 else jnp.concatenate(outs, axis=1)


def _rope_tr(yt, tr, half):
    cos, sin = tr[0], tr[1]
    parts = []
    for h in range(yt.shape[0] // HEAD):
        b = h * HEAD
        x1 = yt[b:b + half]
        x2 = yt[b + half:b + 2 * half]
        parts += [x1 * cos - x2 * sin, x2 * cos + x1 * sin, yt[b + 2 * half:b + HEAD]]
    return jnp.concatenate(parts, axis=0)


def _inproj_kernel(*refs, segs):
    x_ref, tok8, tok16, tr8, tr16 = refs[:5]
    w_refs = refs[5:5 + len(segs)]
    out_refs = list(refs[5 + len(segs):])
    xb = x_ref[...].astype(BF16)
    for seg, w_ref in zip(segs, w_refs):
        if seg["tr"]:
            y = _dot_nt(w_ref[...], xb)
            if seg["rot"]:
                y = _rope_tr(y, tr8 if seg["rot"] == 8 else tr16, seg["rot"])
        else:
            y = _dot(xb, w_ref[...])
            if seg["rot"]:
                y = _rope_tok(y, tok8 if seg["rot"] == 8 else tok16, seg["rot"])
        if seg["scale"] != 1.0:
            y = y * seg["scale"]
        for dt, split in seg["outs"]:
            o_ref = out_refs.pop(0)
            if split:
                for g in range(y.shape[1] // HEAD):
                    o_ref[g] = y[:, g * HEAD:(g + 1) * HEAD].astype(dt)
            else:
                o_ref[...] = y.astype(dt)


INPROJ_TILE = 512


def _inproj(x, tabs, segs, *, seq, tm):
    t, k = x.shape
    nb = t // seq
    tps = max(seq // tm, 1)
    n_tab = tabs[0].shape[1] // tm
    in_specs = [
        pl.BlockSpec((tm, k), lambda i: (i, 0)),
        pl.BlockSpec((3, tm, LANES), lambda i: (0, i % n_tab, 0)),
        pl.BlockSpec((3, tm, LANES), lambda i: (0, i % n_tab, 0)),
        pl.BlockSpec((2, 8, tm), lambda i: (0, 0, i % n_tab)),
        pl.BlockSpec((2, 16, tm), lambda i: (0, 0, i % n_tab)),
    ]
    out_specs, out_shape = [], []
    for seg in segs:
        w = seg["w"]
        in_specs.append(pl.BlockSpec(w.shape, lambda i: (0, 0)))
        n = w.shape[0] if seg["tr"] else w.shape[1]
        for dt, split in seg["outs"]:
            if seg["tr"]:
                out_specs.append(pl.BlockSpec((None, n, tm), lambda i: (i // tps, 0, i % tps)))
                out_shape.append(jax.ShapeDtypeStruct((nb, n, seq), dt))
            elif split:
                out_specs.append(pl.BlockSpec((n // HEAD, tm, HEAD), lambda i: (0, i, 0)))
                out_shape.append(jax.ShapeDtypeStruct((n // HEAD, t, HEAD), dt))
            else:
                out_specs.append(pl.BlockSpec((tm, n), lambda i: (i, 0)))
                out_shape.append(jax.ShapeDtypeStruct((t, n), dt))
    kern_segs = tuple({k2: v for k2, v in seg.items() if k2 != "w"} for seg in segs)
    return pl.pallas_call(
        functools.partial(_inproj_kernel, segs=kern_segs),
        grid=(t // tm,),
        in_specs=in_specs,
        out_specs=out_specs,
        out_shape=out_shape,
        compiler_params=_params(("arbitrary",)),
        name="inproj",
    )(x, *tabs, *[seg["w"] for seg in segs])


def _lambda(lq1, lk1, lq2, lk2, lam_init):
    return (jnp.exp(jnp.sum(lq1[...] * lk1[...], axis=1, keepdims=True))
            - jnp.exp(jnp.sum(lq2[...] * lk2[...], axis=1, keepdims=True)) + lam_init)


def _sub_norm(o, sub, lam_init):
    return o * lax.rsqrt(jnp.mean(o * o, axis=-1, keepdims=True) + RMS_EPS) * sub * (1.0 - lam_init)


def _diff_attn_kernel(q_ref, kt_ref, v_ref, lq1, lk1, lq2, lk2, sub_ref, o_ref, *, tq, tk, lam_init):
    qi = pl.program_id(2)
    q = q_ref[...]
    ratio = tq // tk

    def step(j, carry, masked):
        off = pl.multiple_of(j * tk, tk)
        v = v_ref[pl.ds(off, tk), :]
        new = []
        for c in range(2):
            m, l, acc = carry[c]
            kt = kt_ref[c * HEAD:(c + 1) * HEAD, pl.ds(off, tk)]
            s = _dot(q[:, c * HEAD:(c + 1) * HEAD], kt)
            if masked:
                row = lax.broadcasted_iota(I32, (tq, tk), 0)
                col = lax.broadcasted_iota(I32, (tq, tk), 1)
                s = jnp.where(col <= row + (qi * tq - j * tk), s, -jnp.inf)
            m_new = jnp.maximum(m, jnp.max(s, axis=1, keepdims=True))
            alpha = jnp.exp(m - m_new)
            p = jnp.exp(s - m_new)
            l = alpha * l + jnp.sum(p, axis=1, keepdims=True)
            acc = alpha * acc + _dot(p.astype(BF16), v)
            new.append((m_new, l, acc))
        return tuple(new)

    init = tuple((jnp.full((tq, 1), -jnp.inf, F32), jnp.zeros((tq, 1), F32), jnp.zeros((tq, 2 * HEAD), F32))
                 for _ in range(2))
    carry = lax.fori_loop(0, qi * ratio, functools.partial(step, masked=False), init)
    if ratio == 1:
        carry = step(qi, carry, True)
    else:
        carry = lax.fori_loop(qi * ratio, (qi + 1) * ratio, functools.partial(step, masked=True), carry)
    (_, l0, a0), (_, l1, a1) = carry
    lam = _lambda(lq1, lk1, lq2, lk2, lam_init)
    o = a0 / l0 - lam * (a1 / l1)
    o_ref[...] = _sub_norm(o, sub_ref[...], lam_init).astype(o_ref.dtype)


DIFF_TQ = 1024
DIFF_TK = 1024


def _diff_attn_prompt(q, kt, v, lams, sub, *, nb, seq, lam_init):
    tq = min(seq, DIFF_TQ)
    tk = min(seq, DIFF_TK)
    nq = seq // tq
    small = lambda shape: pl.BlockSpec(shape, lambda b, h, i: (0, 0))
    return pl.pallas_call(
        functools.partial(_diff_attn_kernel, tq=tq, tk=tk, lam_init=lam_init),
        grid=(nb, H_A, nq),
        in_specs=[
            pl.BlockSpec((tq, 2 * HEAD), lambda b, h, i: (b * nq + i, h)),
            pl.BlockSpec((None, 2 * HEAD, seq), lambda b, h, i: (b, h, 0)),
            pl.BlockSpec((seq, 2 * HEAD), lambda b, h, i: (b, h)),
            small((1, HEAD)), small((1, HEAD)), small((1, HEAD)), small((1, HEAD)), small((1, 2 * HEAD)),
        ],
        out_specs=pl.BlockSpec((tq, 2 * HEAD), lambda b, h, i: (b * nq + i, h)),
        out_shape=jax.ShapeDtypeStruct(q.shape, BF16),
        compiler_params=_params(("arbitrary", "arbitrary", "arbitrary")),
        name="diff_attn_prompt",
    )(q, kt, v, *lams, sub)


PAGES_PER_STEP = 8
DSA_PAGES_PER_STEP = 16


def _block_diag_q(q, n_heads, width):
    s = q.shape[0]
    rows = jnp.concatenate([q] * n_heads, axis=0)
    r = _div_pow2(lax.broadcasted_iota(I32, rows.shape, 0), s)
    c = _div_pow2(lax.broadcasted_iota(I32, rows.shape, 1), HEAD)
    return jnp.where(r == c, rows, 0.0)


def _diff_attn_dec_kernel(pt_ref, q_ref, *refs, g, s_new, n_steps, lam_init):
    kt_refs, v_refs = refs[:g], refs[g:2 * g]
    ktn_ref, vn_ref, lq1, lk1, lq2, lk2, sub_ref, o_ref, qbd_ref, m_ref, l_ref, acc_ref = refs[2 * g:]
    step = pl.program_id(1)
    nr = 2 * H_A * s_new

    @pl.when(step == 0)
    def _():
        qbd_ref[...] = _block_diag_q(q_ref[...], 2 * H_A, D_MODEL).astype(BF16)
        m_ref[...] = jnp.full(m_ref.shape, -jnp.inf, F32)
        l_ref[...] = jnp.zeros(l_ref.shape, F32)
        acc_ref[...] = jnp.zeros(acc_ref.shape, F32)

    def consume(kts, page_v_refs, mask):
        kt = kts[0] if len(kts) == 1 else jnp.concatenate(kts, axis=1)
        s = _dot(qbd_ref[...], kt.astype(BF16))
        if mask is not None:
            s = jnp.where(mask, s, -jnp.inf)
        m = m_ref[...]
        m_new = jnp.maximum(m, jnp.max(s, axis=1, keepdims=True))
        alpha = jnp.exp(m - m_new)
        p = jnp.exp(s - m_new)
        l_ref[...] = alpha * l_ref[...] + jnp.sum(p, axis=1, keepdims=True)
        m_ref[...] = m_new
        pb = p.astype(BF16)

        def head_values(h):
            vs = [v_ref[pl.ds(h, LANES, stride=H_A), :] for v_ref in page_v_refs]
            return (vs[0] if len(vs) == 1 else jnp.concatenate(vs, axis=0)).astype(BF16)

        pvs = []
        hr = 2 * s_new
        for h in range(0, H_A, 2):
            both = _dot(pb[hr * h:hr * (h + 2)], jnp.concatenate([head_values(h), head_values(h + 1)], axis=1))
            pvs += [both[:hr, :2 * HEAD], both[hr:, 2 * HEAD:]]
        acc_ref[...] = alpha * acc_ref[...] + jnp.concatenate(pvs, axis=0)

    @pl.when(step < n_steps - 1)
    def _():
        consume([kt_ref[...] for kt_ref in kt_refs], v_refs, None)

    @pl.when(step == n_steps - 1)
    def _():
        key = lax.broadcasted_iota(I32, (nr, LANES), 1)
        tok = lax.broadcasted_iota(I32, (nr, LANES), 0) & (s_new - 1)
        consume([ktn_ref[...]], [vn_ref], key <= tok)
        lam = _lambda(lq1, lk1, lq2, lk2, lam_init)
        o = acc_ref[...] / l_ref[...]
        outs = []
        for h in range(H_A):
            b = 2 * s_new * h
            outs.append(_sub_norm(o[b:b + s_new] - lam * o[b + s_new:b + 2 * s_new], sub_ref[...], lam_init))
        o_ref[...] = jnp.concatenate(outs, axis=1)


def _diff_attn_decode(q, kt_pool, v_pool, page_table, kt_new, v_new, lams, sub, *, s_new, lam_init):
    nb, n_pages = page_table.shape
    g = min(PAGES_PER_STEP, n_pages)
    assert g % 2 == 0 and n_pages % g == 0
    n_steps = n_pages // g + 1
    nr = 2 * H_A * s_new

    def page_map(k):
        return lambda b, i, pt: (pt[b, jnp.minimum(i, n_steps - 2) * g + k], 0, 0)

    page = lambda k: pl.BlockSpec((None, D_MODEL, LANES), page_map(k))
    new = pl.BlockSpec((None, D_MODEL, LANES), lambda b, i, pt: (b, 0, 0))
    small = lambda shape: pl.BlockSpec(shape, lambda b, i, pt: (0, 0))
    grid_spec = pltpu.PrefetchScalarGridSpec(
        num_scalar_prefetch=1,
        grid=(nb, n_steps),
        in_specs=[pl.BlockSpec((s_new, D_MODEL), lambda b, i, pt: (b, 0))]
        + [page(k) for k in range(g)] + [page(k) for k in range(g)] + [new, new]
        + [small((1, HEAD))] * 4 + [small((1, 2 * HEAD))],
        out_specs=pl.BlockSpec((s_new, D_MODEL), lambda b, i, pt: (b, 0)),
        scratch_shapes=[
            pltpu.VMEM((nr, D_MODEL), BF16),
            pltpu.VMEM((nr, 1), F32),
            pltpu.VMEM((nr, 1), F32),
            pltpu.VMEM((nr, 2 * HEAD), F32),
        ],
    )
    return pl.pallas_call(
        functools.partial(_diff_attn_dec_kernel, g=g, s_new=s_new, n_steps=n_steps, lam_init=lam_init),
        grid_spec=grid_spec,
        out_shape=jax.ShapeDtypeStruct(q.shape, F32),
        compiler_params=_params(("arbitrary", "arbitrary")),
        name="diff_attn_decode",
    )(page_table, q, *([kt_pool] * g), *([v_pool] * g), kt_new, v_new, *lams, sub)


def _float_of_rank(u):
    key = u ^ INT_MIN
    bits = key ^ ((key >> 31) & 0x7FFFFFFF)
    return lax.bitcast_convert_type(bits, F32)


def _count(pred):
    return jnp.sum(pred.astype(I32), axis=1, keepdims=True)


def _write_bias(store, score, valid, col, n_valid, n_sel, idx_bits):
    score = jnp.where(valid, score, jnp.nan)

    def body(i, t_u):
        cand = t_u | lax.shift_left(jnp.int32(1), 31 - i)
        return jnp.where(_count(score >= _float_of_rank(cand)) >= n_sel, cand, t_u)

    t = _float_of_rank(lax.fori_loop(0, 32, body, jnp.zeros((score.shape[0], 1), I32)))
    keep_all = n_valid <= n_sel
    t = jnp.where(keep_all, -jnp.inf, t)
    tie_rows = jnp.logical_not(keep_all) & (_count(score >= t) > n_sel)
    store(jnp.where(score >= t, 0.0, NEG_BIG))

    @pl.when(jnp.max(tie_rows.astype(I32)) > 0)
    def _():
        need = n_sel - _count(score > t)
        eq = score == t

        def idx_body(i, c):
            cand = c | lax.shift_left(jnp.int32(1), idx_bits - 1 - i)
            return jnp.where(_count(eq & (col < cand)) <= need - 1, cand, c)

        c = jnp.where(tie_rows, lax.fori_loop(0, idx_bits, idx_body, jnp.zeros_like(need)), 2 ** 30)
        store(jnp.where((score > t) | (eq & (col <= c)), 0.0, NEG_BIG))


SELECT_WIDTH_STEP = 512
SCORE_CHUNK = 256


def _dsa_select_kernel(qi_ref, w_ref, kit_ref, bias_ref, *, tq, n_sel, wstep):
    seq = kit_ref.shape[1]
    j = pl.program_id(1)
    variant = lax.shift_right_logical((j + 1) * tq + wstep - 1, wstep.bit_length() - 1) - 1

    def run(width):
        qi = qi_ref[...]
        w = w_ref[...]
        chunks = []
        for c in range(width // SCORE_CHUNK):
            kit = kit_ref[:, c * SCORE_CHUNK:(c + 1) * SCORE_CHUNK]
            sc = w[:, 0:1] * jnp.maximum(_dot(qi[:, 0:HEAD], kit), 0.0)
            for h in range(1, H_I):
                sc = sc + w[:, h:h + 1] * jnp.maximum(_dot(qi[:, h * HEAD:(h + 1) * HEAD], kit), 0.0)
            chunks.append(sc)
        score = chunks[0] if len(chunks) == 1 else jnp.concatenate(chunks, axis=1)
        row = j * tq + lax.broadcasted_iota(I32, (tq, width), 0)
        col = lax.broadcasted_iota(I32, (tq, width), 1)

        def store(bias):
            bias_ref[:, :width] = bias.astype(bias_ref.dtype)

        _write_bias(store, score, col <= row, col, row[:, :1] + 1, n_sel, width.bit_length())
        if width < seq:
            bias_ref[:, width:] = jnp.full((tq, seq - width), NEG_BIG, bias_ref.dtype)

    for k in range(seq // wstep):
        pl.when(variant == k)(functools.partial(run, (k + 1) * wstep))


def _dsa_select_prompt(qi, w, kit, *, nb, seq, n_sel):
    tq = min(seq, SELECT_WIDTH_STEP)
    nq = seq // tq
    return pl.pallas_call(
        functools.partial(_dsa_select_kernel, tq=tq, n_sel=n_sel, wstep=min(seq, SELECT_WIDTH_STEP)),
        grid=(nb, nq),
        in_specs=[
            pl.BlockSpec((tq, H_I * HEAD), lambda b, i: (b * nq + i, 0)),
            pl.BlockSpec((tq, LANES), lambda b, i: (b * nq + i, 0)),
            pl.BlockSpec((None, HEAD, seq), lambda b, i: (b, 0, 0)),
        ],
        out_specs=pl.BlockSpec((None, tq, seq), lambda b, i: (b, i, 0)),
        out_shape=jax.ShapeDtypeStruct((nb, seq, seq), BF16),
        compiler_params=_params(("arbitrary", "arbitrary")),
        name="dsa_select_prompt",
    )(qi, w, kit)


def _dsa_score_dec_kernel(pt_ref, qi_ref, w_ref, *refs, g, s_new, n_steps, n_past):
    kit_refs = refs[:g]
    kitn_ref, score_ref, qs_ref, ws_ref = refs[g:]
    step = pl.program_id(1)

    @pl.when(step == 0)
    def _():
        qi = qi_ref[...]
        w = w_ref[...]
        qs_ref[...] = jnp.concatenate([qi[:, h * HEAD:(h + 1) * HEAD] for h in range(H_I)], axis=0).astype(BF16)
        ws_ref[...] = jnp.concatenate([w[:, h:h + 1] for h in range(H_I)], axis=0)

    def page_score(kit):
        d = jnp.maximum(_dot(qs_ref[...], kit.astype(BF16)), 0.0) * ws_ref[...]
        sc = d[0:s_new]
        for h in range(1, H_I):
            sc = sc + d[h * s_new:(h + 1) * s_new]
        return sc

    @pl.when(step < n_steps - 1)
    def _():
        off = pl.multiple_of(step * (g * LANES), g * LANES)
        score_ref[:, pl.ds(off, g * LANES)] = page_score(jnp.concatenate([r[...] for r in kit_refs], axis=1))

    @pl.when(step == n_steps - 1)
    def _():
        score_ref[:, n_past:n_past + LANES] = page_score(kitn_ref[...])


def _dsa_select_rows_kernel(score_ref, bias_ref, *, s_new, n_sel, n_past):
    rows, width = score_ref.shape
    col = lax.broadcasted_iota(I32, (rows, width), 1)
    tok = lax.broadcasted_iota(I32, (rows, width), 0) & (s_new - 1)

    def store(bias):
        bias_ref[...] = bias

    _write_bias(store, score_ref[...], col <= n_past + tok, col, n_past + tok[:, :1] + 1, n_sel, width.bit_length())


SELECT_PAGES_PER_STEP = 32


def _dsa_select_decode(qi, w, kit_pool, page_table, kit_new, *, s_new, n_sel):
    nb, n_pages = page_table.shape
    g = min(SELECT_PAGES_PER_STEP, n_pages)
    n_steps = n_pages // g + 1
    n_past = n_pages * LANES
    width = n_past + LANES

    def page_map(k):
        return lambda b, i, pt: (pt[b, jnp.minimum(i, n_steps - 2) * g + k], 0, 0)

    grid_spec = pltpu.PrefetchScalarGridSpec(
        num_scalar_prefetch=1,
        grid=(nb, n_steps),
        in_specs=[pl.BlockSpec((s_new, H_I * HEAD), lambda b, i, pt: (b, 0)),
                  pl.BlockSpec((s_new, LANES), lambda b, i, pt: (b, 0))]
        + [pl.BlockSpec((None, HEAD, LANES), page_map(k)) for k in range(g)]
        + [pl.BlockSpec((None, HEAD, LANES), lambda b, i, pt: (b, 0, 0))],
        out_specs=pl.BlockSpec((None, s_new, width), lambda b, i, pt: (b, 0, 0)),
        scratch_shapes=[
            pltpu.VMEM((H_I * s_new, HEAD), BF16),
            pltpu.VMEM((H_I * s_new, 1), F32),
        ],
    )
    score = pl.pallas_call(
        functools.partial(_dsa_score_dec_kernel, g=g, s_new=s_new, n_steps=n_steps, n_past=n_past),
        grid_spec=grid_spec,
        out_shape=jax.ShapeDtypeStruct((nb, s_new, width), F32),
        compiler_params=_params(("arbitrary", "arbitrary")),
        name="dsa_score_decode",
    )(page_table, qi, w, *([kit_pool] * g), kit_new)
    rows = nb * s_new
    tr = min(rows, LANES)
    bias = pl.pallas_call(
        functools.partial(_dsa_select_rows_kernel, s_new=s_new, n_sel=n_sel, n_past=n_past),
        grid=(rows // tr,),
        in_specs=[pl.BlockSpec((tr, width), lambda i: (i, 0))],
        out_specs=pl.BlockSpec((tr, width), lambda i: (i, 0)),
        out_shape=jax.ShapeDtypeStruct((rows, width), F32),
        compiler_params=_params(("arbitrary",)),
        name="dsa_select_decode",
    )(score.reshape(rows, width))
    return bias.reshape(nb, s_new, width)


def _dsa_attn_kernel(q_ref, kt_ref, v_ref, bias_ref, o_ref, *, tq, tk):
    grp = H_B // KV_B
    qi = pl.program_id(2)
    qb = q_ref[...]
    q4 = jnp.concatenate([qb[:, h * HEAD:(h + 1) * HEAD] for h in range(grp)], axis=0)
    n_kv = lax.shift_right_logical((qi + 1) * tq + tk - 1, tk.bit_length() - 1)

    def step(j, carry):
        m, l, acc = carry
        off = pl.multiple_of(j * tk, tk)
        s = _dot(q4, kt_ref[:, pl.ds(off, tk)])
        s = (s.reshape(grp, tq, tk) + bias_ref[:, pl.ds(off, tk)].astype(F32)[None]).reshape(grp * tq, tk)
        m_new = jnp.maximum(m, jnp.max(s, axis=1, keepdims=True))
        alpha = jnp.exp(m - m_new)
        p = jnp.exp(s - m_new)
        l = alpha * l + jnp.sum(p, axis=1, keepdims=True)
        acc = alpha * acc + _dot(p.astype(BF16), v_ref[pl.ds(off, tk), :])
        return m_new, l, acc

    init = (jnp.full((grp * tq, 1), -jnp.inf, F32), jnp.zeros((grp * tq, 1), F32), jnp.zeros((grp * tq, HEAD), F32))
    _, l, acc = lax.fori_loop(0, n_kv, step, init)
    o = acc / l
    o_ref[...] = jnp.concatenate([o[h * tq:(h + 1) * tq] for h in range(grp)], axis=1).astype(o_ref.dtype)


DSA_TQ = 256
DSA_TK = 512


def _dsa_attn_prompt(q, kt, v, bias, *, nb, seq):
    tq = min(seq, DSA_TQ)
    tk = min(seq, DSA_TK)
    nq = seq // tq
    grp = H_B // KV_B
    return pl.pallas_call(
        functools.partial(_dsa_attn_kernel, tq=tq, tk=tk),
        grid=(nb, KV_B, nq),
        in_specs=[
            pl.BlockSpec((tq, grp * HEAD), lambda b, g, i: (b * nq + i, g)),
            pl.BlockSpec((None, HEAD, seq), lambda b, g, i: (b, g, 0)),
            pl.BlockSpec((None, seq, HEAD), lambda b, g, i: (g, b, 0)),
            pl.BlockSpec((None, tq, seq), lambda b, g, i: (b, i, 0)),
        ],
        out_specs=pl.BlockSpec((tq, grp * HEAD), lambda b, g, i: (b * nq + i, g)),
        out_shape=jax.ShapeDtypeStruct(q.shape, BF16),
        compiler_params=_params(("arbitrary", "arbitrary", "arbitrary")),
        name="dsa_attn_prompt",
    )(q, kt, v, bias)


def _dsa_attn_dec_kernel(pt_ref, q_ref, bias_ref, *refs, g, s_new, n_steps, n_past):
    kt_refs, vt_refs = refs[:g], refs[g:2 * g]
    ktn_ref, vtn_ref, o_ref, qbd_ref, m_ref, l_ref, acc_ref = refs[2 * g:]
    step = pl.program_id(1)
    nr = H_B * s_new
    grp = H_B // KV_B

    @pl.when(step == 0)
    def _():
        q = q_ref[...]
        rows = jnp.concatenate([q[:, h * HEAD:(h + 1) * HEAD] for h in range(H_B)], axis=0)
        wide = jnp.concatenate([rows] * KV_B, axis=1)
        r = _div_pow2(lax.broadcasted_iota(I32, wide.shape, 0), grp * s_new)
        c = _div_pow2(lax.broadcasted_iota(I32, wide.shape, 1), HEAD)
        qbd_ref[...] = jnp.where(r == c, wide, 0.0).astype(BF16)
        m_ref[...] = jnp.full(m_ref.shape, -jnp.inf, F32)
        l_ref[...] = jnp.zeros(l_ref.shape, F32)
        acc_ref[...] = jnp.zeros(acc_ref.shape, F32)

    def consume(ch, kt, vt, bias):
        keys = kt.shape[1]
        s = _dot(qbd_ref[...], kt.astype(BF16))
        s = (s.reshape(H_B, s_new, keys) + bias[None]).reshape(nr, keys)
        m = m_ref[ch]
        m_new = jnp.maximum(m, jnp.max(s, axis=1, keepdims=True))
        alpha = jnp.exp(m - m_new)
        p = jnp.exp(s - m_new)
        l_ref[ch] = alpha * l_ref[ch] + jnp.sum(p, axis=1, keepdims=True)
        m_ref[ch] = m_new
        acc_ref[ch] = alpha * acc_ref[ch] + _dot_nt(p.astype(BF16), vt.astype(BF16))

    @pl.when(step < n_steps - 1)
    def _():
        half = g // 2
        for ch in range(2):
            off = pl.multiple_of(step * (g * LANES) + ch * (half * LANES), half * LANES)
            pages = slice(ch * half, (ch + 1) * half)
            consume(ch, jnp.concatenate([r[...] for r in kt_refs[pages]], axis=1),
                    jnp.concatenate([r[...] for r in vt_refs[pages]], axis=1), bias_ref[:, pl.ds(off, half * LANES)])

    @pl.when(step == n_steps - 1)
    def _():
        consume(0, ktn_ref[...], vtn_ref[...], bias_ref[:, n_past:n_past + LANES])
        m = jnp.maximum(m_ref[0], m_ref[1])
        a0, a1 = jnp.exp(m_ref[0] - m), jnp.exp(m_ref[1] - m)
        o = (a0 * acc_ref[0] + a1 * acc_ref[1]) / (a0 * l_ref[0] + a1 * l_ref[1])
        outs = []
        for h in range(H_B):
            kv = h // grp
            outs.append(o[h * s_new:(h + 1) * s_new, kv * HEAD:(kv + 1) * HEAD])
        o_ref[...] = jnp.concatenate(outs, axis=1)


def _dsa_attn_decode(q, bias, kt_pool, vt_pool, page_table, kt_new, vt_new, *, s_new):
    nb, n_pages = page_table.shape
    g = min(DSA_PAGES_PER_STEP, n_pages)
    assert g % 2 == 0 and n_pages % g == 0
    n_steps = n_pages // g + 1
    n_past = n_pages * LANES
    nr = H_B * s_new
    kvw = KV_B * HEAD

    def page_map(k):
        return lambda b, i, pt: (pt[b, jnp.minimum(i, n_steps - 2) * g + k], 0, 0)

    page = lambda k: pl.BlockSpec((None, kvw, LANES), page_map(k))
    new = pl.BlockSpec((None, kvw, LANES), lambda b, i, pt: (b, 0, 0))
    grid_spec = pltpu.PrefetchScalarGridSpec(
        num_scalar_prefetch=1,
        grid=(nb, n_steps),
        in_specs=[pl.BlockSpec((s_new, D_MODEL), lambda b, i, pt: (b, 0)),
                  pl.BlockSpec((None, s_new, n_past + LANES), lambda b, i, pt: (b, 0, 0))]
        + [page(k) for k in range(g)] + [page(k) for k in range(g)] + [new, new],
        out_specs=pl.BlockSpec((s_new, D_MODEL), lambda b, i, pt: (b, 0)),
        scratch_shapes=[
            pltpu.VMEM((nr, kvw), BF16),
            pltpu.VMEM((2, nr, 1), F32),
            pltpu.VMEM((2, nr, 1), F32),
            pltpu.VMEM((2, nr, kvw), F32),
        ],
    )
    return pl.pallas_call(
        functools.partial(_dsa_attn_dec_kernel, g=g, s_new=s_new, n_steps=n_steps, n_past=n_past),
        grid_spec=grid_spec,
        out_shape=jax.ShapeDtypeStruct(q.shape, F32),
        compiler_params=_params(("arbitrary", "arbitrary")),
        name="dsa_attn_decode",
    )(page_table, q, bias, *([kt_pool] * g), *([vt_pool] * g), kt_new, vt_new)


def _outproj_ln_kernel(o_ref, w_ref, x_ref, g_ref, b_ref, *rest, route):
    h = _dot(o_ref[...].astype(BF16), w_ref[...])
    y = _layer_norm(ALPHA * x_ref[...] + h, g_ref[...], b_ref[...])
    if not route:
        rest[0][...] = y
        return
    rwh_ref, rwl_ref, rb_ref, out_ref, cls_ref = rest
    out_ref[...] = y
    yh = y.astype(BF16)
    yl = (y - yh.astype(F32)).astype(BF16)
    logits = (_dot(yh, rwh_ref[...]) + (_dot(yl, rwh_ref[...]) + _dot(yh, rwl_ref[...]))).T[:N_EXPERTS]
    aff = _sigmoid(logits)
    sel = aff + rb_ref[...]
    gi, i1, i2, _, _ = _route_choice([sel[e:e + 1] for e in range(N_EXPERTS)],
                                     [aff[e:e + 1] for e in range(N_EXPERTS)])
    lo, hi = jnp.minimum(i1, i2), jnp.maximum(i1, i2)
    pair = jnp.where(lo == 0, 0, jnp.where(lo == 1, 3, 5)) + hi - lo - 1
    cls_ref[...] = gi * PAIRS_PER_GROUP + pair


def _outproj_ln(o, w, x, g, b, router=None, *, tm):
    t, d = x.shape
    row = lambda n: pl.BlockSpec((tm, n), lambda i: (i, 0))
    full = lambda shape: pl.BlockSpec(shape, lambda i: (0, 0))
    in_specs = [row(o.shape[1]), full(w.shape), row(d), full((1, d)), full((1, d))]
    out_specs, out_shape = row(d), jax.ShapeDtypeStruct((t, d), F32)
    if router is not None:
        in_specs += [full(r.shape) for r in router]
        out_specs = [out_specs, pl.BlockSpec((1, tm), lambda i: (0, i))]
        out_shape = [out_shape, jax.ShapeDtypeStruct((1, t), I32)]
    return pl.pallas_call(
        functools.partial(_outproj_ln_kernel, route=router is not None),
        grid=(t // tm,),
        in_specs=in_specs,
        out_specs=out_specs,
        out_shape=out_shape,
        compiler_params=_params(("arbitrary",)),
        name="outproj_ln",
    )(o, w, x, g, b, *(router or ()))


def _route_choice(sel, aff):
    gs = []
    for g in range(N_GROUPS):
        a, b, c, d = sel[4 * g:4 * g + 4]
        hi1, lo1, hi2, lo2 = jnp.maximum(a, b), jnp.minimum(a, b), jnp.maximum(c, d), jnp.minimum(c, d)
        gs.append(jnp.maximum(hi1, hi2) + jnp.maximum(jnp.minimum(hi1, hi2), jnp.maximum(lo1, lo2)))
    best, gi = gs[0], jnp.zeros(gs[0].shape, I32)
    for g in range(1, N_GROUPS):
        better = gs[g] > best
        best = jnp.where(better, gs[g], best)
        gi = jnp.where(better, g, gi)

    def pick(rows):
        out = []
        for j in range(E_PER_GROUP):
            v = rows[j]
            for g in range(1, N_GROUPS):
                v = jnp.where(gi == g, rows[4 * g + j], v)
            out.append(v)
        return out

    sv, av = pick(sel), pick(aff)

    def argmax_first(vals):
        bv, bi = vals[0], jnp.zeros(vals[0].shape, I32)
        for j in range(1, E_PER_GROUP):
            better = vals[j] > bv
            bv = jnp.where(better, vals[j], bv)
            bi = jnp.where(better, j, bi)
        return bi

    i1 = argmax_first(sv)
    i2 = argmax_first([jnp.where(i1 == j, -jnp.inf, sv[j]) for j in range(E_PER_GROUP)])
    g1, g2 = av[0], av[0]
    for j in range(1, E_PER_GROUP):
        g1 = jnp.where(i1 == j, av[j], g1)
        g2 = jnp.where(i2 == j, av[j], g2)
    tot = g1 + g2
    return gi, i1, i2, g1 / tot, g2 / tot


def _route(sel, aff):
    gi, i1, i2, g1, g2 = _route_choice(sel, aff)
    comb = []
    for e in range(N_EXPERTS):
        g, j = divmod(e, E_PER_GROUP)
        in_g = gi == g
        comb.append(jnp.where(in_g & (i1 == j), g1, jnp.where(in_g & (i2 == j), g2, 0.0)))
    return comb


def _router_rows(x, rwt_hi, rwt_lo, rb):
    xh = x.astype(BF16)
    xl = (x - xh.astype(F32)).astype(BF16)
    logits = _dot_nt(rwt_hi, xh) + (_dot_nt(rwt_hi, xl) + _dot_nt(rwt_lo, xh))
    aff = _sigmoid(logits)
    sel = aff + rb
    return [sel[e:e + 1] for e in range(N_EXPERTS)], [aff[e:e + 1] for e in range(N_EXPERTS)]


def _moe_dense_kernel(x_ref, rwh_ref, rwl_ref, rb_ref, wg_ref, wu_ref, wd_ref, g_ref, b_ref, out_ref,
                      xb_ref, comb_ref, acc_ref, *, tm):
    e = pl.program_id(1)

    @pl.when(e == 0)
    def _():
        x = x_ref[...]
        xb_ref[...] = x.astype(BF16)
        sel, aff = _router_rows(x, rwh_ref[...], rwl_ref[...], rb_ref[...])
        comb = jnp.concatenate(_route(sel, aff) + [jnp.zeros((LANES - N_EXPERTS, tm), F32)], axis=0)
        comb_ref[...] = comb.T
        acc_ref[...] = jnp.zeros(acc_ref.shape, F32)

    xb = xb_ref[...]
    hg = _dot(xb, wg_ref[...])
    hu = _dot(xb, wu_ref[...])
    hdn = (hg * _sigmoid(hg)) * hu
    y = _dot(hdn.astype(BF16), wd_ref[...])
    lane = lax.broadcasted_iota(I32, (tm, LANES), 1)
    col = jnp.sum(jnp.where(lane == e, comb_ref[...], 0.0), axis=1, keepdims=True)
    acc_ref[...] += col * y

    @pl.when(e == N_EXPERTS - 1)
    def _():
        out_ref[...] = _layer_norm(ALPHA * x_ref[...] + acc_ref[...], g_ref[...], b_ref[...])


def _moe_ln(x, rwt_hi, rwt_lo, rb, wg, wu, wd, g, b, *, layer, tm):
    t, d = x.shape
    f = wg.shape[3]
    full = lambda shape: pl.BlockSpec(shape, lambda i, e: (0,) * len(shape))
    return pl.pallas_call(
        functools.partial(_moe_dense_kernel, tm=tm),
        grid=(t // tm, N_EXPERTS),
        in_specs=[
            pl.BlockSpec((tm, d), lambda i, e: (i, 0)),
            full((N_EXPERTS, d)), full((N_EXPERTS, d)), full((N_EXPERTS, 1)),
            pl.BlockSpec((None, None, d, f), lambda i, e: (layer, e, 0, 0)),
            pl.BlockSpec((None, None, d, f), lambda i, e: (layer, e, 0, 0)),
            pl.BlockSpec((None, None, f, d), lambda i, e: (layer, e, 0, 0)),
            full((1, d)), full((1, d)),
        ],
        out_specs=pl.BlockSpec((tm, d), lambda i, e: (i, 0)),
        out_shape=jax.ShapeDtypeStruct((t, d), F32),
        scratch_shapes=[pltpu.VMEM((tm, d), BF16), pltpu.VMEM((tm, LANES), F32), pltpu.VMEM((tm, d), F32)],
        compiler_params=_params(("arbitrary", "arbitrary")),
        name="moe_ln",
    )(x, rwt_hi, rwt_lo, rb, wg, wu, wd, g, b)


PAIRS_PER_GROUP = 6
N_CLASSES = N_GROUPS * PAIRS_PER_GROUP
MOE_TILE = 256


def _moe_plan_kernel(cls_ref, slot_ref, tcls_ref, nused_ref, *, tm):
    cls = cls_ref[...]
    r = cls.shape[0]
    upper = jnp.where(lax.broadcasted_iota(I32, (LANES, LANES), 0) < lax.broadcasted_iota(I32, (LANES, LANES), 1),
                      1.0, 0.0).astype(BF16)
    lower = jnp.where(lax.broadcasted_iota(I32, (r, r), 1) < lax.broadcasted_iota(I32, (r, r), 0),
                      1.0, 0.0).astype(BF16)
    tile_start = lax.broadcasted_iota(I32, (1, LANES), 1).astype(F32) * tm
    base = jnp.zeros((1, 1), F32)
    slot = jnp.zeros((r, LANES), F32)
    tcls = jnp.zeros((1, LANES), F32)
    last = jnp.zeros((1, 1), F32)
    for c in range(N_CLASSES):
        oh = cls == c
        ohf = jnp.where(oh, 1.0, 0.0)
        before_in_row = _dot(ohf.astype(BF16), upper)
        row_total = jnp.sum(ohf, axis=1, keepdims=True)
        before_rows = _dot(lower, jnp.broadcast_to(row_total, (r, LANES)).astype(BF16))
        count = jnp.sum(row_total, axis=0, keepdims=True)
        padded = jnp.ceil(count / tm) * tm
        slot = slot + jnp.where(oh, base + before_in_row + before_rows, 0.0)
        tcls = jnp.where((tile_start >= base) & (tile_start < base + padded), float(c), tcls)
        last = jnp.where(padded > 0, float(c), last)
        base = base + padded
    slot_ref[...] = slot.astype(I32)
    tcls_ref[...] = jnp.where(tile_start >= base, last, tcls).astype(I32)
    nused_ref[...] = jnp.broadcast_to(base / tm, (1, LANES)).astype(I32)


def _moe_plan(cls, *, tm):
    t = cls.shape[1]
    r = t // LANES
    full = lambda shape: pl.BlockSpec(shape, lambda: (0, 0))
    slot, tcls, nused = pl.pallas_call(
        functools.partial(_moe_plan_kernel, tm=tm),
        in_specs=[full((r, LANES))],
        out_specs=[full((r, LANES)), full((1, LANES)), full((1, LANES))],
        out_shape=[jax.ShapeDtypeStruct((r, LANES), I32)] + [jax.ShapeDtypeStruct((1, LANES), I32)] * 2,
        name="moe_plan",
    )(cls.reshape(r, LANES))
    return slot.reshape(t // tm, 1, tm), tcls[0], nused[0, :1]


def _wait_rows(src, dst, sem):
    pltpu.make_async_copy(src, dst, sem).wait()


def _moe_scatter_kernel(x_ref, slot_ref, xs_in, xs_out, stage, sems, *, tm, n_steps):
    i = pl.program_id(0)
    b = i & 1
    tile_rows = xs_out.at[pl.ds(0, tm)]

    @pl.when(i >= 2)
    def _():
        _wait_rows(stage.at[b], tile_rows, sems.at[b])

    stage[b] = x_ref[...]

    def issue(r, carry):
        pltpu.make_async_copy(stage.at[b, pl.ds(r, 1)], xs_out.at[pl.ds(slot_ref[0, r], 1)], sems.at[b]).start()
        return carry

    lax.fori_loop(0, tm, issue, 0, unroll=8)

    @pl.when(i == n_steps - 1)
    def _():
        _wait_rows(stage.at[b], tile_rows, sems.at[b])
        if n_steps > 1:
            _wait_rows(stage.at[1 - b], tile_rows, sems.at[1 - b])


def _moe_scatter(x, slot, *, tm, n_rows):
    t, d = x.shape
    n_steps = t // tm
    return pl.pallas_call(
        functools.partial(_moe_scatter_kernel, tm=tm, n_steps=n_steps),
        grid=(n_steps,),
        in_specs=[
            pl.BlockSpec((tm, d), lambda i: (i, 0)),
            pl.BlockSpec((None, 1, tm), lambda i: (i, 0, 0), memory_space=pltpu.SMEM),
            pl.BlockSpec(memory_space=pl.ANY),
        ],
        out_specs=pl.BlockSpec(memory_space=pl.ANY),
        out_shape=jax.ShapeDtypeStruct((n_rows, d), F32),
        scratch_shapes=[pltpu.VMEM((2, tm, d), F32), pltpu.SemaphoreType.DMA((2,))],
        input_output_aliases={2: 0},
        compiler_params=_params(("arbitrary",)),
        name="moe_scatter",
    )(x, slot, jnp.zeros((n_rows, d), F32))


def _class_experts(c):
    g = c // PAIRS_PER_GROUP
    pair = c - g * PAIRS_PER_GROUP
    lo = (pair >= 3).astype(I32) + (pair >= 5).astype(I32)
    hi = pair - (3 * (lo >= 1).astype(I32) + 2 * (lo >= 2).astype(I32)) + lo + 1
    return g * E_PER_GROUP + lo, g * E_PER_GROUP + hi


def _moe_sorted_kernel(tcls_ref, nused_ref, xs_ref, rwh_ref, rwl_ref, wg1, wu1, wd1, wg2, wu2, wd2,
                       g_ref, b_ref, out_ref, *, tm):
    i = pl.program_id(0)

    @pl.when(i >= nused_ref[0])
    def _():
        out_ref[...] = jnp.zeros(out_ref.shape, F32)

    @pl.when(i < nused_ref[0])
    def _():
        e_lo, e_hi = _class_experts(tcls_ref[i])
        x = xs_ref[...]
        xb = x.astype(BF16)
        xl = (x - xb.astype(F32)).astype(BF16)
        logits = _dot(xb, rwh_ref[...]) + (_dot(xl, rwh_ref[...]) + _dot(xb, rwl_ref[...]))
        aff = _sigmoid(logits)
        lane = lax.broadcasted_iota(I32, (tm, LANES), 1)
        a_lo = jnp.sum(jnp.where(lane == e_lo, aff, 0.0), axis=1, keepdims=True)
        a_hi = jnp.sum(jnp.where(lane == e_hi, aff, 0.0), axis=1, keepdims=True)
        tot = a_lo + a_hi

        def expert(wg, wu, wd):
            hg = _dot(xb, wg[...])
            return _dot(((hg * _sigmoid(hg)) * _dot(xb, wu[...])).astype(BF16), wd[...])

        acc = (a_lo / tot) * expert(wg1, wu1, wd1)
        acc = acc + (a_hi / tot) * expert(wg2, wu2, wd2)
        out_ref[...] = _layer_norm(ALPHA * x + acc, g_ref[...], b_ref[...])


def _moe_sorted(xs, tcls, nused, rw_hi, rw_lo, wg, wu, wd, g, b, *, layer, tm):
    n_rows, d = xs.shape
    f = wg.shape[3]
    full = lambda shape: pl.BlockSpec(shape, lambda i, tc, nu: (0,) * len(shape))

    def w_spec(shape, which):
        return pl.BlockSpec((None, None) + shape, lambda i, tc, nu: (layer, _class_experts(tc[i])[which], 0, 0))

    grid_spec = pltpu.PrefetchScalarGridSpec(
        num_scalar_prefetch=2,
        grid=(n_rows // tm,),
        in_specs=[pl.BlockSpec((tm, d), lambda i, tc, nu: (jnp.minimum(i, nu[0] - 1), 0)),
                  full(rw_hi.shape), full(rw_lo.shape),
                  w_spec((d, f), 0), w_spec((d, f), 0), w_spec((f, d), 0),
                  w_spec((d, f), 1), w_spec((d, f), 1), w_spec((f, d), 1),
                  full((1, d)), full((1, d))],
        out_specs=pl.BlockSpec((tm, d), lambda i, tc, nu: (i, 0)),
    )
    return pl.pallas_call(
        functools.partial(_moe_sorted_kernel, tm=tm),
        grid_spec=grid_spec,
        out_shape=jax.ShapeDtypeStruct((n_rows, d), F32),
        compiler_params=_params(("arbitrary",)),
        name="moe_sorted",
    )(tcls, nused, xs, rw_hi, rw_lo, wg, wu, wd, wg, wu, wd, g, b)


def _ple_math(x, p_ref, wg_ref, bg_ref, wp_ref):
    gate = _sigmoid(_dot(x.astype(BF16), wg_ref[...]) + bg_ref[...])
    return x + gate * _dot(p_ref[...].astype(BF16), wp_ref[...])


def _ple_gather_kernel(slot_ref, slot_next_ref, ys_ref, p_ref, wg_ref, bg_ref, wp_ref, out_ref, buf, sems,
                       *, tm, n_steps):
    i = pl.program_id(0)
    b = i & 1

    def row_copy(slots, r, bb):
        return pltpu.make_async_copy(ys_ref.at[pl.ds(slots[0, r], 1)], buf.at[bb, pl.ds(r, 1)], sems.at[bb])

    @pl.when(i == 0)
    def _():
        def body(r, carry):
            row_copy(slot_ref, r, 0).start()
            return carry
        lax.fori_loop(0, tm, body, 0, unroll=8)

    _wait_rows(ys_ref.at[pl.ds(0, tm)], buf.at[b], sems.at[b])
    for r in range(tm):
        row_copy(slot_next_ref, r, 1 - b).start()
    out_ref[...] = _ple_math(buf[b], p_ref, wg_ref, bg_ref, wp_ref)

    @pl.when(i == n_steps - 1)
    def _():
        _wait_rows(ys_ref.at[pl.ds(0, tm)], buf.at[1 - b], sems.at[1 - b])


def _ple_gather(ys, slot, p, wg, bg, wp, *, tm):
    d = ys.shape[1]
    t, pd = p.shape
    n_steps = t // tm
    full = lambda shape: pl.BlockSpec(shape, lambda i: (0, 0))
    return pl.pallas_call(
        functools.partial(_ple_gather_kernel, tm=tm, n_steps=n_steps),
        grid=(n_steps,),
        in_specs=[
            pl.BlockSpec((None, 1, tm), lambda i: (i, 0, 0), memory_space=pltpu.SMEM),
            pl.BlockSpec((None, 1, tm), lambda i: (jnp.minimum(i + 1, n_steps - 1), 0, 0), memory_space=pltpu.SMEM),
            pl.BlockSpec(memory_space=pl.ANY),
            pl.BlockSpec((tm, pd), lambda i: (i, 0)),
            full((d, d)), full((1, d)), full((pd, d)),
        ],
        out_specs=pl.BlockSpec((tm, d), lambda i: (i, 0)),
        out_shape=jax.ShapeDtypeStruct((t, d), F32),
        scratch_shapes=[pltpu.VMEM((2, tm, d), F32), pltpu.SemaphoreType.DMA((2,))],
        compiler_params=_params(("arbitrary",)),
        name="ple_gather",
    )(slot, slot, ys, p, wg, bg, wp)


def _ple_kernel(x_ref, p_ref, wg_ref, bg_ref, wp_ref, out_ref):
    out_ref[...] = _ple_math(x_ref[...], p_ref, wg_ref, bg_ref, wp_ref)


def _ple(x, p, wg, bg, wp, *, tm):
    t, d = x.shape
    pd = p.shape[1]
    full = lambda shape: pl.BlockSpec(shape, lambda i: (0, 0))
    return pl.pallas_call(
        _ple_kernel,
        grid=(t // tm,),
        in_specs=[pl.BlockSpec((tm, d), lambda i: (i, 0)), pl.BlockSpec((tm, pd), lambda i: (i, 0)),
                  full((d, d)), full((1, d)), full((pd, d))],
        out_specs=pl.BlockSpec((tm, d), lambda i: (i, 0)),
        out_shape=jax.ShapeDtypeStruct((t, d), F32),
        compiler_params=_params(("arbitrary",)),
        name="ple",
    )(x, p, wg, bg, wp)


def _pad_page_t(x, nb, s_new):
    xt = x.reshape(nb, s_new, -1).transpose(0, 2, 1)
    return jnp.pad(xt, ((0, 0), (0, 0), (0, LANES - s_new)))


def _layer_tail(o, w_out, x, p, w, layer, *, sort_tokens, tm):
    t = x.shape[0]
    ln_g, ln_b = w["ln_g"][layer], w["ln_b"][layer]
    wg, wu, wd = w["moe_w_gate"], w["moe_w_up"], w["moe_w_down"]
    ple = (w["ple_gate_w"][layer], w["ple_gate_b"][layer], w["ple_proj"][layer])
    router = (w["rwt_hi"], w["rwt_lo"], w["rb"])
    if not sort_tokens:
        x = _outproj_ln(o, w_out, x, ln_g[0], ln_b[0], tm=tm)
        x = _moe_ln(x, *router, wg, wu, wd, ln_g[1], ln_b[1], layer=layer, tm=512 if t % 512 == 0 else tm)
        return _ple(x, p, *ple, tm=tm)
    n_rows = t + N_CLASSES * MOE_TILE
    assert n_rows // MOE_TILE <= LANES, "the plan kernel lists at most 128 tiles"
    x, cls = _outproj_ln(o, w_out, x, ln_g[0], ln_b[0], (w["rw_hi"], w["rw_lo"], w["rb"]), tm=tm)
    slot, tile_cls, n_used = _moe_plan(cls, tm=MOE_TILE)
    xs = _moe_scatter(x, slot, tm=MOE_TILE, n_rows=n_rows)
    ys = _moe_sorted(xs, tile_cls, n_used, w["rw_hi"], w["rw_lo"], wg, wu, wd, ln_g[1], ln_b[1], layer=layer,
                     tm=MOE_TILE)
    return _ple_gather(ys, slot, p, *ple, tm=MOE_TILE)


def _trunk(x3, p4, pos_off, past, page_table, w):
    nb, seq, d = x3.shape
    t = nb * seq
    prompt = past is None
    x = x3.reshape(t, d)
    p = p4.reshape(DEPTH, t, -1)
    tm = min(256, t)
    tm_in = min(INPROJ_TILE, seq) if prompt else tm
    if prompt:
        tabs = _rope_tables(seq, seq, pos_off)
    else:
        tabs = _rope_tables(t, seq, pos_off)
    outs = {}

    lam_init = 0.8 - 0.6 * math.exp(-0.3 * 0)
    w_in = w["a_w_in"]
    nq = 2 * H_A * HD_A
    wq, wk, wv = w_in[:, :nq], w_in[:, nq:2 * nq], w_in[:, 2 * nq:]
    lams = [w[n] for n in ("a_lam_q1", "a_lam_k1", "a_lam_q2", "a_lam_k2")]
    if prompt:
        segs = [
            dict(w=wq, tr=False, rot=8, scale=HD_A ** -0.5, outs=[(BF16, False)]),
            dict(w=wk.T, tr=True, rot=8, scale=1.0, outs=[(F32, False), (BF16, False)]),
            dict(w=wv, tr=False, rot=0, scale=1.0, outs=[(F32, False), (BF16, False)]),
        ]
        q, kt, kt_b, v, v_b = _inproj(x, tabs, segs, seq=seq, tm=tm_in)
        o = _diff_attn_prompt(q, kt_b, v_b, lams, w["a_subln"], nb=nb, seq=seq, lam_init=lam_init)
        outs["a_k"] = kt.reshape(nb, 2 * H_A, HD_A, seq).transpose(0, 3, 1, 2)
    else:
        segs = [
            dict(w=wq, tr=False, rot=8, scale=HD_A ** -0.5, outs=[(F32, False)]),
            dict(w=wk, tr=False, rot=8, scale=1.0, outs=[(F32, False)]),
            dict(w=wv, tr=False, rot=0, scale=1.0, outs=[(F32, False)]),
        ]
        q, k, v = _inproj(x, tabs, segs, seq=seq, tm=tm)
        v_new = jnp.pad(v.reshape(nb, seq, H_A, 2 * HD_A), ((0, 0), (0, LANES - seq), (0, 0), (0, 0)))
        o = _diff_attn_decode(q, past["a_kt"], past["a_v"], page_table, _pad_page_t(k, nb, seq),
                              v_new.reshape(nb, LANES * H_A, 2 * HD_A), lams, w["a_subln"],
                              s_new=seq, lam_init=lam_init)
        outs["a_k"] = k.reshape(nb, seq, 2 * H_A, HD_A)
    outs["a_v"] = v.reshape(nb, seq, H_A, 2 * HD_A)
    x = _layer_tail(o, w["a_w_out"], x, p[0], w, 0, sort_tokens=prompt, tm=tm)

    w_in = w["b_w_in"]
    sizes = (H_B * HD_B, KV_B * HD_B, KV_B * HD_B, H_I * D_I, D_I, H_I)
    offs = [sum(sizes[:m]) for m in range(len(sizes) + 1)]
    wq, wk, wv, wqi, wki, wwi = [w_in[:, offs[m]:offs[m + 1]] for m in range(len(sizes))]
    wwi = jnp.pad(wwi, ((0, 0), (0, LANES - H_I)))
    if prompt:
        n_sel = min(TOPK_MAX, seq // 4)
        segs = [
            dict(w=wq, tr=False, rot=8, scale=HD_B ** -0.5, outs=[(BF16, False)]),
            dict(w=wk.T, tr=True, rot=8, scale=1.0, outs=[(F32, False), (BF16, False)]),
            dict(w=wv.T, tr=True, rot=0, scale=1.0, outs=[(F32, False)]),
            dict(w=wv, tr=False, rot=0, scale=1.0, outs=[(BF16, True)]),
            dict(w=wqi, tr=False, rot=16, scale=1.0, outs=[(BF16, False)]),
            dict(w=wki.T, tr=True, rot=16, scale=1.0, outs=[(F32, False), (BF16, False)]),
            dict(w=wwi, tr=False, rot=0, scale=IDX_SCALE, outs=[(F32, False)]),
        ]
        q, kt, kt_b, vt, v_b, qi, kit, kit_b, wi = _inproj(x, tabs, segs, seq=seq, tm=tm_in)
        bias = _dsa_select_prompt(qi, wi, kit_b, nb=nb, seq=seq, n_sel=n_sel)
        o = _dsa_attn_prompt(q, kt_b, v_b, bias, nb=nb, seq=seq)
        tr4 = lambda a, h: a.reshape(nb, h, HEAD, seq).transpose(0, 3, 1, 2)
        outs["b_k"] = tr4(kt, KV_B)
        outs["b_v"] = tr4(vt, KV_B)
        outs["b_ki"] = kit.transpose(0, 2, 1)
    else:
        n_past = page_table.shape[1] * LANES
        n_sel = min(TOPK_MAX, (n_past + seq) // 4)
        segs = [
            dict(w=wq, tr=False, rot=8, scale=HD_B ** -0.5, outs=[(F32, False)]),
            dict(w=wk, tr=False, rot=8, scale=1.0, outs=[(F32, False)]),
            dict(w=wv, tr=False, rot=0, scale=1.0, outs=[(F32, False)]),
            dict(w=wqi, tr=False, rot=16, scale=1.0, outs=[(F32, False)]),
            dict(w=jnp.pad(wki, ((0, 0), (0, LANES - D_I))), tr=False, rot=16, scale=1.0, outs=[(F32, False)]),
            dict(w=wwi, tr=False, rot=0, scale=IDX_SCALE, outs=[(F32, False)]),
        ]
        q, k, v, qi, ki, wi = _inproj(x, tabs, segs, seq=seq, tm=tm)
        ki = ki[:, :D_I]
        bias = _dsa_select_decode(qi, wi, past["b_kit"], page_table, _pad_page_t(ki, nb, seq), s_new=seq, n_sel=n_sel)
        o = _dsa_attn_decode(q, bias, past["b_kt"], past["b_vt"], page_table, _pad_page_t(k, nb, seq),
                             _pad_page_t(v, nb, seq), s_new=seq)
        outs["b_k"] = k.reshape(nb, seq, KV_B, HD_B)
        outs["b_v"] = v.reshape(nb, seq, KV_B, HD_B)
        outs["b_ki"] = ki.reshape(nb, seq, D_I)
    x = _layer_tail(o, w["b_w_out"], x, p[1], w, 1, sort_tokens=prompt, tm=tm)
    return x.reshape(nb, seq, d), outs


def _prep_weights(a_w_in, a_w_out, a_lam_q1, a_lam_k1, a_lam_q2, a_lam_k2, a_subln, b_w_in, b_w_out, ln_g, ln_b,
                  router_w, router_b, moe_w_gate, moe_w_up, moe_w_down, ple_proj, ple_gate_w, ple_gate_b):
    rwt = router_w.T
    rwt_hi = rwt.astype(BF16)
    rw = jnp.pad(router_w, ((0, 0), (0, LANES - N_EXPERTS)))
    rw_hi = rw.astype(BF16)
    return {
        "a_w_in": a_w_in[0].astype(BF16), "a_w_out": a_w_out[0].astype(BF16),
        "a_lam_q1": a_lam_q1, "a_lam_k1": a_lam_k1, "a_lam_q2": a_lam_q2, "a_lam_k2": a_lam_k2,
        "a_subln": a_subln,
        "b_w_in": b_w_in[0].astype(BF16), "b_w_out": b_w_out[0].astype(BF16),
        "ln_g": ln_g[:, :, None, :], "ln_b": ln_b[:, :, None, :],
        "rwt_hi": rwt_hi, "rwt_lo": (rwt - rwt_hi.astype(F32)).astype(BF16), "rb": router_b.reshape(N_EXPERTS, 1),
        "rw_hi": rw_hi, "rw_lo": (rw - rw_hi.astype(F32)).astype(BF16),
        "moe_w_gate": moe_w_gate.astype(BF16), "moe_w_up": moe_w_up.astype(BF16),
        "moe_w_down": moe_w_down.astype(BF16),
        "ple_proj": ple_proj.astype(BF16), "ple_gate_w": ple_gate_w.astype(BF16),
        "ple_gate_b": ple_gate_b[:, None, :],
    }


def kernel(x_prompt, x_sample, cache_a_k, cache_a_v, cache_b_k, cache_b_v, cache_b_kidx, page_table, p_prompt,
           p_sample, a_w_in, a_w_out, a_lam_q1, a_lam_k1, a_lam_q2, a_lam_k2, a_subln, b_w_in, b_w_out, ln_g, ln_b,
           router_w, router_b, moe_w_gate, moe_w_up, moe_w_down, ple_proj, ple_gate_w, ple_gate_b):
    w = _prep_weights(a_w_in, a_w_out, a_lam_q1, a_lam_k1, a_lam_q2, a_lam_k2, a_subln, b_w_in, b_w_out, ln_g, ln_b,
                      router_w, router_b, moe_w_gate, moe_w_up, moe_w_down, ple_proj, ple_gate_w, ple_gate_b)
    n_pool, page = cache_a_k.shape[1], cache_a_k.shape[2]
    past_len = page_table.shape[1] * page
    past = {
        "a_kt": cache_a_k[0].transpose(0, 2, 3, 1).reshape(n_pool, 2 * H_A * HD_A, page),
        "a_v": cache_a_v[0].reshape(n_pool, page * H_A, 2 * HD_A),
        "b_kt": cache_b_k[0].transpose(0, 2, 3, 1).reshape(n_pool, KV_B * HD_B, page),
        "b_vt": cache_b_v[0].transpose(0, 2, 3, 1).reshape(n_pool, KV_B * HD_B, page),
        "b_kit": cache_b_kidx[0].transpose(0, 2, 1),
    }
    y_p, op = _trunk(x_prompt, p_prompt, 0, None, None, w)
    y_s, os_ = _trunk(x_sample, p_sample, past_len, past, page_table, w)
    lead = lambda a: a[None]
    return (y_p, y_s,
            lead(op["a_k"]), lead(op["a_v"]), lead(op["b_k"]), lead(op["b_v"]), lead(op["b_ki"]),
            lead(os_["a_k"]), lead(os_["a_v"]), lead(os_["b_k"]), lead(os_["b_v"]), lead(os_["b_ki"]))
```

```python
import functools
import math

import jax
import jax.numpy as jnp
from jax import lax
from jax.experimental import pallas as pl
from jax.experimental.pallas import tpu as pltpu

F32 = jnp.float32
BF16 = jnp.bfloat16
I32 = jnp.int32

D_MODEL = 1024
DEPTH = 2
H_A = 8
HD_A = 64
H_B = 16
KV_B = 4
HD_B = 64
H_I = 8
D_I = 64
TOPK_MAX = 256
ROPE_THETA = 500000.0
N_EXPERTS = 16
N_GROUPS = 4
E_PER_GROUP = 4
D_FF = 512
ALPHA = (2 * DEPTH) ** 0.25
LN_EPS = 1e-5
RMS_EPS = 1e-5
IDX_SCALE = (H_I ** -0.5) * (D_I ** -0.5)

LANES = 128
HEAD = 64
VMEM_LIMIT = 56 * 1024 * 1024
INT_MIN = -2 ** 31
NEG_BIG = -1e30


def _params(sem):
    return pltpu.CompilerParams(dimension_semantics=sem, vmem_limit_bytes=VMEM_LIMIT)


def _dot(a, b):
    return jnp.dot(a, b, preferred_element_type=F32)


def _dot_nt(a, b):
    return lax.dot_general(a, b, (((1,), (1,)), ((), ())), preferred_element_type=F32)


def _div_pow2(x, n):
    assert n & (n - 1) == 0, n
    return x >> (n.bit_length() - 1)


def _layer_norm(y, g, b):
    mu = jnp.mean(y, axis=-1, keepdims=True)
    var = jnp.mean(jnp.square(y - mu), axis=-1, keepdims=True)
    return (y - mu) * lax.rsqrt(var + LN_EPS) * g + b


def _sigmoid(x):
    return 1.0 / (1.0 + jnp.exp(-x))


def _rope_tables_kernel(inv8l, inv16l, inv8c, inv16c, tok8, tok16, tr8, tr16, *, seq, off, tp):
    base = pl.program_id(0) * tp
    row = base + lax.broadcasted_iota(I32, (tp, LANES), 0)
    lane = lax.broadcasted_iota(I32, (tp, LANES), 1)
    pos = (off + (row & (seq - 1))).astype(F32)
    d = lane & (HEAD - 1)
    for half, inv, out in ((8, inv8l, tok8), (16, inv16l, tok16)):
        ang = pos * inv[...]
        c = jnp.cos(ang)
        s = jnp.sin(ang)
        out[0] = jnp.where(d < 2 * half, c, 1.0)
        out[1] = jnp.where(d < half, -s, 0.0)
        out[2] = jnp.where((d >= half) & (d < 2 * half), s, 0.0)
    col = base + lax.broadcasted_iota(I32, (1, tp), 1)
    posr = (off + (col & (seq - 1))).astype(F32)
    for inv, out in ((inv8c, tr8), (inv16c, tr16)):
        ang = posr * inv[...]
        out[0] = jnp.cos(ang)
        out[1] = jnp.sin(ang)


def _rope_tables(n_pos, seq, off):
    tp = min(n_pos, 256)
    lane = jnp.arange(LANES) % HEAD
    invs = []
    for half in (8, 16):
        inv = ROPE_THETA ** (-jnp.arange(half, dtype=F32) / half)
        invs.append((jnp.where(lane < 2 * half, inv[lane % half], 0.0).reshape(1, LANES), inv.reshape(half, 1)))
    full = lambda shape: pl.BlockSpec(shape, lambda i: (0,) * len(shape))
    return pl.pallas_call(
        functools.partial(_rope_tables_kernel, seq=seq, off=off, tp=tp),
        grid=(n_pos // tp,),
        in_specs=[full((1, LANES)), full((1, LANES)), full((8, 1)), full((16, 1))],
        out_specs=[
            pl.BlockSpec((3, tp, LANES), lambda i: (0, i, 0)),
            pl.BlockSpec((3, tp, LANES), lambda i: (0, i, 0)),
            pl.BlockSpec((2, 8, tp), lambda i: (0, 0, i)),
            pl.BlockSpec((2, 16, tp), lambda i: (0, 0, i)),
        ],
        out_shape=[
            jax.ShapeDtypeStruct((3, n_pos, LANES), F32),
            jax.ShapeDtypeStruct((3, n_pos, LANES), F32),
            jax.ShapeDtypeStruct((2, 8, n_pos), F32),
            jax.ShapeDtypeStruct((2, 16, n_pos), F32),
        ],
        compiler_params=_params(("arbitrary",)),
        name="rope_tables",
    )(invs[0][0], invs[1][0], invs[0][1], invs[1][1])


def _rope_tok(y, tab, half):
    c, a, b = tab[0], tab[1], tab[2]
    outs = []
    for k in range(y.shape[1] // LANES):
        yc = y[:, k * LANES:(k + 1) * LANES]
        outs.append(yc * c + pltpu.roll(yc, LANES - half, 1) * a + pltpu.roll(yc, half, 1) * b)
    return outs[0] if len(outs) == 1 else jnp.concatenate(outs, axis=1)


def _rope_tr(yt, tr, half):
    cos, sin = tr[0], tr[1]
    parts = []
    for h in range(yt.shape[0] // HEAD):
        b = h * HEAD
        x1 = yt[b:b + half]
        x2 = yt[b + half:b + 2 * half]
        parts += [x1 * cos - x2 * sin, x2 * cos + x1 * sin, yt[b + 2 * half:b + HEAD]]
    return jnp.concatenate(parts, axis=0)


def _inproj_kernel(*refs, segs):
    x_ref, tok8, tok16, tr8, tr16 = refs[:5]
    w_refs = refs[5:5 + len(segs)]
    out_refs = list(refs[5 + len(segs):])
    xb = x_ref[...].astype(BF16)
    for seg, w_ref in zip(segs, w_refs):
        if seg["tr"]:
            y = _dot_nt(w_ref[...], xb)
            if seg["rot"]:
                y = _rope_tr(y, tr8 if seg["rot"] == 8 else tr16, seg["rot"])
        else:
            y = _dot(xb, w_ref[...])
            if seg["rot"]:
                y = _rope_tok(y, tok8 if seg["rot"] == 8 else tok16, seg["rot"])
        if seg["scale"] != 1.0:
            y = y * seg["scale"]
        for dt, split in seg["outs"]:
            o_ref = out_refs.pop(0)
            if split:
                for g in range(y.shape[1] // HEAD):
                    o_ref[g] = y[:, g * HEAD:(g + 1) * HEAD].astype(dt)
            else:
                o_ref[...] = y.astype(dt)


INPROJ_TILE = 512


def _inproj(x, tabs, segs, *, seq, tm):
    t, k = x.shape
    nb = t // seq
    tps = max(seq // tm, 1)
    n_tab = tabs[0].shape[1] // tm
    in_specs = [
        pl.BlockSpec((tm, k), lambda i: (i, 0)),
        pl.BlockSpec((3, tm, LANES), lambda i: (0, i % n_tab, 0)),
        pl.BlockSpec((3, tm, LANES), lambda i: (0, i % n_tab, 0)),
        pl.BlockSpec((2, 8, tm), lambda i: (0, 0, i % n_tab)),
        pl.BlockSpec((2, 16, tm), lambda i: (0, 0, i % n_tab)),
    ]
    out_specs, out_shape = [], []
    for seg in segs:
        w = seg["w"]
        in_specs.append(pl.BlockSpec(w.shape, lambda i: (0, 0)))
        n = w.shape[0] if seg["tr"] else w.shape[1]
        for dt, split in seg["outs"]:
            if seg["tr"]:
                out_specs.append(pl.BlockSpec((None, n, tm), lambda i: (i // tps, 0, i % tps)))
                out_shape.append(jax.ShapeDtypeStruct((nb, n, seq), dt))
            elif split:
                out_specs.append(pl.BlockSpec((n // HEAD, tm, HEAD), lambda i: (0, i, 0)))
                out_shape.append(jax.ShapeDtypeStruct((n // HEAD, t, HEAD), dt))
            else:
                out_specs.append(pl.BlockSpec((tm, n), lambda i: (i, 0)))
                out_shape.append(jax.ShapeDtypeStruct((t, n), dt))
    kern_segs = tuple({k2: v for k2, v in seg.items() if k2 != "w"} for seg in segs)
    return pl.pallas_call(
        functools.partial(_inproj_kernel, segs=kern_segs),
        grid=(t // tm,),
        in_specs=in_specs,
        out_specs=out_specs,
        out_shape=out_shape,
        compiler_params=_params(("arbitrary",)),
        name="inproj",
    )(x, *tabs, *[seg["w"] for seg in segs])


def _lambda(lq1, lk1, lq2, lk2, lam_init):
    return (jnp.exp(jnp.sum(lq1[...] * lk1[...], axis=1, keepdims=True))
            - jnp.exp(jnp.sum(lq2[...] * lk2[...], axis=1, keepdims=True)) + lam_init)


def _sub_norm(o, sub, lam_init):
    return o * lax.rsqrt(jnp.mean(o * o, axis=-1, keepdims=True) + RMS_EPS) * sub * (1.0 - lam_init)


def _diff_attn_kernel(q_ref, kt_ref, v_ref, lq1, lk1, lq2, lk2, sub_ref, o_ref, *, tq, tk, lam_init):
    qi = pl.program_id(2)
    q = q_ref[...]
    ratio = tq // tk

    def step(j, carry, masked):
        off = pl.multiple_of(j * tk, tk)
        v = v_ref[pl.ds(off, tk), :]
        new = []
        for c in range(2):
            m, l, acc = carry[c]
            kt = kt_ref[c * HEAD:(c + 1) * HEAD, pl.ds(off, tk)]
            s = _dot(q[:, c * HEAD:(c + 1) * HEAD], kt)
            if masked:
                row = lax.broadcasted_iota(I32, (tq, tk), 0)
                col = lax.broadcasted_iota(I32, (tq, tk), 1)
                s = jnp.where(col <= row + (qi * tq - j * tk), s, -jnp.inf)
            m_new = jnp.maximum(m, jnp.max(s, axis=1, keepdims=True))
            alpha = jnp.exp(m - m_new)
            p = jnp.exp(s - m_new)
            l = alpha * l + jnp.sum(p, axis=1, keepdims=True)
            acc = alpha * acc + _dot(p.astype(BF16), v)
            new.append((m_new, l, acc))
        return tuple(new)

    init = tuple((jnp.full((tq, 1), -jnp.inf, F32), jnp.zeros((tq, 1), F32), jnp.zeros((tq, 2 * HEAD), F32))
                 for _ in range(2))
    carry = lax.fori_loop(0, qi * ratio, functools.partial(step, masked=False), init)
    if ratio == 1:
        carry = step(qi, carry, True)
    else:
        carry = lax.fori_loop(qi * ratio, (qi + 1) * ratio, functools.partial(step, masked=True), carry)
    (_, l0, a0), (_, l1, a1) = carry
    lam = _lambda(lq1, lk1, lq2, lk2, lam_init)
    o = a0 / l0 - lam * (a1 / l1)
    o_ref[...] = _sub_norm(o, sub_ref[...], lam_init).astype(o_ref.dtype)


DIFF_TQ = 1024
DIFF_TK = 1024


def _diff_attn_prompt(q, kt, v, lams, sub, *, nb, seq, lam_init):
    tq = min(seq, DIFF_TQ)
    tk = min(seq, DIFF_TK)
    nq = seq // tq
    small = lambda shape: pl.BlockSpec(shape, lambda b, h, i: (0, 0))
    return pl.pallas_call(
        functools.partial(_diff_attn_kernel, tq=tq, tk=tk, lam_init=lam_init),
        grid=(nb, H_A, nq),
        in_specs=[
            pl.BlockSpec((tq, 2 * HEAD), lambda b, h, i: (b * nq + i, h)),
            pl.BlockSpec((None, 2 * HEAD, seq), lambda b, h, i: (b, h, 0)),
            pl.BlockSpec((seq, 2 * HEAD), lambda b, h, i: (b, h)),
            small((1, HEAD)), small((1, HEAD)), small((1, HEAD)), small((1, HEAD)), small((1, 2 * HEAD)),
        ],
        out_specs=pl.BlockSpec((tq, 2 * HEAD), lambda b, h, i: (b * nq + i, h)),
        out_shape=jax.ShapeDtypeStruct(q.shape, BF16),
        compiler_params=_params(("arbitrary", "arbitrary", "arbitrary")),
        name="diff_attn_prompt",
    )(q, kt, v, *lams, sub)


PAGES_PER_STEP = 8
DSA_PAGES_PER_STEP = 16


def _block_diag_q(q, n_heads, width):
    s = q.shape[0]
    rows = jnp.concatenate([q] * n_heads, axis=0)
    r = _div_pow2(lax.broadcasted_iota(I32, rows.shape, 0), s)
    c = _div_pow2(lax.broadcasted_iota(I32, rows.shape, 1), HEAD)
    return jnp.where(r == c, rows, 0.0)


def _diff_attn_dec_kernel(pt_ref, q_ref, *refs, g, s_new, n_steps, lam_init):
    kt_refs, v_refs = refs[:g], refs[g:2 * g]
    ktn_ref, vn_ref, lq1, lk1, lq2, lk2, sub_ref, o_ref, qbd_ref, m_ref, l_ref, acc_ref = refs[2 * g:]
    step = pl.program_id(1)
    nr = 2 * H_A * s_new

    @pl.when(step == 0)
    def _():
        qbd_ref[...] = _block_diag_q(q_ref[...], 2 * H_A, D_MODEL).astype(BF16)
        m_ref[...] = jnp.full(m_ref.shape, -jnp.inf, F32)
        l_ref[...] = jnp.zeros(l_ref.shape, F32)
        acc_ref[...] = jnp.zeros(acc_ref.shape, F32)

    def consume(kts, page_v_refs, mask):
        kt = kts[0] if len(kts) == 1 else jnp.concatenate(kts, axis=1)
        s = _dot(qbd_ref[...], kt.astype(BF16))
        if mask is not None:
            s = jnp.where(mask, s, -jnp.inf)
        m = m_ref[...]
        m_new = jnp.maximum(m, jnp.max(s, axis=1, keepdims=True))
        alpha = jnp.exp(m - m_new)
        p = jnp.exp(s - m_new)
        l_ref[...] = alpha * l_ref[...] + jnp.sum(p, axis=1, keepdims=True)
        m_ref[...] = m_new
        pb = p.astype(BF16)

        def head_values(h):
            vs = [v_ref[pl.ds(h, LANES, stride=H_A), :] for v_ref in page_v_refs]
            return (vs[0] if len(vs) == 1 else jnp.concatenate(vs, axis=0)).astype(BF16)

        pvs = []
        hr = 2 * s_new
        for h in range(0, H_A, 2):
            both = _dot(pb[hr * h:hr * (h + 2)], jnp.concatenate([head_values(h), head_values(h + 1)], axis=1))
            pvs += [both[:hr, :2 * HEAD], both[hr:, 2 * HEAD:]]
        acc_ref[...] = alpha * acc_ref[...] + jnp.concatenate(pvs, axis=0)

    @pl.when(step < n_steps - 1)
    def _():
        consume([kt_ref[...] for kt_ref in kt_refs], v_refs, None)

    @pl.when(step == n_steps - 1)
    def _():
        key = lax.broadcasted_iota(I32, (nr, LANES), 1)
        tok = lax.broadcasted_iota(I32, (nr, LANES), 0) & (s_new - 1)
        consume([ktn_ref[...]], [vn_ref], key <= tok)
        lam = _lambda(lq1, lk1, lq2, lk2, lam_init)
        o = acc_ref[...] / l_ref[...]
        outs = []
        for h in range(H_A):
            b = 2 * s_new * h
            outs.append(_sub_norm(o[b:b + s_new] - lam * o[b + s_new:b + 2 * s_new], sub_ref[...], lam_init))
        o_ref[...] = jnp.concatenate(outs, axis=1)


def _diff_attn_decode(q, kt_pool, v_pool, page_table, kt_new, v_new, lams, sub, *, s_new, lam_init):
    nb, n_pages = page_table.shape
    g = min(PAGES_PER_STEP, n_pages)
    assert g % 2 == 0 and n_pages % g == 0
    n_steps = n_pages // g + 1
    nr = 2 * H_A * s_new

    def page_map(k):
        return lambda b, i, pt: (pt[b, jnp.minimum(i, n_steps - 2) * g + k], 0, 0)

    page = lambda k: pl.BlockSpec((None, D_MODEL, LANES), page_map(k))
    new = pl.BlockSpec((None, D_MODEL, LANES), lambda b, i, pt: (b, 0, 0))
    small = lambda shape: pl.BlockSpec(shape, lambda b, i, pt: (0, 0))
    grid_spec = pltpu.PrefetchScalarGridSpec(
        num_scalar_prefetch=1,
        grid=(nb, n_steps),
        in_specs=[pl.BlockSpec((s_new, D_MODEL), lambda b, i, pt: (b, 0))]
        + [page(k) for k in range(g)] + [page(k) for k in range(g)] + [new, new]
        + [small((1, HEAD))] * 4 + [small((1, 2 * HEAD))],
        out_specs=pl.BlockSpec((s_new, D_MODEL), lambda b, i, pt: (b, 0)),
        scratch_shapes=[
            pltpu.VMEM((nr, D_MODEL), BF16),
            pltpu.VMEM((nr, 1), F32),
            pltpu.VMEM((nr, 1), F32),
            pltpu.VMEM((nr, 2 * HEAD), F32),
        ],
    )
    return pl.pallas_call(
        functools.partial(_diff_attn_dec_kernel, g=g, s_new=s_new, n_steps=n_steps, lam_init=lam_init),
        grid_spec=grid_spec,
        out_shape=jax.ShapeDtypeStruct(q.shape, F32),
        compiler_params=_params(("arbitrary", "arbitrary")),
        name="diff_attn_decode",
    )(page_table, q, *([kt_pool] * g), *([v_pool] * g), kt_new, v_new, *lams, sub)


def _float_of_rank(u):
    key = u ^ INT_MIN
    bits = key ^ ((key >> 31) & 0x7FFFFFFF)
    return lax.bitcast_convert_type(bits, F32)


def _count(pred):
    return jnp.sum(pred.astype(I32), axis=1, keepdims=True)


def _write_bias(store, score, valid, col, n_valid, n_sel, idx_bits):
    score = jnp.where(valid, score, jnp.nan)

    def body(i, t_u):
        cand = t_u | lax.shift_left(jnp.int32(1), 31 - i)
        return jnp.where(_count(score >= _float_of_rank(cand)) >= n_sel, cand, t_u)

    t = _float_of_rank(lax.fori_loop(0, 32, body, jnp.zeros((score.shape[0], 1), I32)))
    keep_all = n_valid <= n_sel
    t = jnp.where(keep_all, -jnp.inf, t)
    tie_rows = jnp.logical_not(keep_all) & (_count(score >= t) > n_sel)
    store(jnp.where(score >= t, 0.0, NEG_BIG))

    @pl.when(jnp.max(tie_rows.astype(I32)) > 0)
    def _():
        need = n_sel - _count(score > t)
        eq = score == t

        def idx_body(i, c):
            cand = c | lax.shift_left(jnp.int32(1), idx_bits - 1 - i)
            return jnp.where(_count(eq & (col < cand)) <= need - 1, cand, c)

        c = jnp.where(tie_rows, lax.fori_loop(0, idx_bits, idx_body, jnp.zeros_like(need)), 2 ** 30)
        store(jnp.where((score > t) | (eq & (col <= c)), 0.0, NEG_BIG))


SELECT_WIDTH_STEP = 512
SCORE_CHUNK = 256


def _dsa_select_kernel(qi_ref, w_ref, kit_ref, bias_ref, *, tq, n_sel, wstep):
    seq = kit_ref.shape[1]
    j = pl.program_id(1)
    variant = lax.shift_right_logical((j + 1) * tq + wstep - 1, wstep.bit_length() - 1) - 1

    def run(width):
        qi = qi_ref[...]
        w = w_ref[...]
        chunks = []
        for c in range(width // SCORE_CHUNK):
            kit = kit_ref[:, c * SCORE_CHUNK:(c + 1) * SCORE_CHUNK]
            sc = w[:, 0:1] * jnp.maximum(_dot(qi[:, 0:HEAD], kit), 0.0)
            for h in range(1, H_I):
                sc = sc + w[:, h:h + 1] * jnp.maximum(_dot(qi[:, h * HEAD:(h + 1) * HEAD], kit), 0.0)
            chunks.append(sc)
        score = chunks[0] if len(chunks) == 1 else jnp.concatenate(chunks, axis=1)
        row = j * tq + lax.broadcasted_iota(I32, (tq, width), 0)
        col = lax.broadcasted_iota(I32, (tq, width), 1)

        def store(bias):
            bias_ref[:, :width] = bias.astype(bias_ref.dtype)

        _write_bias(store, score, col <= row, col, row[:, :1] + 1, n_sel, width.bit_length())
        if width < seq:
            bias_ref[:, width:] = jnp.full((tq, seq - width), NEG_BIG, bias_ref.dtype)

    for k in range(seq // wstep):
        pl.when(variant == k)(functools.partial(run, (k + 1) * wstep))


def _dsa_select_prompt(qi, w, kit, *, nb, seq, n_sel):
    tq = min(seq, SELECT_WIDTH_STEP)
    nq = seq // tq
    return pl.pallas_call(
        functools.partial(_dsa_select_kernel, tq=tq, n_sel=n_sel, wstep=min(seq, SELECT_WIDTH_STEP)),
        grid=(nb, nq),
        in_specs=[
            pl.BlockSpec((tq, H_I * HEAD), lambda b, i: (b * nq + i, 0)),
            pl.BlockSpec((tq, LANES), lambda b, i: (b * nq + i, 0)),
            pl.BlockSpec((None, HEAD, seq), lambda b, i: (b, 0, 0)),
        ],
        out_specs=pl.BlockSpec((None, tq, seq), lambda b, i: (b, i, 0)),
        out_shape=jax.ShapeDtypeStruct((nb, seq, seq), BF16),
        compiler_params=_params(("arbitrary", "arbitrary")),
        name="dsa_select_prompt",
    )(qi, w, kit)


def _dsa_score_dec_kernel(pt_ref, qi_ref, w_ref, *refs, g, s_new, n_steps, n_past):
    kit_refs = refs[:g]
    kitn_ref, score_ref, qs_ref, ws_ref = refs[g:]
    step = pl.program_id(1)

    @pl.when(step == 0)
    def _():
        qi = qi_ref[...]
        w = w_ref[...]
        qs_ref[...] = jnp.concatenate([qi[:, h * HEAD:(h + 1) * HEAD] for h in range(H_I)], axis=0).astype(BF16)
        ws_ref[...] = jnp.concatenate([w[:, h:h + 1] for h in range(H_I)], axis=0)

    def page_score(kit):
        d = jnp.maximum(_dot(qs_ref[...], kit.astype(BF16)), 0.0) * ws_ref[...]
        sc = d[0:s_new]
        for h in range(1, H_I):
            sc = sc + d[h * s_new:(h + 1) * s_new]
        return sc

    @pl.when(step < n_steps - 1)
    def _():
        off = pl.multiple_of(step * (g * LANES), g * LANES)
        score_ref[:, pl.ds(off, g * LANES)] = page_score(jnp.concatenate([r[...] for r in kit_refs], axis=1))

    @pl.when(step == n_steps - 1)
    def _():
        score_ref[:, n_past:n_past + LANES] = page_score(kitn_ref[...])


def _dsa_select_rows_kernel(score_ref, bias_ref, *, s_new, n_sel, n_past):
    rows, width = score_ref.shape
    col = lax.broadcasted_iota(I32, (rows, width), 1)
    tok = lax.broadcasted_iota(I32, (rows, width), 0) & (s_new - 1)

    def store(bias):
        bias_ref[...] = bias

    _write_bias(store, score_ref[...], col <= n_past + tok, col, n_past + tok[:, :1] + 1, n_sel, width.bit_length())


SELECT_PAGES_PER_STEP = 32


def _dsa_select_decode(qi, w, kit_pool, page_table, kit_new, *, s_new, n_sel):
    nb, n_pages = page_table.shape
    g = min(SELECT_PAGES_PER_STEP, n_pages)
    n_steps = n_pages // g + 1
    n_past = n_pages * LANES
    width = n_past + LANES

    def page_map(k):
        return lambda b, i, pt: (pt[b, jnp.minimum(i, n_steps - 2) * g + k], 0, 0)

    grid_spec = pltpu.PrefetchScalarGridSpec(
        num_scalar_prefetch=1,
        grid=(nb, n_steps),
        in_specs=[pl.BlockSpec((s_new, H_I * HEAD), lambda b, i, pt: (b, 0)),
                  pl.BlockSpec((s_new, LANES), lambda b, i, pt: (b, 0))]
        + [pl.BlockSpec((None, HEAD, LANES), page_map(k)) for k in range(g)]
        + [pl.BlockSpec((None, HEAD, LANES), lambda b, i, pt: (b, 0, 0))],
        out_specs=pl.BlockSpec((None, s_new, width), lambda b, i, pt: (b, 0, 0)),
        scratch_shapes=[
            pltpu.VMEM((H_I * s_new, HEAD), BF16),
            pltpu.VMEM((H_I * s_new, 1), F32),
        ],
    )
    score = pl.pallas_call(
        functools.partial(_dsa_score_dec_kernel, g=g, s_new=s_new, n_steps=n_steps, n_past=n_past),
        grid_spec=grid_spec,
        out_shape=jax.ShapeDtypeStruct((nb, s_new, width), F32),
        compiler_params=_params(("arbitrary", "arbitrary")),
        name="dsa_score_decode",
    )(page_table, qi, w, *([kit_pool] * g), kit_new)
    rows = nb * s_new
    tr = min(rows, LANES)
    bias = pl.pallas_call(
        functools.partial(_dsa_select_rows_kernel, s_new=s_new, n_sel=n_sel, n_past=n_past),
        grid=(rows // tr,),
        in_specs=[pl.BlockSpec((tr, width), lambda i: (i, 0))],
        out_specs=pl.BlockSpec((tr, width), lambda i: (i, 0)),
        out_shape=jax.ShapeDtypeStruct((rows, width), F32),
        compiler_params=_params(("arbitrary",)),
        name="dsa_select_decode",
    )(score.reshape(rows, width))
    return bias.reshape(nb, s_new, width)


def _dsa_attn_kernel(q_ref, kt_ref, v_ref, bias_ref, o_ref, *, tq, tk):
    grp = H_B // KV_B
    qi = pl.program_id(2)
    qb = q_ref[...]
    q4 = jnp.concatenate([qb[:, h * HEAD:(h + 1) * HEAD] for h in range(grp)], axis=0)
    n_kv = lax.shift_right_logical((qi + 1) * tq + tk - 1, tk.bit_length() - 1)

    def step(j, carry):
        m, l, acc = carry
        off = pl.multiple_of(j * tk, tk)
        s = _dot(q4, kt_ref[:, pl.ds(off, tk)])
        s = (s.reshape(grp, tq, tk) + bias_ref[:, pl.ds(off, tk)].astype(F32)[None]).reshape(grp * tq, tk)
        m_new = jnp.maximum(m, jnp.max(s, axis=1, keepdims=True))
        alpha = jnp.exp(m - m_new)
        p = jnp.exp(s - m_new)
        l = alpha * l + jnp.sum(p, axis=1, keepdims=True)
        acc = alpha * acc + _dot(p.astype(BF16), v_ref[pl.ds(off, tk), :])
        return m_new, l, acc

    init = (jnp.full((grp * tq, 1), -jnp.inf, F32), jnp.zeros((grp * tq, 1), F32), jnp.zeros((grp * tq, HEAD), F32))
    _, l, acc = lax.fori_loop(0, n_kv, step, init)
    o = acc / l
    o_ref[...] = jnp.concatenate([o[h * tq:(h + 1) * tq] for h in range(grp)], axis=1).astype(o_ref.dtype)


DSA_TQ = 512
DSA_TK = 512


def _dsa_attn_prompt(q, kt, v, bias, *, nb, seq):
    tq = min(seq, DSA_TQ)
    tk = min(seq, DSA_TK)
    nq = seq // tq
    grp = H_B // KV_B
    return pl.pallas_call(
        functools.partial(_dsa_attn_kernel, tq=tq, tk=tk),
        grid=(nb, KV_B, nq),
        in_specs=[
            pl.BlockSpec((tq, grp * HEAD), lambda b, g, i: (b * nq + i, g)),
            pl.BlockSpec((None, HEAD, seq), lambda b, g, i: (b, g, 0)),
            pl.BlockSpec((None, seq, HEAD), lambda b, g, i: (g, b, 0)),
            pl.BlockSpec((None, tq, seq), lambda b, g, i: (b, i, 0)),
        ],
        out_specs=pl.BlockSpec((tq, grp * HEAD), lambda b, g, i: (b * nq + i, g)),
        out_shape=jax.ShapeDtypeStruct(q.shape, BF16),
        compiler_params=_params(("arbitrary", "arbitrary", "arbitrary")),
        name="dsa_attn_prompt",
    )(q, kt, v, bias)


def _dsa_attn_dec_kernel(pt_ref, q_ref, bias_ref, *refs, g, s_new, n_steps, n_past):
    kt_refs, vt_refs = refs[:g], refs[g:2 * g]
    ktn_ref, vtn_ref, o_ref, qbd_ref, m_ref, l_ref, acc_ref = refs[2 * g:]
    step = pl.program_id(1)
    nr = H_B * s_new
    grp = H_B // KV_B

    @pl.when(step == 0)
    def _():
        q = q_ref[...]
        rows = jnp.concatenate([q[:, h * HEAD:(h + 1) * HEAD] for h in range(H_B)], axis=0)
        wide = jnp.concatenate([rows] * KV_B, axis=1)
        r = _div_pow2(lax.broadcasted_iota(I32, wide.shape, 0), grp * s_new)
        c = _div_pow2(lax.broadcasted_iota(I32, wide.shape, 1), HEAD)
        qbd_ref[...] = jnp.where(r == c, wide, 0.0).astype(BF16)
        m_ref[...] = jnp.full(m_ref.shape, -jnp.inf, F32)
        l_ref[...] = jnp.zeros(l_ref.shape, F32)
        acc_ref[...] = jnp.zeros(acc_ref.shape, F32)

    def consume(ch, kt, vt, bias):
        keys = kt.shape[1]
        s = _dot(qbd_ref[...], kt.astype(BF16))
        s = (s.reshape(H_B, s_new, keys) + bias[None]).reshape(nr, keys)
        m = m_ref[ch]
        m_new = jnp.maximum(m, jnp.max(s, axis=1, keepdims=True))
        alpha = jnp.exp(m - m_new)
        p = jnp.exp(s - m_new)
        l_ref[ch] = alpha * l_ref[ch] + jnp.sum(p, axis=1, keepdims=True)
        m_ref[ch] = m_new
        acc_ref[ch] = alpha * acc_ref[ch] + _dot_nt(p.astype(BF16), vt.astype(BF16))

    @pl.when(step < n_steps - 1)
    def _():
        half = g // 2
        for ch in range(2):
            off = pl.multiple_of(step * (g * LANES) + ch * (half * LANES), half * LANES)
            pages = slice(ch * half, (ch + 1) * half)
            consume(ch, jnp.concatenate([r[...] for r in kt_refs[pages]], axis=1),
                    jnp.concatenate([r[...] for r in vt_refs[pages]], axis=1), bias_ref[:, pl.ds(off, half * LANES)])

    @pl.when(step == n_steps - 1)
    def _():
        consume(0, ktn_ref[...], vtn_ref[...], bias_ref[:, n_past:n_past + LANES])
        m = jnp.maximum(m_ref[0], m_ref[1])
        a0, a1 = jnp.exp(m_ref[0] - m), jnp.exp(m_ref[1] - m)
        o = (a0 * acc_ref[0] + a1 * acc_ref[1]) / (a0 * l_ref[0] + a1 * l_ref[1])
        outs = []
        for h in range(H_B):
            kv = h // grp
            outs.append(o[h * s_new:(h + 1) * s_new, kv * HEAD:(kv + 1) * HEAD])
        o_ref[...] = jnp.concatenate(outs, axis=1)


def _dsa_attn_decode(q, bias, kt_pool, vt_pool, page_table, kt_new, vt_new, *, s_new):
    nb, n_pages = page_table.shape
    g = min(DSA_PAGES_PER_STEP, n_pages)
    assert g % 2 == 0 and n_pages % g == 0
    n_steps = n_pages // g + 1
    n_past = n_pages * LANES
    nr = H_B * s_new
    kvw = KV_B * HEAD

    def page_map(k):
        return lambda b, i, pt: (pt[b, jnp.minimum(i, n_steps - 2) * g + k], 0, 0)

    page = lambda k: pl.BlockSpec((None, kvw, LANES), page_map(k))
    new = pl.BlockSpec((None, kvw, LANES), lambda b, i, pt: (b, 0, 0))
    grid_spec = pltpu.PrefetchScalarGridSpec(
        num_scalar_prefetch=1,
        grid=(nb, n_steps),
        in_specs=[pl.BlockSpec((s_new, D_MODEL), lambda b, i, pt: (b, 0)),
                  pl.BlockSpec((None, s_new, n_past + LANES), lambda b, i, pt: (b, 0, 0))]
        + [page(k) for k in range(g)] + [page(k) for k in range(g)] + [new, new],
        out_specs=pl.BlockSpec((s_new, D_MODEL), lambda b, i, pt: (b, 0)),
        scratch_shapes=[
            pltpu.VMEM((nr, kvw), BF16),
            pltpu.VMEM((2, nr, 1), F32),
            pltpu.VMEM((2, nr, 1), F32),
            pltpu.VMEM((2, nr, kvw), F32),
        ],
    )
    return pl.pallas_call(
        functools.partial(_dsa_attn_dec_kernel, g=g, s_new=s_new, n_steps=n_steps, n_past=n_past),
        grid_spec=grid_spec,
        out_shape=jax.ShapeDtypeStruct(q.shape, F32),
        compiler_params=_params(("arbitrary", "arbitrary")),
        name="dsa_attn_decode",
    )(page_table, q, bias, *([kt_pool] * g), *([vt_pool] * g), kt_new, vt_new)


def _outproj_ln_kernel(o_ref, w_ref, x_ref, g_ref, b_ref, *rest, route):
    h = _dot(o_ref[...].astype(BF16), w_ref[...])
    y = _layer_norm(ALPHA * x_ref[...] + h, g_ref[...], b_ref[...])
    if not route:
        rest[0][...] = y
        return
    rwh_ref, rwl_ref, rb_ref, out_ref, cls_ref = rest
    out_ref[...] = y
    yh = y.astype(BF16)
    yl = (y - yh.astype(F32)).astype(BF16)
    logits = (_dot(yh, rwh_ref[...]) + (_dot(yl, rwh_ref[...]) + _dot(yh, rwl_ref[...]))).T[:N_EXPERTS]
    aff = _sigmoid(logits)
    sel = aff + rb_ref[...]
    gi, i1, i2, _, _ = _route_choice([sel[e:e + 1] for e in range(N_EXPERTS)],
                                     [aff[e:e + 1] for e in range(N_EXPERTS)])
    lo, hi = jnp.minimum(i1, i2), jnp.maximum(i1, i2)
    pair = jnp.where(lo == 0, 0, jnp.where(lo == 1, 3, 5)) + hi - lo - 1
    cls_ref[...] = gi * PAIRS_PER_GROUP + pair


def _outproj_ln(o, w, x, g, b, router=None, *, tm):
    t, d = x.shape
    row = lambda n: pl.BlockSpec((tm, n), lambda i: (i, 0))
    full = lambda shape: pl.BlockSpec(shape, lambda i: (0, 0))
    in_specs = [row(o.shape[1]), full(w.shape), row(d), full((1, d)), full((1, d))]
    out_specs, out_shape = row(d), jax.ShapeDtypeStruct((t, d), F32)
    if router is not None:
        in_specs += [full(r.shape) for r in router]
        out_specs = [out_specs, pl.BlockSpec((1, tm), lambda i: (0, i))]
        out_shape = [out_shape, jax.ShapeDtypeStruct((1, t), I32)]
    return pl.pallas_call(
        functools.partial(_outproj_ln_kernel, route=router is not None),
        grid=(t // tm,),
        in_specs=in_specs,
        out_specs=out_specs,
        out_shape=out_shape,
        compiler_params=_params(("arbitrary",)),
        name="outproj_ln",
    )(o, w, x, g, b, *(router or ()))


def _route_choice(sel, aff):
    gs = []
    for g in range(N_GROUPS):
        a, b, c, d = sel[4 * g:4 * g + 4]
        hi1, lo1, hi2, lo2 = jnp.maximum(a, b), jnp.minimum(a, b), jnp.maximum(c, d), jnp.minimum(c, d)
        gs.append(jnp.maximum(hi1, hi2) + jnp.maximum(jnp.minimum(hi1, hi2), jnp.maximum(lo1, lo2)))
    best, gi = gs[0], jnp.zeros(gs[0].shape, I32)
    for g in range(1, N_GROUPS):
        better = gs[g] > best
        best = jnp.where(better, gs[g], best)
        gi = jnp.where(better, g, gi)

    def pick(rows):
        out = []
        for j in range(E_PER_GROUP):
            v = rows[j]
            for g in range(1, N_GROUPS):
                v = jnp.where(gi == g, rows[4 * g + j], v)
            out.append(v)
        return out

    sv, av = pick(sel), pick(aff)

    def argmax_first(vals):
        bv, bi = vals[0], jnp.zeros(vals[0].shape, I32)
        for j in range(1, E_PER_GROUP):
            better = vals[j] > bv
            bv = jnp.where(better, vals[j], bv)
            bi = jnp.where(better, j, bi)
        return bi

    i1 = argmax_first(sv)
    i2 = argmax_first([jnp.where(i1 == j, -jnp.inf, sv[j]) for j in range(E_PER_GROUP)])
    g1, g2 = av[0], av[0]
    for j in range(1, E_PER_GROUP):
        g1 = jnp.where(i1 == j, av[j], g1)
        g2 = jnp.where(i2 == j, av[j], g2)
    tot = g1 + g2
    return gi, i1, i2, g1 / tot, g2 / tot


def _route(sel, aff):
    gi, i1, i2, g1, g2 = _route_choice(sel, aff)
    comb = []
    for e in range(N_EXPERTS):
        g, j = divmod(e, E_PER_GROUP)
        in_g = gi == g
        comb.append(jnp.where(in_g & (i1 == j), g1, jnp.where(in_g & (i2 == j), g2, 0.0)))
    return comb


def _router_rows(x, rwt_hi, rwt_lo, rb):
    xh = x.astype(BF16)
    xl = (x - xh.astype(F32)).astype(BF16)
    logits = _dot_nt(rwt_hi, xh) + (_dot_nt(rwt_hi, xl) + _dot_nt(rwt_lo, xh))
    aff = _sigmoid(logits)
    sel = aff + rb
    return [sel[e:e + 1] for e in range(N_EXPERTS)], [aff[e:e + 1] for e in range(N_EXPERTS)]


def _moe_dense_kernel(x_ref, rwh_ref, rwl_ref, rb_ref, wg_ref, wu_ref, wd_ref, g_ref, b_ref, out_ref,
                      xb_ref, comb_ref, acc_ref, *, tm):
    e = pl.program_id(1)

    @pl.when(e == 0)
    def _():
        x = x_ref[...]
        xb_ref[...] = x.astype(BF16)
        sel, aff = _router_rows(x, rwh_ref[...], rwl_ref[...], rb_ref[...])
        comb = jnp.concatenate(_route(sel, aff) + [jnp.zeros((LANES - N_EXPERTS, tm), F32)], axis=0)
        comb_ref[...] = comb.T
        acc_ref[...] = jnp.zeros(acc_ref.shape, F32)

    xb = xb_ref[...]
    hg = _dot(xb, wg_ref[...])
    hu = _dot(xb, wu_ref[...])
    hdn = (hg * _sigmoid(hg)) * hu
    y = _dot(hdn.astype(BF16), wd_ref[...])
    lane = lax.broadcasted_iota(I32, (tm, LANES), 1)
    col = jnp.sum(jnp.where(lane == e, comb_ref[...], 0.0), axis=1, keepdims=True)
    acc_ref[...] += col * y

    @pl.when(e == N_EXPERTS - 1)
    def _():
        out_ref[...] = _layer_norm(ALPHA * x_ref[...] + acc_ref[...], g_ref[...], b_ref[...])


def _moe_ln(x, rwt_hi, rwt_lo, rb, wg, wu, wd, g, b, *, layer, tm):
    t, d = x.shape
    f = wg.shape[3]
    full = lambda shape: pl.BlockSpec(shape, lambda i, e: (0,) * len(shape))
    return pl.pallas_call(
        functools.partial(_moe_dense_kernel, tm=tm),
        grid=(t // tm, N_EXPERTS),
        in_specs=[
            pl.BlockSpec((tm, d), lambda i, e: (i, 0)),
            full((N_EXPERTS, d)), full((N_EXPERTS, d)), full((N_EXPERTS, 1)),
            pl.BlockSpec((None, None, d, f), lambda i, e: (layer, e, 0, 0)),
            pl.BlockSpec((None, None, d, f), lambda i, e: (layer, e, 0, 0)),
            pl.BlockSpec((None, None, f, d), lambda i, e: (layer, e, 0, 0)),
            full((1, d)), full((1, d)),
        ],
        out_specs=pl.BlockSpec((tm, d), lambda i, e: (i, 0)),
        out_shape=jax.ShapeDtypeStruct((t, d), F32),
        scratch_shapes=[pltpu.VMEM((tm, d), BF16), pltpu.VMEM((tm, LANES), F32), pltpu.VMEM((tm, d), F32)],
        compiler_params=_params(("arbitrary", "arbitrary")),
        name="moe_ln",
    )(x, rwt_hi, rwt_lo, rb, wg, wu, wd, g, b)


PAIRS_PER_GROUP = 6
N_CLASSES = N_GROUPS * PAIRS_PER_GROUP
MOE_TILE = 256


def _moe_plan_kernel(cls_ref, slot_ref, tcls_ref, nused_ref, *, tm):
    cls = cls_ref[...]
    r = cls.shape[0]
    upper = jnp.where(lax.broadcasted_iota(I32, (LANES, LANES), 0) < lax.broadcasted_iota(I32, (LANES, LANES), 1),
                      1.0, 0.0).astype(BF16)
    lower = jnp.where(lax.broadcasted_iota(I32, (r, r), 1) < lax.broadcasted_iota(I32, (r, r), 0),
                      1.0, 0.0).astype(BF16)
    tile_start = lax.broadcasted_iota(I32, (1, LANES), 1).astype(F32) * tm
    base = jnp.zeros((1, 1), F32)
    slot = jnp.zeros((r, LANES), F32)
    tcls = jnp.zeros((1, LANES), F32)
    last = jnp.zeros((1, 1), F32)
    for c in range(N_CLASSES):
        oh = cls == c
        ohf = jnp.where(oh, 1.0, 0.0)
        before_in_row = _dot(ohf.astype(BF16), upper)
        row_total = jnp.sum(ohf, axis=1, keepdims=True)
        before_rows = _dot(lower, jnp.broadcast_to(row_total, (r, LANES)).astype(BF16))
        count = jnp.sum(row_total, axis=0, keepdims=True)
        padded = jnp.ceil(count / tm) * tm
        slot = slot + jnp.where(oh, base + before_in_row + before_rows, 0.0)
        tcls = jnp.where((tile_start >= base) & (tile_start < base + padded), float(c), tcls)
        last = jnp.where(padded > 0, float(c), last)
        base = base + padded
    slot_ref[...] = slot.astype(I32)
    tcls_ref[...] = jnp.where(tile_start >= base, last, tcls).astype(I32)
    nused_ref[...] = jnp.broadcast_to(base / tm, (1, LANES)).astype(I32)


def _moe_plan(cls, *, tm):
    t = cls.shape[1]
    r = t // LANES
    full = lambda shape: pl.BlockSpec(shape, lambda: (0, 0))
    slot, tcls, nused = pl.pallas_call(
        functools.partial(_moe_plan_kernel, tm=tm),
        in_specs=[full((r, LANES))],
        out_specs=[full((r, LANES)), full((1, LANES)), full((1, LANES))],
        out_shape=[jax.ShapeDtypeStruct((r, LANES), I32)] + [jax.ShapeDtypeStruct((1, LANES), I32)] * 2,
        name="moe_plan",
    )(cls.reshape(r, LANES))
    return slot.reshape(t // tm, 1, tm), tcls[0], nused[0, :1]


def _wait_rows(src, dst, sem):
    pltpu.make_async_copy(src, dst, sem).wait()


def _moe_scatter_kernel(x_ref, slot_ref, xs_in, xs_out, stage, sems, *, tm, n_steps):
    i = pl.program_id(0)
    b = i & 1
    tile_rows = xs_out.at[pl.ds(0, tm)]

    @pl.when(i >= 2)
    def _():
        _wait_rows(stage.at[b], tile_rows, sems.at[b])

    stage[b] = x_ref[...]

    def issue(r, carry):
        pltpu.make_async_copy(stage.at[b, pl.ds(r, 1)], xs_out.at[pl.ds(slot_ref[0, r], 1)], sems.at[b]).start()
        return carry

    lax.fori_loop(0, tm, issue, 0, unroll=8)

    @pl.when(i == n_steps - 1)
    def _():
        _wait_rows(stage.at[b], tile_rows, sems.at[b])
        if n_steps > 1:
            _wait_rows(stage.at[1 - b], tile_rows, sems.at[1 - b])


def _moe_scatter(x, slot, *, tm, n_rows):
    t, d = x.shape
    n_steps = t // tm
    return pl.pallas_call(
        functools.partial(_moe_scatter_kernel, tm=tm, n_steps=n_steps),
        grid=(n_steps,),
        in_specs=[
            pl.BlockSpec((tm, d), lambda i: (i, 0)),
            pl.BlockSpec((None, 1, tm), lambda i: (i, 0, 0), memory_space=pltpu.SMEM),
            pl.BlockSpec(memory_space=pl.ANY),
        ],
        out_specs=pl.BlockSpec(memory_space=pl.ANY),
        out_shape=jax.ShapeDtypeStruct((n_rows, d), F32),
        scratch_shapes=[pltpu.VMEM((2, tm, d), F32), pltpu.SemaphoreType.DMA((2,))],
        input_output_aliases={2: 0},
        compiler_params=_params(("arbitrary",)),
        name="moe_scatter",
    )(x, slot, jnp.zeros((n_rows, d), F32))


def _class_experts(c):
    g = c // PAIRS_PER_GROUP
    pair = c - g * PAIRS_PER_GROUP
    lo = (pair >= 3).astype(I32) + (pair >= 5).astype(I32)
    hi = pair - (3 * (lo >= 1).astype(I32) + 2 * (lo >= 2).astype(I32)) + lo + 1
    return g * E_PER_GROUP + lo, g * E_PER_GROUP + hi


def _moe_sorted_kernel(tcls_ref, nused_ref, xs_ref, rwh_ref, rwl_ref, wg1, wu1, wd1, wg2, wu2, wd2,
                       g_ref, b_ref, out_ref, *, tm):
    i = pl.program_id(0)

    @pl.when(i >= nused_ref[0])
    def _():
        out_ref[...] = jnp.zeros(out_ref.shape, F32)

    @pl.when(i < nused_ref[0])
    def _():
        e_lo, e_hi = _class_experts(tcls_ref[i])
        x = xs_ref[...]
        xb = x.astype(BF16)
        xl = (x - xb.astype(F32)).astype(BF16)
        logits = _dot(xb, rwh_ref[...]) + (_dot(xl, rwh_ref[...]) + _dot(xb, rwl_ref[...]))
        aff = _sigmoid(logits)
        lane = lax.broadcasted_iota(I32, (tm, LANES), 1)
        a_lo = jnp.sum(jnp.where(lane == e_lo, aff, 0.0), axis=1, keepdims=True)
        a_hi = jnp.sum(jnp.where(lane == e_hi, aff, 0.0), axis=1, keepdims=True)
        tot = a_lo + a_hi

        def expert(wg, wu, wd):
            hg = _dot(xb, wg[...])
            return _dot(((hg * _sigmoid(hg)) * _dot(xb, wu[...])).astype(BF16), wd[...])

        acc = (a_lo / tot) * expert(wg1, wu1, wd1)
        acc = acc + (a_hi / tot) * expert(wg2, wu2, wd2)
        out_ref[...] = _layer_norm(ALPHA * x + acc, g_ref[...], b_ref[...])


def _moe_sorted(xs, tcls, nused, rw_hi, rw_lo, wg, wu, wd, g, b, *, layer, tm):
    n_rows, d = xs.shape
    f = wg.shape[3]
    full = lambda shape: pl.BlockSpec(shape, lambda i, tc, nu: (0,) * len(shape))

    def w_spec(shape, which):
        return pl.BlockSpec((None, None) + shape, lambda i, tc, nu: (layer, _class_experts(tc[i])[which], 0, 0))

    grid_spec = pltpu.PrefetchScalarGridSpec(
        num_scalar_prefetch=2,
        grid=(n_rows // tm,),
        in_specs=[pl.BlockSpec((tm, d), lambda i, tc, nu: (jnp.minimum(i, nu[0] - 1), 0)),
                  full(rw_hi.shape), full(rw_lo.shape),
                  w_spec((d, f), 0), w_spec((d, f), 0), w_spec((f, d), 0),
                  w_spec((d, f), 1), w_spec((d, f), 1), w_spec((f, d), 1),
                  full((1, d)), full((1, d))],
        out_specs=pl.BlockSpec((tm, d), lambda i, tc, nu: (i, 0)),
    )
    return pl.pallas_call(
        functools.partial(_moe_sorted_kernel, tm=tm),
        grid_spec=grid_spec,
        out_shape=jax.ShapeDtypeStruct((n_rows, d), F32),
        compiler_params=_params(("arbitrary",)),
        name="moe_sorted",
    )(tcls, nused, xs, rw_hi, rw_lo, wg, wu, wd, wg, wu, wd, g, b)


def _ple_math(x, p_ref, wg_ref, bg_ref, wp_ref):
    gate = _sigmoid(_dot(x.astype(BF16), wg_ref[...]) + bg_ref[...])
    return x + gate * _dot(p_ref[...].astype(BF16), wp_ref[...])


def _ple_gather_kernel(slot_ref, slot_next_ref, ys_ref, p_ref, wg_ref, bg_ref, wp_ref, out_ref, buf, sems,
                       *, tm, n_steps):
    i = pl.program_id(0)
    b = i & 1

    def row_copy(slots, r, bb):
        return pltpu.make_async_copy(ys_ref.at[pl.ds(slots[0, r], 1)], buf.at[bb, pl.ds(r, 1)], sems.at[bb])

    @pl.when(i == 0)
    def _():
        def body(r, carry):
            row_copy(slot_ref, r, 0).start()
            return carry
        lax.fori_loop(0, tm, body, 0, unroll=8)

    _wait_rows(ys_ref.at[pl.ds(0, tm)], buf.at[b], sems.at[b])
    for r in range(tm):
        row_copy(slot_next_ref, r, 1 - b).start()
    out_ref[...] = _ple_math(buf[b], p_ref, wg_ref, bg_ref, wp_ref)

    @pl.when(i == n_steps - 1)
    def _():
        _wait_rows(ys_ref.at[pl.ds(0, tm)], buf.at[1 - b], sems.at[1 - b])


def _ple_gather(ys, slot, p, wg, bg, wp, *, tm):
    d = ys.shape[1]
    t, pd = p.shape
    n_steps = t // tm
    full = lambda shape: pl.BlockSpec(shape, lambda i: (0, 0))
    return pl.pallas_call(
        functools.partial(_ple_gather_kernel, tm=tm, n_steps=n_steps),
        grid=(n_steps,),
        in_specs=[
            pl.BlockSpec((None, 1, tm), lambda i: (i, 0, 0), memory_space=pltpu.SMEM),
            pl.BlockSpec((None, 1, tm), lambda i: (jnp.minimum(i + 1, n_steps - 1), 0, 0), memory_space=pltpu.SMEM),
            pl.BlockSpec(memory_space=pl.ANY),
            pl.BlockSpec((tm, pd), lambda i: (i, 0)),
            full((d, d)), full((1, d)), full((pd, d)),
        ],
        out_specs=pl.BlockSpec((tm, d), lambda i: (i, 0)),
        out_shape=jax.ShapeDtypeStruct((t, d), F32),
        scratch_shapes=[pltpu.VMEM((2, tm, d), F32), pltpu.SemaphoreType.DMA((2,))],
        compiler_params=_params(("arbitrary",)),
        name="ple_gather",
    )(slot, slot, ys, p, wg, bg, wp)


def _ple_kernel(x_ref, p_ref, wg_ref, bg_ref, wp_ref, out_ref):
    out_ref[...] = _ple_math(x_ref[...], p_ref, wg_ref, bg_ref, wp_ref)


def _ple(x, p, wg, bg, wp, *, tm):
    t, d = x.shape
    pd = p.shape[1]
    full = lambda shape: pl.BlockSpec(shape, lambda i: (0, 0))
    return pl.pallas_call(
        _ple_kernel,
        grid=(t // tm,),
        in_specs=[pl.BlockSpec((tm, d), lambda i: (i, 0)), pl.BlockSpec((tm, pd), lambda i: (i, 0)),
                  full((d, d)), full((1, d)), full((pd, d))],
        out_specs=pl.BlockSpec((tm, d), lambda i: (i, 0)),
        out_shape=jax.ShapeDtypeStruct((t, d), F32),
        compiler_params=_params(("arbitrary",)),
        name="ple",
    )(x, p, wg, bg, wp)


def _pad_page_t(x, nb, s_new):
    xt = x.reshape(nb, s_new, -1).transpose(0, 2, 1)
    return jnp.pad(xt, ((0, 0), (0, 0), (0, LANES - s_new)))


def _layer_tail(o, w_out, x, p, w, layer, *, sort_tokens, tm):
    t = x.shape[0]
    ln_g, ln_b = w["ln_g"][layer], w["ln_b"][layer]
    wg, wu, wd = w["moe_w_gate"], w["moe_w_up"], w["moe_w_down"]
    ple = (w["ple_gate_w"][layer], w["ple_gate_b"][layer], w["ple_proj"][layer])
    router = (w["rwt_hi"], w["rwt_lo"], w["rb"])
    if not sort_tokens:
        x = _outproj_ln(o, w_out, x, ln_g[0], ln_b[0], tm=tm)
        x = _moe_ln(x, *router, wg, wu, wd, ln_g[1], ln_b[1], layer=layer, tm=512 if t % 512 == 0 else tm)
        return _ple(x, p, *ple, tm=tm)
    n_rows = t + N_CLASSES * MOE_TILE
    assert n_rows // MOE_TILE <= LANES, "the plan kernel lists at most 128 tiles"
    x, cls = _outproj_ln(o, w_out, x, ln_g[0], ln_b[0], (w["rw_hi"], w["rw_lo"], w["rb"]), tm=tm)
    slot, tile_cls, n_used = _moe_plan(cls, tm=MOE_TILE)
    xs = _moe_scatter(x, slot, tm=MOE_TILE, n_rows=n_rows)
    ys = _moe_sorted(xs, tile_cls, n_used, w["rw_hi"], w["rw_lo"], wg, wu, wd, ln_g[1], ln_b[1], layer=layer,
                     tm=MOE_TILE)
    return _ple_gather(ys, slot, p, *ple, tm=MOE_TILE)


def _trunk(x3, p4, pos_off, past, page_table, w):
    nb, seq, d = x3.shape
    t = nb * seq
    prompt = past is None
    x = x3.reshape(t, d)
    p = p4.reshape(DEPTH, t, -1)
    tm = min(256, t)
    tm_in = min(INPROJ_TILE, seq) if prompt else tm
    if prompt:
        tabs = _rope_tables(seq, seq, pos_off)
    else:
        tabs = _rope_tables(t, seq, pos_off)
    outs = {}

    lam_init = 0.8 - 0.6 * math.exp(-0.3 * 0)
    w_in = w["a_w_in"]
    nq = 2 * H_A * HD_A
    wq, wk, wv = w_in[:, :nq], w_in[:, nq:2 * nq], w_in[:, 2 * nq:]
    lams = [w[n] for n in ("a_lam_q1", "a_lam_k1", "a_lam_q2", "a_lam_k2")]
    if prompt:
        segs = [
            dict(w=wq, tr=False, rot=8, scale=HD_A ** -0.5, outs=[(BF16, False)]),
            dict(w=wk.T, tr=True, rot=8, scale=1.0, outs=[(F32, False), (BF16, False)]),
            dict(w=wv, tr=False, rot=0, scale=1.0, outs=[(F32, False), (BF16, False)]),
        ]
        q, kt, kt_b, v, v_b = _inproj(x, tabs, segs, seq=seq, tm=tm_in)
        o = _diff_attn_prompt(q, kt_b, v_b, lams, w["a_subln"], nb=nb, seq=seq, lam_init=lam_init)
        outs["a_k"] = kt.reshape(nb, 2 * H_A, HD_A, seq).transpose(0, 3, 1, 2)
    else:
        segs = [
            dict(w=wq, tr=False, rot=8, scale=HD_A ** -0.5, outs=[(F32, False)]),
            dict(w=wk, tr=False, rot=8, scale=1.0, outs=[(F32, False)]),
            dict(w=wv, tr=False, rot=0, scale=1.0, outs=[(F32, False)]),
        ]
        q, k, v = _inproj(x, tabs, segs, seq=seq, tm=tm)
        v_new = jnp.pad(v.reshape(nb, seq, H_A, 2 * HD_A), ((0, 0), (0, LANES - seq), (0, 0), (0, 0)))
        o = _diff_attn_decode(q, past["a_kt"], past["a_v"], page_table, _pad_page_t(k, nb, seq),
                              v_new.reshape(nb, LANES * H_A, 2 * HD_A), lams, w["a_subln"],
                              s_new=seq, lam_init=lam_init)
        outs["a_k"] = k.reshape(nb, seq, 2 * H_A, HD_A)
    outs["a_v"] = v.reshape(nb, seq, H_A, 2 * HD_A)
    x = _layer_tail(o, w["a_w_out"], x, p[0], w, 0, sort_tokens=prompt, tm=tm)

    w_in = w["b_w_in"]
    sizes = (H_B * HD_B, KV_B * HD_B, KV_B * HD_B, H_I * D_I, D_I, H_I)
    offs = [sum(sizes[:m]) for m in range(len(sizes) + 1)]
    wq, wk, wv, wqi, wki, wwi = [w_in[:, offs[m]:offs[m + 1]] for m in range(len(sizes))]
    wwi = jnp.pad(wwi, ((0, 0), (0, LANES - H_I)))
    if prompt:
        n_sel = min(TOPK_MAX, seq // 4)
        segs = [
            dict(w=wq, tr=False, rot=8, scale=HD_B ** -0.5, outs=[(BF16, False)]),
            dict(w=wk.T, tr=True, rot=8, scale=1.0, outs=[(F32, False), (BF16, False)]),
            dict(w=wv.T, tr=True, rot=0, scale=1.0, outs=[(F32, False)]),
            dict(w=wv, tr=False, rot=0, scale=1.0, outs=[(BF16, True)]),
            dict(w=wqi, tr=False, rot=16, scale=1.0, outs=[(BF16, False)]),
            dict(w=wki.T, tr=True, rot=16, scale=1.0, outs=[(F32, False), (BF16, False)]),
            dict(w=wwi, tr=False, rot=0, scale=IDX_SCALE, outs=[(F32, False)]),
        ]
        q, kt, kt_b, vt, v_b, qi, kit, kit_b, wi = _inproj(x, tabs, segs, seq=seq, tm=tm_in)
        bias = _dsa_select_prompt(qi, wi, kit_b, nb=nb, seq=seq, n_sel=n_sel)
        o = _dsa_attn_prompt(q, kt_b, v_b, bias, nb=nb, seq=seq)
        tr4 = lambda a, h: a.reshape(nb, h, HEAD, seq).transpose(0, 3, 1, 2)
        outs["b_k"] = tr4(kt, KV_B)
        outs["b_v"] = tr4(vt, KV_B)
        outs["b_ki"] = kit.transpose(0, 2, 1)
    else:
        n_past = page_table.shape[1] * LANES
        n_sel = min(TOPK_MAX, (n_past + seq) // 4)
        segs = [
            dict(w=wq, tr=False, rot=8, scale=HD_B ** -0.5, outs=[(F32, False)]),
            dict(w=wk, tr=False, rot=8, scale=1.0, outs=[(F32, False)]),
            dict(w=wv, tr=False, rot=0, scale=1.0, outs=[(F32, False)]),
            dict(w=wqi, tr=False, rot=16, scale=1.0, outs=[(F32, False)]),
            dict(w=jnp.pad(wki, ((0, 0), (0, LANES - D_I))), tr=False, rot=16, scale=1.0, outs=[(F32, False)]),
            dict(w=wwi, tr=False, rot=0, scale=IDX_SCALE, outs=[(F32, False)]),
        ]
        q, k, v, qi, ki, wi = _inproj(x, tabs, segs, seq=seq, tm=tm)
        ki = ki[:, :D_I]
        bias = _dsa_select_decode(qi, wi, past["b_kit"], page_table, _pad_page_t(ki, nb, seq), s_new=seq, n_sel=n_sel)
        o = _dsa_attn_decode(q, bias, past["b_kt"], past["b_vt"], page_table, _pad_page_t(k, nb, seq),
                             _pad_page_t(v, nb, seq), s_new=seq)
        outs["b_k"] = k.reshape(nb, seq, KV_B, HD_B)
        outs["b_v"] = v.reshape(nb, seq, KV_B, HD_B)
        outs["b_ki"] = ki.reshape(nb, seq, D_I)
    x = _layer_tail(o, w["b_w_out"], x, p[1], w, 1, sort_tokens=prompt, tm=tm)
    return x.reshape(nb, seq, d), outs


def _prep_weights(a_w_in, a_w_out, a_lam_q1, a_lam_k1, a_lam_q2, a_lam_k2, a_subln, b_w_in, b_w_out, ln_g, ln_b,
                  router_w, router_b, moe_w_gate, moe_w_up, moe_w_down, ple_proj, ple_gate_w, ple_gate_b):
    rwt = router_w.T
    rwt_hi = rwt.astype(BF16)
    rw = jnp.pad(router_w, ((0, 0), (0, LANES - N_EXPERTS)))
    rw_hi = rw.astype(BF16)
    return {
        "a_w_in": a_w_in[0].astype(BF16), "a_w_out": a_w_out[0].astype(BF16),
        "a_lam_q1": a_lam_q1, "a_lam_k1": a_lam_k1, "a_lam_q2": a_lam_q2, "a_lam_k2": a_lam_k2,
        "a_subln": a_subln,
        "b_w_in": b_w_in[0].astype(BF16), "b_w_out": b_w_out[0].astype(BF16),
        "ln_g": ln_g[:, :, None, :], "ln_b": ln_b[:, :, None, :],
        "rwt_hi": rwt_hi, "rwt_lo": (rwt - rwt_hi.astype(F32)).astype(BF16), "rb": router_b.reshape(N_EXPERTS, 1),
        "rw_hi": rw_hi, "rw_lo": (rw - rw_hi.astype(F32)).astype(BF16),
        "moe_w_gate": moe_w_gate.astype(BF16), "moe_w_up": moe_w_up.astype(BF16),
        "moe_w_down": moe_w_down.astype(BF16),
        "ple_proj": ple_proj.astype(BF16), "ple_gate_w": ple_gate_w.astype(BF16),
        "ple_gate_b": ple_gate_b[:, None, :],
    }


def kernel(x_prompt, x_sample, cache_a_k, cache_a_v, cache_b_k, cache_b_v, cache_b_kidx, page_table, p_prompt,
           p_sample, a_w_in, a_w_out, a_lam_q1, a_lam_k1, a_lam_q2, a_lam_k2, a_subln, b_w_in, b_w_out, ln_g, ln_b,
           router_w, router_b, moe_w_gate, moe_w_up, moe_w_down, ple_proj, ple_gate_w, ple_gate_b):
    w = _prep_weights(a_w_in, a_w_out, a_lam_q1, a_lam_k1, a_lam_q2, a_lam_k2, a_subln, b_w_in, b_w_out, ln_g, ln_b,
                      router_w, router_b, moe_w_gate, moe_w_up, moe_w_down, ple_proj, ple_gate_w, ple_gate_b)
    n_pool, page = cache_a_k.shape[1], cache_a_k.shape[2]
    past_len = page_table.shape[1] * page
    past = {
        "a_kt": cache_a_k[0].transpose(0, 2, 3, 1).reshape(n_pool, 2 * H_A * HD_A, page),
        "a_v": cache_a_v[0].reshape(n_pool, page * H_A, 2 * HD_A),
        "b_kt": cache_b_k[0].transpose(0, 2, 3, 1).reshape(n_pool, KV_B * HD_B, page),
        "b_vt": cache_b_v[0].transpose(0, 2, 3, 1).reshape(n_pool, KV_B * HD_B, page),
        "b_kit": cache_b_kidx[0].transpose(0, 2, 1),
    }
    y_p, op = _trunk(x_prompt, p_prompt, 0, None, None, w)
    y_s, os_ = _trunk(x_sample, p_sample, past_len, past, page_table, w)
    lead = lambda a: a[None]
    return (y_p, y_s,
            lead(op["a_k"]), lead(op["a_v"]), lead(op["b_k"]), lead(op["b_v"]), lead(op["b_ki"]),
            lead(os_["a_k"]), lead(os_["a_v"]), lead(os_["b_k"]), lead(os_["b_v"]), lead(os_["b_ki"]))
```
